```python
import jax, jax.numpy as jnp
from jax import lax
import numpy as np

D_MODEL = 2048
BATCH = 1
SEQ = 8192
DEPTH = 2

GRID_W = 64
CTX_LEN = 256
HEAD_DIM = 128
A_GROUPS = D_MODEL // (4 * HEAD_DIM)
A_GROUP_DIM = HEAD_DIM
A_WIDTH = A_GROUPS * A_GROUP_DIM
CHUNK = 128
B_HEADS = (3 * D_MODEL) // (8 * HEAD_DIM)
B_KV_HEADS = 2
WINDOW = 128
WBLOCK = 128
C_HEADS = (3 * D_MODEL) // (8 * HEAD_DIM)
NA_ROWS = 8
NA_COLS = 16
A_IN = 2 * A_WIDTH
QB_W = B_HEADS * HEAD_DIM
QC_W = C_HEADS * HEAD_DIM
KB_W = B_KV_HEADS * HEAD_DIM
Q_END = A_IN + QB_W + QC_W
IN_WIDTH = Q_END + 2 * KB_W + 2 * QC_W
MIX_WIDTH = A_WIDTH + QB_W + QC_W
FFN_DIM = 7 * D_MODEL // 2
N_EXPERTS = 8
TOP_K = 2
ROPE_BASE = 10000.0
EPS = 1e-6
NEG_INF = -1e30

kernel_name = "hybrid_dit_sgu_swa_natten_moe"


def rms_norm(x, g):
    x32 = x.astype(jnp.float32)
    y = x32 * lax.rsqrt(jnp.mean(x32 * x32, axis=-1, keepdims=True) + EPS)
    return y.astype(x.dtype) * g


def modulate(h, shift, scale):
    return h * (1 + scale) + shift


def rope_1d(x, pos):
    d = x.shape[-1]
    n = d // 2
    inv = ROPE_BASE ** (-(2.0 / d) * jnp.arange(n, dtype=jnp.float32))
    ang = pos.astype(jnp.float32)[:, None] * inv[None, :]
    cos = jnp.cos(ang)[None, :, None, :].astype(x.dtype)
    sin = jnp.sin(ang)[None, :, None, :].astype(x.dtype)
    x1, x2 = x[..., :n], x[..., n:]
    return jnp.concatenate([x1 * cos - x2 * sin, x1 * sin + x2 * cos], axis=-1)


def axial_rope(x, row, col):
    h = x.shape[-1] // 2
    return jnp.concatenate([rope_1d(x[..., :h], row), rope_1d(x[..., h:], col)], axis=-1)


def split_q(p):
    b, t, _ = p.shape
    pa = p[..., :A_IN]
    qb = p[..., A_IN:A_IN + QB_W].reshape(b, t, B_HEADS, HEAD_DIM)
    qc = p[..., A_IN + QB_W:Q_END].reshape(b, t, C_HEADS, HEAD_DIM)
    return pa, qb, qc


def split_kv(p):
    b, t, _ = p.shape
    o = 0
    kb = p[..., o:o + KB_W].reshape(b, t, B_KV_HEADS, HEAD_DIM); o += KB_W
    vb = p[..., o:o + KB_W].reshape(b, t, B_KV_HEADS, HEAD_DIM); o += KB_W
    kc = p[..., o:o + QC_W].reshape(b, t, C_HEADS, HEAD_DIM); o += QC_W
    vc = p[..., o:o + QC_W].reshape(b, t, C_HEADS, HEAD_DIM)
    return kb, vb, kc, vc


def chunk_sgu(p, ln_g, w_s, b_s):
    b, t, _ = p.shape
    z = jax.nn.gelu(p)
    u, v = z[..., :A_WIDTH], z[..., A_WIDTH:]
    v = v.reshape(b, t // CHUNK, CHUNK, A_GROUPS, A_GROUP_DIM)
    v32 = v.astype(jnp.float32)
    mu = jnp.mean(v32, axis=-1, keepdims=True)
    var = jnp.mean(jnp.square(v32 - mu), axis=-1, keepdims=True)
    v = ((v32 - mu) * lax.rsqrt(var + EPS)).astype(p.dtype) * ln_g.reshape(A_GROUPS, A_GROUP_DIM)
    s = jnp.einsum('gij,bnjgc->bnigc', w_s, v) + b_s.T[None, None, :, :, None]
    return u * s.reshape(b, t, A_WIDTH)


def window_attn(q, k, v, k_ctx, v_ctx, sink):
    b, s, h, d = q.shape
    kvh = k.shape[2]
    g = h // kvh
    nb = s // WBLOCK
    lc = k_ctx.shape[1]
    nl = 3 * WBLOCK
    scale = d ** -0.5
    qb = q.reshape(b, nb, WBLOCK, kvh, g, d)

    def band(t):
        tp = jnp.pad(t, ((0, 0), (WBLOCK, WBLOCK), (0, 0), (0, 0))).reshape(b, nb + 2, WBLOCK, kvh, d)
        return jnp.concatenate([tp[:, :-2], tp[:, 1:-1], tp[:, 2:]], axis=2)

    kband, vband = band(k), band(v)
    qi = jnp.arange(WBLOCK)
    kj = jnp.arange(nl)
    rel = (kj[None, :] - WBLOCK) - qi[:, None]
    kpos = (jnp.arange(nb)[:, None] - 1) * WBLOCK + kj[None, :]
    mask = (jnp.abs(rel) <= WINDOW)[None] & ((kpos >= 0) & (kpos < s))[:, None, :]
    s_loc = jnp.einsum('bnqkgd,bnjkd->bnkgqj', qb, kband).astype(jnp.float32) * scale
    s_loc = jnp.where(mask[None, :, None, None], s_loc, NEG_INF)
    s_ctx = jnp.einsum('bnqkgd,blkd->bnkgql', qb, k_ctx).astype(jnp.float32) * scale
    s_sink = jnp.broadcast_to(sink.astype(jnp.float32).reshape(kvh, g)[None, None, :, :, None, None],
                              s_ctx.shape[:-1] + (1,))
    p = jax.nn.softmax(jnp.concatenate([s_loc, s_ctx, s_sink], axis=-1), axis=-1).astype(v.dtype)
    o = (jnp.einsum('bnkgqj,bnjkd->bnqkgd', p[..., :nl], vband)
         + jnp.einsum('bnkgql,blkd->bnqkgd', p[..., nl:nl + lc], v_ctx))
    return o.reshape(b, s, h * d)


def neighborhood_attn(q, k, v, k_ctx, v_ctx, rpb):
    b, s, h, d = q.shape
    rows = s // GRID_W
    kh = min(NA_ROWS, rows)
    scale = d ** -0.5
    qg = q.reshape(b, rows, GRID_W, h, d)
    kg = k.reshape(b, rows, GRID_W, h, d)
    vg = v.reshape(b, rows, GRID_W, h, d)
    r = jnp.arange(rows)
    r0 = jnp.clip(r - kh // 2, 0, rows - kh)
    ridx = r0[:, None] + jnp.arange(kh)[None, :]
    kn = kg[:, ridx]
    vn = vg[:, ridx].reshape(b, rows, kh * GRID_W, h, d)
    cq = jnp.arange(GRID_W)
    c0 = jnp.clip(cq - NA_COLS // 2, 0, GRID_W - NA_COLS)
    cmask = (cq[None, :] >= c0[:, None]) & (cq[None, :] < c0[:, None] + NA_COLS)
    roff = ridx - r[:, None] + (NA_ROWS - 1)
    coff = jnp.clip(cq[None, :] - cq[:, None] + (NA_COLS - 1), 0, 2 * NA_COLS - 2)
    bias = rpb.astype(jnp.float32)[:, roff][..., coff]
    bias = bias.transpose(0, 1, 3, 2, 4)
    s_nb = jnp.einsum('brqhd,brjchd->bhrqjc', qg, kn).astype(jnp.float32) * scale
    s_nb = jnp.where(cmask[None, None, None, :, None, :], s_nb + bias[None], NEG_INF)
    s_nb = s_nb.reshape(b, h, rows, GRID_W, kh * GRID_W)
    s_ctx = jnp.einsum('brqhd,blhd->bhrql', qg, k_ctx).astype(jnp.float32) * scale
    nn_ = kh * GRID_W
    p = jax.nn.softmax(jnp.concatenate([s_nb, s_ctx], axis=-1), axis=-1).astype(v.dtype)
    o = (jnp.einsum('bhrqn,brnhd->brqhd', p[..., :nn_], vn)
         + jnp.einsum('bhrql,blhd->brqhd', p[..., nn_:], v_ctx))
    return o.reshape(b, s, h * d)


def ctx_self_attn(q, k, v, sink):
    b, l, h, d = q.shape
    kvh = k.shape[2]
    g = h // kvh
    qg = q.reshape(b, l, kvh, g, d)
    s = jnp.einsum('bqkgd,bjkd->bkgqj', qg, k).astype(jnp.float32) * (d ** -0.5)
    if sink is not None:
        s_sink = jnp.broadcast_to(sink.astype(jnp.float32).reshape(kvh, g)[None, :, :, None, None],
                                  s.shape[:-1] + (1,))
        s = jnp.concatenate([s, s_sink], axis=-1)
    p = jax.nn.softmax(s, axis=-1)[..., :l].astype(v.dtype)
    o = jnp.einsum('bkgqj,bjkd->bqkgd', p, v)
    return o.reshape(b, l, h * d)


def token_mix(h_lat, h_ctx, w_in, ln_g, sgu_w, sgu_b, sink, rpb, w_out, with_ctx_out):
    b, s, _ = h_lat.shape
    t = jnp.arange(s)
    row, col = t // GRID_W, t % GRID_W
    p_lat = h_lat @ w_in
    pa, qb, qc = split_q(p_lat[..., :Q_END])
    kb, vb, kc, vc = split_kv(p_lat[..., Q_END:])
    qb = axial_rope(qb, row, col)
    kb = axial_rope(kb, row, col)
    if with_ctx_out:
        p_ctx = h_ctx @ w_in
        pa_c, qb_c, qc_c = split_q(p_ctx[..., :Q_END])
        kb_c, vb_c, kc_c, vc_c = split_kv(p_ctx[..., Q_END:])
    else:
        kb_c, vb_c, kc_c, vc_c = split_kv(h_ctx @ w_in[:, Q_END:])
    y_lat = jnp.concatenate([
        chunk_sgu(pa, ln_g, sgu_w, sgu_b),
        window_attn(qb, kb, vb, kb_c, vb_c, sink),
        neighborhood_attn(qc, kc, vc, kc_c, vc_c, rpb),
    ], axis=-1) @ w_out
    y_ctx = None
    if with_ctx_out:
        y_ctx = jnp.concatenate([
            chunk_sgu(pa_c, ln_g, sgu_w, sgu_b),
            ctx_self_attn(qb_c, kb_c, vb_c, sink),
            ctx_self_attn(qc_c, kc_c, vc_c, None),
        ], axis=-1) @ w_out
    return y_lat, y_ctx


def swiglu(h, w_gu, w_down):
    gt, up = jnp.split(h @ w_gu, 2, axis=-1)
    return (jax.nn.silu(gt) * up) @ w_down


def moe_swiglu(h, w_router, w_gu, w_down):
    logits = (h @ w_router).astype(jnp.float32)
    top_v, top_i = lax.top_k(logits, TOP_K)
    weights = jax.nn.softmax(top_v, axis=-1)
    gate = jnp.sum(jax.nn.one_hot(top_i, N_EXPERTS, dtype=jnp.float32) * weights[..., None], axis=-2)
    gate = gate.astype(h.dtype)
    out = jnp.zeros_like(h)
    for e in range(N_EXPERTS):
        out = out + gate[..., e:e + 1] * swiglu(h, w_gu[e], w_down[e])
    return out


def channel_mix(h, l, ffn_w_gu, ffn_w_down, moe_router, moe_w_gu, moe_w_down):
    if l % 2 == 0:
        return swiglu(h, ffn_w_gu[l // 2], ffn_w_down[l // 2])
    return moe_swiglu(h, moe_router[l // 2], moe_w_gu[l // 2], moe_w_down[l // 2])


def setup_inputs(seed: int = 0) -> dict:
    key = jax.random.key(seed)
    ks = jax.random.split(key, 22)
    f32 = jnp.float32
    n_dense = (DEPTH + 1) // 2
    n_moe = DEPTH // 2

    def nrm(k, shape, scale):
        return jax.random.normal(k, shape, f32) * scale

    def gain(k, shape):
        return 1.0 + 0.02 * jax.random.normal(k, shape, f32)

    return {
        "x": nrm(ks[0], (BATCH, SEQ, D_MODEL), 1.0),
        "c": nrm(ks[1], (BATCH, D_MODEL), 1.0),
        "ctx": nrm(ks[2], (BATCH, CTX_LEN, D_MODEL), 1.0),
        "c_ctx": nrm(ks[3], (D_MODEL,), 1.0),
        "w_mod": nrm(ks[4], (DEPTH, D_MODEL, 6 * D_MODEL), D_MODEL ** -0.5),
        "b_mod": nrm(ks[5], (DEPTH, 6 * D_MODEL), 0.02),
        "g_mix_pre": gain(ks[6], (DEPTH, D_MODEL)),
        "g_mix_post": gain(ks[7], (DEPTH, D_MODEL)),
        "g_ffn_pre": gain(ks[8], (DEPTH, D_MODEL)),
        "g_ffn_post": gain(ks[9], (DEPTH, D_MODEL)),
        "w_in": nrm(ks[10], (DEPTH, D_MODEL, IN_WIDTH), D_MODEL ** -0.5),
        "sgu_ln_g": gain(ks[11], (DEPTH, A_WIDTH)),
        "sgu_w": nrm(ks[12], (DEPTH, A_GROUPS, CHUNK, CHUNK), CHUNK ** -0.5),
        "sgu_b": gain(ks[13], (DEPTH, A_GROUPS, CHUNK)),
        "attn_sink": nrm(ks[14], (DEPTH, B_HEADS), 0.5),
        "na_rpb": nrm(ks[15], (DEPTH, C_HEADS, 2 * NA_ROWS - 1, 2 * NA_COLS - 1), 0.1),
        "w_out": nrm(ks[16], (DEPTH, MIX_WIDTH, D_MODEL), MIX_WIDTH ** -0.5),
        "ffn_w_gu": nrm(ks[17], (n_dense, D_MODEL, 2 * FFN_DIM), D_MODEL ** -0.5),
        "ffn_w_down": nrm(ks[18], (n_dense, FFN_DIM, D_MODEL), FFN_DIM ** -0.5),
        "moe_router": nrm(ks[19], (n_moe, D_MODEL, N_EXPERTS), D_MODEL ** -0.5),
        "moe_w_gu": nrm(ks[20], (n_moe, N_EXPERTS, D_MODEL, 2 * FFN_DIM), D_MODEL ** -0.5),
        "moe_w_down": nrm(ks[21], (n_moe, N_EXPERTS, FFN_DIM, D_MODEL), FFN_DIM ** -0.5),
    }


def reference(x, c, ctx, c_ctx, w_mod, b_mod, g_mix_pre, g_mix_post, g_ffn_pre, g_ffn_post,
              w_in, sgu_ln_g, sgu_w, sgu_b, attn_sink, na_rpb, w_out,
              ffn_w_gu, ffn_w_down, moe_router, moe_w_gu, moe_w_down):
    z = ctx
    for l in range(DEPTH):
        last = l == DEPTH - 1
        mod_lat = (jax.nn.silu(c) @ w_mod[l] + b_mod[l])[:, None, :]
        mod_ctx = (jax.nn.silu(c_ctx) @ w_mod[l] + b_mod[l])[None, None, :]
        sh1, sc1, g1, sh2, sc2, g2 = jnp.split(mod_lat, 6, axis=-1)
        sh1c, sc1c, g1c, sh2c, sc2c, g2c = jnp.split(mod_ctx, 6, axis=-1)

        h_lat = modulate(rms_norm(x, g_mix_pre[l]), sh1, sc1)
        h_ctx = modulate(rms_norm(z, g_mix_pre[l]), sh1c, sc1c)
        y_lat, y_ctx = token_mix(h_lat, h_ctx, w_in[l], sgu_ln_g[l], sgu_w[l], sgu_b[l],
                                 attn_sink[l], na_rpb[l], w_out[l], not last)
        x = x + g1 * rms_norm(y_lat, g_mix_post[l])

        h_lat = modulate(rms_norm(x, g_ffn_pre[l]), sh2, sc2)
        x = x + g2 * rms_norm(channel_mix(h_lat, l, ffn_w_gu, ffn_w_down, moe_router, moe_w_gu, moe_w_down),
                              g_ffn_post[l])

        if not last:
            z = z + g1c * rms_norm(y_ctx, g_mix_post[l])
            h_ctx = modulate(rms_norm(z, g_ffn_pre[l]), sh2c, sc2c)
            z = z + g2c * rms_norm(channel_mix(h_ctx, l, ffn_w_gu, ffn_w_down, moe_router, moe_w_gu, moe_w_down),
                                   g_ffn_post[l])
    return x
```

```python
import functools

import numpy as np
import jax
import jax.numpy as jnp
from jax import lax
from jax.experimental import pallas as pl
from jax.experimental.pallas import tpu as pltpu

F32 = jnp.float32
BF16 = jnp.bfloat16

HEAD_DIM = 128
GRID_W = 64
CHUNK = 128
WINDOW_BLOCK = 128
B_HEADS = 6
B_KV_HEADS = 2
C_HEADS = 6
NA_ROWS = 8
NA_COLS = 16
N_EXPERTS = 8
ROPE_BASE = 10000.0
EPS = 1e-6
NEG_INF = -1e30

V7X_VMEM_BYTES = 64 * 1024 * 1024
VMEM_LIMIT = V7X_VMEM_BYTES - 6 * 1024 * 1024
LANES = 128

A_WIDTH = 512
QB_COL = 2 * A_WIDTH // HEAD_DIM
QC_COL = QB_COL + B_HEADS
KB_COL = QC_COL + C_HEADS
VB_COL = KB_COL + B_KV_HEADS
KC_COL = VB_COL + B_KV_HEADS
VC_COL = KC_COL + C_HEADS
IN_COLS = VC_COL + C_HEADS


def _params(sem, vmem=VMEM_LIMIT):
    return pltpu.CompilerParams(dimension_semantics=sem, vmem_limit_bytes=vmem)


def _rms(x):
    return x * lax.rsqrt(jnp.mean(x * x, axis=-1, keepdims=True) + EPS)


def _pick_rows(v2, row0, tm, n_lat):
    rows = row0 + lax.broadcasted_iota(jnp.int32, (tm, 1), 0)
    return jnp.where(rows >= n_lat, v2[1:2, :], v2[0:1, :])


def _norm_mod(x, g, sh2, sc2, row0, tm, n_lat):
    y = _rms(x) * g
    return y * (1.0 + _pick_rows(sc2, row0, tm, n_lat)) + _pick_rows(sh2, row0, tm, n_lat)


def _dot(a, b):
    return jnp.dot(a, b, preferred_element_type=F32)


def _dot_nt(a, b):
    return lax.dot_general(a, b, (((1,), (1,)), ((), ())), preferred_element_type=F32)


def _mod_kernel(c_ref, w_ref, b_ref, o_ref):
    a = jax.nn.silu(c_ref[...]).astype(BF16)
    o_ref[...] = _dot(a, w_ref[...].astype(BF16)) + b_ref[...]


def _mod_call(c8, w_mod, b_mod):
    depth, d, n = w_mod.shape
    tn = 1536
    return pl.pallas_call(
        _mod_kernel,
        grid=(depth, n // tn),
        in_specs=[
            pl.BlockSpec((8, d), lambda l, j: (0, 0)),
            pl.BlockSpec((None, d, tn), lambda l, j: (l, 0, j)),
            pl.BlockSpec((None, 1, tn), lambda l, j: (l, 0, j)),
        ],
        out_specs=pl.BlockSpec((None, 8, tn), lambda l, j: (l, 0, j)),
        out_shape=jax.ShapeDtypeStruct((depth, 8, n), F32),
        compiler_params=_params(("parallel", "parallel")),
        name="modulation",
    )(c8, w_mod, b_mod.reshape(depth, 1, n))


def _rope_head(x, cos, sin):
    lane = lax.broadcasted_iota(jnp.int32, x.shape, 1)
    first = (lane & 63) < 32
    partner = jnp.where(first, pltpu.roll(x, 96, 1), pltpu.roll(x, 32, 1))
    return x * cos + partner * sin


def _inproj_kernel(x_ref, g_ref, sh_ref, sc_ref, w_ref, cos_ref, sin_ref, o_ref, h_ref, *, tm, n_lat):
    i = pl.program_id(0)
    j = pl.program_id(1)

    @pl.when(j == 0)
    def _():
        h = _norm_mod(x_ref[...], g_ref[...], sh_ref[...], sc_ref[...], i * tm, tm, n_lat)
        h_ref[...] = h.astype(BF16)

    acc = _dot(h_ref[...], w_ref[...].astype(BF16))

    def rope(a):
        return _rope_head(a, cos_ref[...], sin_ref[...])

    def store(kinds):
        for hd, kind in enumerate(kinds):
            a = acc[:, hd * HEAD_DIM:(hd + 1) * HEAD_DIM]
            if kind == "gelu":
                a = jax.nn.gelu(a, approximate=True)
            elif kind == "rope":
                a = rope(a)
            o_ref[:, hd * HEAD_DIM:(hd + 1) * HEAD_DIM] = a.astype(BF16)

    @pl.when(j < 2)
    def _():
        store(("gelu",) * 4)

    @pl.when(j == 2)
    def _():
        store(("rope",) * 4)

    @pl.when((j == 3) | (j == 5))
    def _():
        store(("rope", "rope", "none", "none"))

    @pl.when((j == 4) | (j > 5))
    def _():
        store(("none",) * 4)


def _inproj_call(xa, g, sh2, sc2, w_all, layer, cos, sin, *, tm, n_lat):
    t, d = xa.shape
    n = w_all.shape[2]
    tn = 4 * HEAD_DIM
    assert t % tm == 0 and n == IN_COLS * HEAD_DIM
    kern = functools.partial(_inproj_kernel, tm=tm, n_lat=n_lat)
    return pl.pallas_call(
        kern,
        grid=(t // tm, n // tn),
        in_specs=[
            pl.BlockSpec((tm, d), lambda i, j: (i, 0)),
            pl.BlockSpec((1, d), lambda i, j: (0, 0)),
            pl.BlockSpec((2, d), lambda i, j: (0, 0)),
            pl.BlockSpec((2, d), lambda i, j: (0, 0)),
            pl.BlockSpec((None, d, tn), lambda i, j: (layer, 0, j)),
            pl.BlockSpec((tm, HEAD_DIM), lambda i, j: (i, 0)),
            pl.BlockSpec((tm, HEAD_DIM), lambda i, j: (i, 0)),
        ],
        out_specs=pl.BlockSpec((tm, tn), lambda i, j: (i, j)),
        out_shape=jax.ShapeDtypeStruct((t, n), BF16),
        scratch_shapes=[pltpu.VMEM((tm, d), BF16)],
        compiler_params=_params(("parallel", "arbitrary")),
        name="in_projection",
    )(xa, g.reshape(1, d), sh2, sc2, w_all, cos, sin)


def _rope_tables(n_lat, n_ctx):
    t = jnp.arange(n_lat)
    n = HEAD_DIM // 4
    inv = ROPE_BASE ** (-(2.0 / (HEAD_DIM // 2)) * jnp.arange(n, dtype=F32))
    ang_r = (t // GRID_W).astype(F32)[:, None] * inv[None, :]
    ang_c = (t % GRID_W).astype(F32)[:, None] * inv[None, :]
    cos = jnp.concatenate([jnp.cos(ang_r)] * 2 + [jnp.cos(ang_c)] * 2, axis=1)
    sin = jnp.concatenate([-jnp.sin(ang_r), jnp.sin(ang_r), -jnp.sin(ang_c), jnp.sin(ang_c)], axis=1)
    cos = jnp.concatenate([cos, jnp.ones((n_ctx, HEAD_DIM), F32)], axis=0)
    sin = jnp.concatenate([sin, jnp.zeros((n_ctx, HEAD_DIM), F32)], axis=0)
    return cos, sin


def _sgu_kernel(u_ref, v_ref, lng_ref, w_ref, b_ref, o_ref, *, chunks):
    for c in range(chunks):
        rows = slice(c * CHUNK, (c + 1) * CHUNK)
        for g in range(A_WIDTH // HEAD_DIM):
            cols = slice(g * HEAD_DIM, (g + 1) * HEAD_DIM)
            v = v_ref[rows, cols].astype(F32)
            mu = jnp.mean(v, axis=-1, keepdims=True)
            var = jnp.mean(jnp.square(v - mu), axis=-1, keepdims=True)
            vn = (v - mu) * lax.rsqrt(var + EPS) * lng_ref[:, cols]
            s = _dot(w_ref[g].astype(BF16), vn.astype(BF16)) + b_ref[g]
            o_ref[rows, cols] = (u_ref[rows, cols].astype(F32) * s).astype(BF16)


def _sgu_call(p, ln_g, w_s, b_s, *, chunks):
    t = p.shape[0]
    rows = chunks * CHUNK
    assert t % rows == 0
    groups = A_WIDTH // HEAD_DIM
    b_full = jnp.broadcast_to(b_s[:, :, None], (groups, CHUNK, HEAD_DIM))
    kern = functools.partial(_sgu_kernel, chunks=chunks)
    return pl.pallas_call(
        kern,
        grid=(t // rows,),
        in_specs=[
            pl.BlockSpec((rows, A_WIDTH), lambda i: (i, 0)),
            pl.BlockSpec((rows, A_WIDTH), lambda i: (i, 1)),
            pl.BlockSpec((1, A_WIDTH), lambda i: (0, 0)),
            pl.BlockSpec((groups, CHUNK, CHUNK), lambda i: (0, 0, 0)),
            pl.BlockSpec((groups, CHUNK, HEAD_DIM), lambda i: (0, 0, 0)),
        ],
        out_specs=pl.BlockSpec((rows, A_WIDTH), lambda i: (i, 0)),
        out_shape=jax.ShapeDtypeStruct((t, A_WIDTH), BF16),
        compiler_params=_params(("parallel",)),
        name="spatial_gating",
    )(p, p, ln_g.reshape(1, A_WIDTH), w_s, b_full)


WIN_Q_BLOCKS = 4


def _softmax_pv(s_loc, s_ctx, sink_col, v_loc, v_ctx):
    m = jnp.maximum(jnp.max(s_loc, axis=1, keepdims=True), jnp.max(s_ctx, axis=1, keepdims=True))
    if sink_col is not None:
        m = jnp.maximum(m, sink_col)
    p_loc = jnp.exp(s_loc - m)
    p_ctx = jnp.exp(s_ctx - m)
    den = jnp.sum(p_loc, axis=1, keepdims=True) + jnp.sum(p_ctx, axis=1, keepdims=True)
    if sink_col is not None:
        den = den + jnp.exp(sink_col - m)
    o = _dot(p_loc.astype(BF16), v_loc) + _dot(p_ctx.astype(BF16), v_ctx)
    return o / den


def _win_kernel(sink_ref, q0, q1, q2, kp, km, kn, vp, vm, vn, kc, vc, o_ref, kcat, vcat, *, n_lat):
    kv = pl.program_id(0)
    s = pl.program_id(1)
    wb = WINDOW_BLOCK
    main = WIN_Q_BLOCKS * wb
    kcat[0:wb] = kp[...]
    kcat[wb:wb + main] = km[...]
    kcat[wb + main:2 * wb + main] = kn[...]
    vcat[0:wb] = vp[...]
    vcat[wb:wb + main] = vm[...]
    vcat[wb + main:2 * wb + main] = vn[...]
    scale = HEAD_DIM ** -0.5
    g = B_HEADS // B_KV_HEADS
    qs = (q0, q1, q2)
    row =lax.broadcasted_iota(jnp.int32, (g * wb, 3 * wb), 0) & (wb - 1)
    col = lax.broadcasted_iota(jnp.int32, (g * wb, 3 * wb), 1)
    rel = col - wb - row
    band = (rel >= -wb) & (rel <= wb)
    sink_col = jnp.concatenate(
        [jnp.full((wb, 1), sink_ref[kv * g + gi], F32) for gi in range(g)], axis=0)
    for b in range(WIN_Q_BLOCKS):
        n = s * WIN_Q_BLOCKS + b
        rows = slice(b * wb, (b + 1) * wb)
        q3 = jnp.concatenate([qs[gi][rows, :] for gi in range(g)], axis=0)
        keys = kcat[b * wb:(b + 3) * wb, :]
        vals = vcat[b * wb:(b + 3) * wb, :]
        kpos = (n - 1) * wb + col
        valid = band & (kpos >= 0) & (kpos < n_lat)
        s_loc = jnp.where(valid, _dot_nt(q3, keys) * scale, NEG_INF)
        s_ctx = _dot_nt(q3, kc[...]) * scale
        o = _softmax_pv(s_loc, s_ctx, sink_col, vals, vc[...])
        for gi in range(g):
            o_ref[rows, gi * HEAD_DIM:(gi + 1) * HEAD_DIM] = o[gi * wb:(gi + 1) * wb, :].astype(BF16)


def _win_call(p, sink, *, n_lat, n_ctx):
    wb = WINDOW_BLOCK
    main = WIN_Q_BLOCKS * wb
    assert n_lat % main == 0 and n_lat % n_ctx == 0
    nsb = n_lat // main
    nb = n_lat // wb
    g = B_HEADS // B_KV_HEADS
    ctx_blk = n_lat // n_ctx

    def qspec(gi):
        return pl.BlockSpec((main, HEAD_DIM), lambda kv, s: (s, QB_COL + kv * g + gi))

    def band_specs(col0):
        return [
            pl.BlockSpec((wb, HEAD_DIM), lambda kv, s: (jnp.maximum(s * WIN_Q_BLOCKS - 1, 0), col0 + kv)),
            pl.BlockSpec((main, HEAD_DIM), lambda kv, s: (s, col0 + kv)),
            pl.BlockSpec((wb, HEAD_DIM), lambda kv, s: (jnp.minimum((s + 1) * WIN_Q_BLOCKS, nb - 1), col0 + kv)),
        ]

    def ctx_spec(col0):
        return pl.BlockSpec((n_ctx, HEAD_DIM), lambda kv, s: (ctx_blk, col0 + kv))

    kern = functools.partial(_win_kernel, n_lat=n_lat)
    return pl.pallas_call(
        kern,
        grid=(B_KV_HEADS, nsb),
        in_specs=[pl.BlockSpec(memory_space=pltpu.SMEM), qspec(0), qspec(1), qspec(2)]
        + band_specs(KB_COL) + band_specs(VB_COL) + [ctx_spec(KB_COL), ctx_spec(VB_COL)],
        out_specs=pl.BlockSpec((main, g * HEAD_DIM), lambda kv, s: (s, kv)),
        out_shape=jax.ShapeDtypeStruct((n_lat, B_HEADS * HEAD_DIM), BF16),
        scratch_shapes=[pltpu.VMEM((main + 2 * wb, HEAD_DIM), BF16)] * 2,
        compiler_params=_params(("parallel", "parallel")),
        name="window_attention",
    )(sink, p, p, p, p, p, p, p, p, p, p, p)


NA_Q_ROWS = 8
NA_K_ROWS = 16


def _na_kernel(q, kp, km, kn, vp, vm, vn, kc, vc, mb, o, kcat, vcat):
    half = (NA_K_ROWS - NA_Q_ROWS) // 2 * GRID_W
    main = NA_Q_ROWS * GRID_W
    kcat[0:half] = kp[...]
    kcat[half:half + main] = km[...]
    kcat[half + main:2 * half + main] = kn[...]
    vcat[0:half] = vp[...]
    vcat[half:half + main] = vm[...]
    vcat[half + main:2 * half + main] = vn[...]
    scale = HEAD_DIM ** -0.5
    qv = q[...]
    s_loc = _dot_nt(qv, kcat[...]) * scale + mb[...]
    s_ctx = _dot_nt(qv, kc[...]) * scale
    o[...] = _softmax_pv(s_loc, s_ctx, None, vcat[...], vc[...]).astype(BF16)


def _na_bias(rpb, n_rows):
    h = rpb.shape[0]
    w = GRID_W
    cq = np.arange(w)
    c0 = np.clip(cq - NA_COLS // 2, 0, w - NA_COLS)
    cmask = (cq[None, :] >= c0[:, None]) & (cq[None, :] < c0[:, None] + NA_COLS)
    coff = np.clip(cq[None, :] - cq[:, None] + (NA_COLS - 1), 0, 2 * NA_COLS - 2)
    by_col = jnp.where(cmask[None, None], rpb.astype(F32)[:, :, coff], NEG_INF)
    lead = (NA_K_ROWS - NA_Q_ROWS) // 2
    n_groups = n_rows // NA_Q_ROWS
    blocks = []
    for a in (0, 1, n_groups - 1):
        rq = a * NA_Q_ROWS + np.arange(NA_Q_ROWS)
        rk = a * NA_Q_ROWS - lead + np.arange(NA_K_ROWS)
        r0 = np.clip(rq - NA_ROWS // 2, 0, n_rows - NA_ROWS)
        rvalid = (rk[None, :] >= r0[:, None]) & (rk[None, :] < r0[:, None] + NA_ROWS)
        roff = np.clip(rk[None, :] - rq[:, None] + (NA_ROWS - 1), 0, 2 * NA_ROWS - 2)
        bias = jnp.where(rvalid[None, :, :, None, None], by_col[:, roff], NEG_INF)
        bias = bias.transpose(0, 1, 3, 2, 4)
        blocks.append(bias.reshape(h, NA_Q_ROWS * w, NA_K_ROWS * w))
    return jnp.stack(blocks, axis=1)


def _na_call(p, mb, *, n_lat, n_ctx):
    main = NA_Q_ROWS * GRID_W
    half = (NA_K_ROWS - NA_Q_ROWS) // 2 * GRID_W
    assert n_lat % main == 0 and n_lat % n_ctx == 0 and main == 2 * half
    ng = n_lat // main
    nhalf = n_lat // half
    ctx_blk = n_lat // n_ctx

    def band_specs(col0):
        return [
            pl.BlockSpec((half, HEAD_DIM), lambda h, a: (jnp.maximum(2 * a - 1, 0), col0 + h)),
            pl.BlockSpec((main, HEAD_DIM), lambda h, a: (a, col0 + h)),
            pl.BlockSpec((half, HEAD_DIM), lambda h, a: (jnp.minimum(2 * a + 2, nhalf - 1), col0 + h)),
        ]

    def mb_index(h, a):
        kind = jnp.where(a == 0, 0, jnp.where(a == ng - 1, 2, 1))
        return (h, kind, 0, 0)

    return pl.pallas_call(
        _na_kernel,
        grid=(C_HEADS, ng),
        in_specs=[pl.BlockSpec((main, HEAD_DIM), lambda h, a: (a, QC_COL + h))]
        + band_specs(KC_COL) + band_specs(VC_COL)
        + [pl.BlockSpec((n_ctx, HEAD_DIM), lambda h, a: (ctx_blk, KC_COL + h)),
           pl.BlockSpec((n_ctx, HEAD_DIM), lambda h, a: (ctx_blk, VC_COL + h)),
           pl.BlockSpec((None, None, main, NA_K_ROWS * GRID_W), mb_index)],
        out_specs=pl.BlockSpec((main, HEAD_DIM), lambda h, a: (a, h)),
        out_shape=jax.ShapeDtypeStruct((n_lat, C_HEADS * HEAD_DIM), BF16),
        scratch_shapes=[pltpu.VMEM((main + 2 * half, HEAD_DIM), BF16)] * 2,
        compiler_params=_params(("parallel", "parallel")),
        name="neighbourhood_attention",
    )(p, p, p, p, p, p, p, p, p, mb)


def _ctx_attn_kernel(sink_ref, q, k, v, o):
    hh = pl.program_id(0)
    scale = HEAD_DIM ** -0.5
    s = _dot_nt(q[...], k[...]) * scale
    sink = jnp.full((s.shape[0], 1), sink_ref[hh], F32)
    m = jnp.maximum(jnp.max(s, axis=1, keepdims=True), sink)
    pr = jnp.exp(s - m)
    den = jnp.sum(pr, axis=1, keepdims=True) + jnp.exp(sink - m)
    o[...] = (_dot(pr.astype(BF16), v[...]) / den).astype(BF16)


def _ctx_attn_call(p, sink12, *, n_lat, n_ctx):
    blk = n_lat // n_ctx
    g = B_HEADS // B_KV_HEADS

    def kcol(hh):
        return jnp.where(hh < B_HEADS, KB_COL + hh // g, KC_COL + hh - B_HEADS)

    def vcol(hh):
        return jnp.where(hh < B_HEADS, VB_COL + hh // g, VC_COL + hh - B_HEADS)

    return pl.pallas_call(
        _ctx_attn_kernel,
        grid=(B_HEADS + C_HEADS,),
        in_specs=[
            pl.BlockSpec(memory_space=pltpu.SMEM),
            pl.BlockSpec((n_ctx, HEAD_DIM), lambda hh: (blk, QB_COL + hh)),
            pl.BlockSpec((n_ctx, HEAD_DIM), lambda hh: (blk, kcol(hh))),
            pl.BlockSpec((n_ctx, HEAD_DIM), lambda hh: (blk, vcol(hh))),
        ],
        out_specs=pl.BlockSpec((n_ctx, HEAD_DIM), lambda hh: (0, hh)),
        out_shape=jax.ShapeDtypeStruct((n_ctx, (B_HEADS + C_HEADS) * HEAD_DIM), BF16),
        compiler_params=_params(("parallel",)),
        name="context_attention",
    )(sink12, p, p, p)


def _outproj_kernel(ya_ref, yb_ref, yc_ref, w_ref, x_ref, g_ref, gate_ref, o_ref, *, tm, n_lat):
    i = pl.program_id(0)
    ka = ya_ref.shape[1]
    kb = yb_ref.shape[1]
    acc = _dot(ya_ref[...], w_ref[0:ka, :])
    acc += _dot(yb_ref[...], w_ref[ka:ka + kb, :])
    acc += _dot(yc_ref[...], w_ref[ka + kb:, :])
    r = _rms(acc) * g_ref[...]
    o_ref[...] = x_ref[...] + _pick_rows(gate_ref[...], i * tm, tm, n_lat) * r


def _outproj_call(ya, yb, yc, w_all_bf16, layer, xa, g, gate2, *, rows, tm, n_lat):
    d = xa.shape[1]
    assert rows % tm == 0 and ya.shape[1] + yb.shape[1] + yc.shape[1] == d
    kern = functools.partial(_outproj_kernel, tm=tm, n_lat=n_lat)
    return pl.pallas_call(
        kern,
        grid=(rows // tm,),
        in_specs=[
            pl.BlockSpec((tm, ya.shape[1]), lambda i: (i, 0)),
            pl.BlockSpec((tm, yb.shape[1]), lambda i: (i, 0)),
            pl.BlockSpec((tm, yc.shape[1]), lambda i: (i, 0)),
            pl.BlockSpec((None, d, d), lambda i: (layer, 0, 0)),
            pl.BlockSpec((tm, d), lambda i: (i, 0)),
            pl.BlockSpec((1, d), lambda i: (0, 0)),
            pl.BlockSpec((2, d), lambda i: (0, 0)),
        ],
        out_specs=pl.BlockSpec((tm, d), lambda i: (i, 0)),
        out_shape=jax.ShapeDtypeStruct((rows, d), F32),
        compiler_params=_params(("parallel",)),
        name="out_projection",
    )(ya, yb, yc, w_all_bf16, xa, g.reshape(1, d), gate2)


FFN_TF = 256


def _swiglu_accumulate(h_ref, wg_ref, wu_ref, wd_ref, o_ref):
    h = h_ref[...]
    gt = _dot(h, wg_ref[...].astype(BF16))
    up = _dot(h, wu_ref[...].astype(BF16))
    act = (jax.nn.silu(gt) * up).astype(BF16)
    o_ref[...] += _dot(act, wd_ref[...].astype(BF16))


def _ffn_weight_specs(d, tf, nf, expert_of):
    return [
        pl.BlockSpec((None, d, tf), lambda m, f, *pf: (expert_of(m, *pf), 0, f)),
        pl.BlockSpec((None, d, tf), lambda m, f, *pf: (expert_of(m, *pf), 0, nf + f)),
        pl.BlockSpec((None, tf, d), lambda m, f, *pf: (expert_of(m, *pf), f, 0)),
    ]


def _ffn_dense_kernel(x_ref, gpre_ref, sh_ref, sc_ref, wg_ref, wu_ref, wd_ref, gpost_ref, gate_ref,
                      o_ref, h_ref, *, tm, n_lat):
    m = pl.program_id(0)
    f = pl.program_id(1)

    @pl.when(f == 0)
    def _():
        h = _norm_mod(x_ref[...], gpre_ref[...], sh_ref[...], sc_ref[...], m * tm, tm, n_lat)
        h_ref[...] = h.astype(BF16)
        o_ref[...] = jnp.zeros_like(o_ref)

    _swiglu_accumulate(h_ref, wg_ref, wu_ref, wd_ref, o_ref)

    @pl.when(f == pl.num_programs(1) - 1)
    def _():
        r = _rms(o_ref[...]) * gpost_ref[...]
        o_ref[...] = x_ref[...] + _pick_rows(gate_ref[...], m * tm, tm, n_lat) * r


def _ffn_dense_call(xa, gpre, sh2, sc2, w_gu, w_down, layer_set, gpost, gate2, *, tm, n_lat):
    t, d = xa.shape
    ffn = w_gu.shape[2] // 2
    tf = FFN_TF
    nf = ffn // tf
    assert t % tm == 0 and ffn % tf == 0
    kern = functools.partial(_ffn_dense_kernel, tm=tm, n_lat=n_lat)
    vec = lambda rows: pl.BlockSpec((rows, d), lambda m, f: (0, 0))
    return pl.pallas_call(
        kern,
        grid=(t // tm, nf),
        in_specs=[pl.BlockSpec((tm, d), lambda m, f: (m, 0), pipeline_mode=pl.Buffered(1)),
                  vec(1), vec(2), vec(2)]
        + _ffn_weight_specs(d, tf, nf, lambda m: layer_set) + [vec(1), vec(2)],
        out_specs=pl.BlockSpec((tm, d), lambda m, f: (m, 0)),
        out_shape=jax.ShapeDtypeStruct((t, d), F32),
        scratch_shapes=[pltpu.VMEM((tm, d), BF16)],
        compiler_params=_params(("parallel", "arbitrary")),
        name="swiglu_ffn",
    )(xa, gpre.reshape(1, d), sh2, sc2, w_gu, w_gu, w_down, gpost.reshape(1, d), gate2)


def _ffn_moe_kernel(te_ref, nv_ref, src_ref, x_hbm, gpre_ref, sh_ref, sc_ref, wg_ref, wu_ref, wd_ref,
                    o_ref, xbuf, h_ref, sem, *, tm):
    m = pl.program_id(0)
    f = pl.program_id(1)
    valid = m < nv_ref[0]

    @pl.when(valid & (f == 0))
    def _():
        def row_copy(r):
            return pltpu.make_async_copy(x_hbm.at[pl.ds(src_ref[m * tm + r], 1)], xbuf.at[pl.ds(r, 1)], sem)

        def issue(r, carry):
            row_copy(r).start()
            return carry

        def drain(r, carry):
            row_copy(r).wait()
            return carry

        lax.fori_loop(0, tm, issue, 0)
        lax.fori_loop(0, tm, drain, 0)
        h = _rms(xbuf[...]) * gpre_ref[...]
        h_ref[...] = (h * (1.0 + sc_ref[0:1, :]) + sh_ref[0:1, :]).astype(BF16)
        o_ref[...] = jnp.zeros_like(o_ref)

    @pl.when(jnp.logical_not(valid) & (f == 0))
    def _():
        o_ref[...] = jnp.zeros_like(o_ref)

    @pl.when(valid)
    def _():
        _swiglu_accumulate(h_ref, wg_ref, wu_ref, wd_ref, o_ref)


def _ffn_moe_call(x, src, tile_expert, n_valid, gpre, sh2, sc2, w_gu, w_down, *, tm):
    _, d = x.shape
    rows = src.shape[0]
    ffn = w_gu.shape[2] // 2
    tf = FFN_TF
    nf = ffn // tf
    assert rows % tm == 0 and ffn % tf == 0
    kern = functools.partial(_ffn_moe_kernel, tm=tm)
    vec = lambda r: pl.BlockSpec((r, d), lambda m, f, te, nv, sr: (0, 0))
    grid_spec = pltpu.PrefetchScalarGridSpec(
        num_scalar_prefetch=3,
        grid=(rows // tm, nf),
        in_specs=[pl.BlockSpec(memory_space=pl.ANY), vec(1), vec(2), vec(2)]
        + _ffn_weight_specs(d, tf, nf, lambda m, te, nv, sr: te[m]),
        out_specs=pl.BlockSpec((tm, d), lambda m, f, te, nv, sr: (m, 0)),
        scratch_shapes=[pltpu.VMEM((tm, d), F32), pltpu.VMEM((tm, d), BF16), pltpu.SemaphoreType.DMA(())],
    )
    return pl.pallas_call(
        kern,
        grid_spec=grid_spec,
        out_shape=jax.ShapeDtypeStruct((rows, d), F32),
        compiler_params=_params(("arbitrary", "arbitrary")),
        name="expert_ffn",
    )(tile_expert, n_valid, src, x, gpre.reshape(1, d), sh2, sc2, w_gu, w_gu, w_down)


ROUTER_TM = 512
R_E1, R_E2, R_W1, R_W2, R_RANK1, R_RANK2 = range(6)


def _router_kernel(x_ref, gpre_ref, sh_ref, sc_ref, wr_ref, route_ref, count_ref, carry_ref, *, tm):
    i = pl.program_id(0)

    @pl.when(i == 0)
    def _():
        carry_ref[...] = jnp.zeros_like(carry_ref)

    h = _rms(x_ref[...]) * gpre_ref[...]
    h = h * (1.0 + sc_ref[0:1, :]) + sh_ref[0:1, :]
    logits = jnp.dot(h, wr_ref[...], preferred_element_type=F32, precision=lax.Precision.HIGHEST)
    lane_i = lax.broadcasted_iota(jnp.int32, logits.shape, 1)
    lane = lane_i.astype(F32)
    logits = jnp.where(lane_i < N_EXPERTS, logits, -jnp.inf)
    v1 = jnp.max(logits, axis=1, keepdims=True)
    e1 = jnp.min(jnp.where(logits == v1, lane, float(LANES)), axis=1, keepdims=True)
    rest = jnp.where(lane == e1, -jnp.inf, logits)
    v2 = jnp.max(rest, axis=1, keepdims=True)
    e2 = jnp.min(jnp.where(rest == v2, lane, float(LANES)), axis=1, keepdims=True)
    ex = jnp.exp(v2 - v1)
    w1 = 1.0 / (1.0 + ex)
    w2 = ex / (1.0 + ex)
    hit1 = lane == e1
    hit2 = lane == e2
    assign = jnp.where(hit1, 1.0, jnp.where(hit2, 1.0, 0.0))
    r = lax.broadcasted_iota(jnp.int32, (tm, tm), 0)
    c = lax.broadcasted_iota(jnp.int32, (tm, tm), 1)
    before = jnp.where(c < r, 1.0, 0.0).astype(BF16)
    prefix = _dot(before, assign.astype(BF16)) + carry_ref[0:1, :]
    rank1 = jnp.sum(jnp.where(hit1, prefix, 0.0), axis=1, keepdims=True)
    rank2 = jnp.sum(jnp.where(hit2, prefix, 0.0), axis=1, keepdims=True)
    total = carry_ref[0:1, :] + jnp.sum(assign, axis=0, keepdims=True)
    carry_ref[...] = jnp.broadcast_to(total, carry_ref.shape)
    count_ref[...] = jnp.broadcast_to(total, count_ref.shape)
    rec = jnp.zeros(logits.shape, F32)
    for k, val in ((R_E1, e1), (R_E2, e2), (R_W1, w1), (R_W2, w2), (R_RANK1, rank1), (R_RANK2, rank2)):
        rec = jnp.where(lane_i == k, val, rec)
    route_ref[...] = rec


def _router_call(x, gpre, sh2, sc2, w_router):
    t, d = x.shape
    tm = ROUTER_TM
    assert t % tm == 0
    wr = jnp.zeros((d, LANES), F32).at[:, :N_EXPERTS].set(w_router)
    kern = functools.partial(_router_kernel, tm=tm)
    return pl.pallas_call(
        kern,
        grid=(t // tm,),
        in_specs=[
            pl.BlockSpec((tm, d), lambda i: (i, 0)),
            pl.BlockSpec((1, d), lambda i: (0, 0)),
            pl.BlockSpec((2, d), lambda i: (0, 0)),
            pl.BlockSpec((2, d), lambda i: (0, 0)),
            pl.BlockSpec((d, LANES), lambda i: (0, 0)),
        ],
        out_specs=[pl.BlockSpec((tm, LANES), lambda i: (i, 0)),
                   pl.BlockSpec((8, LANES), lambda i: (0, 0))],
        out_shape=[jax.ShapeDtypeStruct((t, LANES), F32), jax.ShapeDtypeStruct((8, LANES), F32)],
        scratch_shapes=[pltpu.VMEM((8, LANES), F32)],
        compiler_params=_params(("arbitrary",)),
        name="router_top2",
    )(x, gpre.reshape(1, d), sh2, sc2, wr)


COMBINE_TM = 256


def _combine_kernel(pos_ref, x_ref, route_ref, gpost_ref, gate_ref, ys_hbm, o_ref, buf, sem, *, tm):
    i = pl.program_id(0)

    def copy(t, k):
        p = pos_ref[(i * tm + t) * 2 + k]
        return pltpu.make_async_copy(ys_hbm.at[pl.ds(p, 1)], buf.at[k, pl.ds(t, 1)], sem)

    def issue(t, carry):
        copy(t, 0).start()
        copy(t, 1).start()
        return carry

    lax.fori_loop(0, tm, issue, 0)

    def drain(t, carry):
        copy(t, 0).wait()
        copy(t, 1).wait()
        return carry

    lax.fori_loop(0, tm, drain, 0)
    rec = route_ref[...]
    y = rec[:, R_W1:R_W1 + 1] * buf[0] + rec[:, R_W2:R_W2 + 1] * buf[1]
    o_ref[...] = x_ref[...] + gate_ref[0:1, :] * (_rms(y) * gpost_ref[...])


def _combine_call(x, ys, pos_flat, route, gpost, gate2):
    t, d = x.shape
    tm = COMBINE_TM
    assert t % tm == 0
    kern = functools.partial(_combine_kernel, tm=tm)
    grid_spec = pltpu.PrefetchScalarGridSpec(
        num_scalar_prefetch=1,
        grid=(t // tm,),
        in_specs=[
            pl.BlockSpec((tm, d), lambda i, pos: (i, 0)),
            pl.BlockSpec((tm, LANES), lambda i, pos: (i, 0)),
            pl.BlockSpec((1, d), lambda i, pos: (0, 0)),
            pl.BlockSpec((2, d), lambda i, pos: (0, 0)),
            pl.BlockSpec(memory_space=pl.ANY),
        ],
        out_specs=pl.BlockSpec((tm, d), lambda i, pos: (i, 0)),
        scratch_shapes=[pltpu.VMEM((2, tm, d), F32), pltpu.SemaphoreType.DMA(())],
    )
    return pl.pallas_call(
        kern,
        grid_spec=grid_spec,
        out_shape=jax.ShapeDtypeStruct((t, d), F32),
        compiler_params=_params(("arbitrary",)),
        name="combine",
    )(pos_flat, x, route, gpost.reshape(1, d), gate2, ys)


DENSE_TM = 768
LAT_TM = 1024
MOE_TM = 1024


def _moe_plan(route, counts, tm):
    t = route.shape[0]
    e = N_EXPERTS
    cnt = counts[0, :e].astype(jnp.int32)
    padded = (cnt + tm - 1) // tm * tm
    ends = jnp.cumsum(padded)
    starts = ends - padded
    ex = route[:, R_E1:R_E2 + 1].astype(jnp.int32)
    rank = route[:, R_RANK1:R_RANK2 + 1].astype(jnp.int32)
    pos = starts[ex] + rank
    max_tiles = (2 * t + e * (tm - 1)) // tm
    n_valid = ends[-1] // tm
    tile = jnp.minimum(jnp.arange(max_tiles, dtype=jnp.int32), n_valid - 1)
    tile_expert = jnp.minimum(jnp.searchsorted(ends, tile * tm, side="right"), e - 1).astype(jnp.int32)
    tok = jnp.repeat(jnp.arange(t, dtype=jnp.int32), 2)
    src = jnp.zeros((max_tiles * tm,), jnp.int32).at[pos.reshape(-1)].set(tok)
    return pos.reshape(-1), src, tile_expert, n_valid.reshape(1).astype(jnp.int32)


def kernel(x, c, ctx, c_ctx, w_mod, b_mod, g_mix_pre, g_mix_post, g_ffn_pre, g_ffn_post, w_in, sgu_ln_g,
           sgu_w, sgu_b, attn_sink, na_rpb, w_out, ffn_w_gu, ffn_w_down, moe_router, moe_w_gu, moe_w_down):
    assert x.shape[0] == 1 and ctx.shape[0] == 1
    n_lat, d = x.shape[1], x.shape[2]
    n_ctx = ctx.shape[1]
    depth = w_mod.shape[0]
    t_all = n_lat + n_ctx

    c8 = jnp.zeros((8, d), F32).at[0].set(c[0]).at[1].set(c_ctx)
    mod = _mod_call(c8, w_mod, b_mod)
    cos, sin = _rope_tables(n_lat, n_ctx)
    w_out_bf16 = w_out.astype(BF16)
    nb = B_HEADS * HEAD_DIM

    xa = jnp.concatenate([x[0], ctx[0]], axis=0)
    for l in range(depth):
        last = l == depth - 1
        m2 = mod[l, 0:2]
        sh1, sc1, g1, sh2, sc2, g2 = (m2[:, k * d:(k + 1) * d] for k in range(6))

        p = _inproj_call(xa, g_mix_pre[l], sh1, sc1, w_in, l, cos, sin, tm=DENSE_TM, n_lat=n_lat)
        y_sgu = _sgu_call(p, sgu_ln_g[l], sgu_w[l], sgu_b[l], chunks=DENSE_TM // CHUNK)
        y_win = _win_call(p, attn_sink[l], n_lat=n_lat, n_ctx=n_ctx)
        y_na = _na_call(p, _na_bias(na_rpb[l], n_lat // GRID_W), n_lat=n_lat, n_ctx=n_ctx)
        if not last:
            sink12 = jnp.concatenate([attn_sink[l], jnp.full((C_HEADS,), NEG_INF, F32)])
            y_ctx = _ctx_attn_call(p, sink12, n_lat=n_lat, n_ctx=n_ctx)
            y_win = jnp.concatenate([y_win, y_ctx[:, :nb]], axis=0)
            y_na = jnp.concatenate([y_na, y_ctx[:, nb:]], axis=0)
            xa = _outproj_call(y_sgu, y_win, y_na, w_out_bf16, l, xa, g_mix_post[l], g1,
                               rows=t_all, tm=DENSE_TM, n_lat=n_lat)
        else:
            xa = _outproj_call(y_sgu, y_win, y_na, w_out_bf16, l, xa, g_mix_post[l], g1,
                               rows=n_lat, tm=LAT_TM, n_lat=n_lat)

        if l % 2 == 0:
            xa = _ffn_dense_call(xa, g_ffn_pre[l], sh2, sc2, ffn_w_gu, ffn_w_down, l // 2,
                                 g_ffn_post[l], g2, tm=DENSE_TM, n_lat=n_lat)
        else:
            assert last, "expert layers are only supported as the last layer (latent rows only)"
            route, counts = _router_call(xa, g_ffn_pre[l], sh2, sc2, moe_router[l // 2])
            pos, src, tile_expert, n_valid = _moe_plan(route, counts, MOE_TM)
            ys = _ffn_moe_call(xa, src, tile_expert, n_valid, g_ffn_pre[l], sh2, sc2,
                               moe_w_gu[l // 2], moe_w_down[l // 2], tm=MOE_TM)
            xa = _combine_call(xa, ys, pos, route, g_ffn_post[l], g2)
    return xa[:n_lat][None]
```

```python
import functools

import numpy as np
import jax
import jax.numpy as jnp
from jax import lax
from jax.experimental import pallas as pl
from jax.experimental.pallas import tpu as pltpu

F32 = jnp.float32
BF16 = jnp.bfloat16

HEAD_DIM = 128
GRID_W = 64
CHUNK = 128
WINDOW_BLOCK = 128
B_HEADS = 6
B_KV_HEADS = 2
C_HEADS = 6
NA_ROWS = 8
NA_COLS = 16
N_EXPERTS = 8
ROPE_BASE = 10000.0
EPS = 1e-6
NEG_INF = -1e30

V7X_VMEM_BYTES = 64 * 1024 * 1024
VMEM_LIMIT = V7X_VMEM_BYTES - 6 * 1024 * 1024
LANES = 128

A_WIDTH = 512
QB_COL = 2 * A_WIDTH // HEAD_DIM
QC_COL = QB_COL + B_HEADS
KB_COL = QC_COL + C_HEADS
VB_COL = KB_COL + B_KV_HEADS
KC_COL = VB_COL + B_KV_HEADS
VC_COL = KC_COL + C_HEADS
IN_COLS = VC_COL + C_HEADS


def _params(sem, vmem=VMEM_LIMIT):
    return pltpu.CompilerParams(dimension_semantics=sem, vmem_limit_bytes=vmem)


def _rms(x):
    return x * lax.rsqrt(jnp.mean(x * x, axis=-1, keepdims=True) + EPS)


def _pick_rows(v2, row0, tm, n_lat):
    rows = row0 + lax.broadcasted_iota(jnp.int32, (tm, 1), 0)
    return jnp.where(rows >= n_lat, v2[1:2, :], v2[0:1, :])


def _norm_mod(x, g, sh2, sc2, row0, tm, n_lat):
    y = _rms(x) * g
    return y * (1.0 + _pick_rows(sc2, row0, tm, n_lat)) + _pick_rows(sh2, row0, tm, n_lat)


def _dot(a, b):
    return jnp.dot(a, b, preferred_element_type=F32)


def _dot_nt(a, b):
    return lax.dot_general(a, b, (((1,), (1,)), ((), ())), preferred_element_type=F32)


def _mod_kernel(c_ref, w_ref, b_ref, o_ref):
    a = jax.nn.silu(c_ref[...]).astype(BF16)
    o_ref[...] = _dot(a, w_ref[...].astype(BF16)) + b_ref[...]


def _mod_call(c8, w_mod, b_mod):
    depth, d, n = w_mod.shape
    tn = 1536
    return pl.pallas_call(
        _mod_kernel,
        grid=(depth, n // tn),
        in_specs=[
            pl.BlockSpec((8, d), lambda l, j: (0, 0)),
            pl.BlockSpec((None, d, tn), lambda l, j: (l, 0, j)),
            pl.BlockSpec((None, 1, tn), lambda l, j: (l, 0, j)),
        ],
        out_specs=pl.BlockSpec((None, 8, tn), lambda l, j: (l, 0, j)),
        out_shape=jax.ShapeDtypeStruct((depth, 8, n), F32),
        compiler_params=_params(("parallel", "parallel")),
        name="modulation",
    )(c8, w_mod, b_mod.reshape(depth, 1, n))


def _rope_head(x, cos, sin):
    lane = lax.broadcasted_iota(jnp.int32, x.shape, 1)
    first = (lane & 63) < 32
    partner = jnp.where(first, pltpu.roll(x, 96, 1), pltpu.roll(x, 32, 1))
    return x * cos + partner * sin


def _inproj_kernel(x_ref, g_ref, sh_ref, sc_ref, w_ref, cos_ref, sin_ref, o_ref, h_ref, *, tm, n_lat):
    i = pl.program_id(0)
    j = pl.program_id(1)

    @pl.when(j == 0)
    def _():
        h = _norm_mod(x_ref[...], g_ref[...], sh_ref[...], sc_ref[...], i * tm, tm, n_lat)
        h_ref[...] = h.astype(BF16)

    acc = _dot(h_ref[...], w_ref[...])

    def store(kinds):
        for hd, kind in enumerate(kinds):
            a = acc[:, hd * HEAD_DIM:(hd + 1) * HEAD_DIM]
            if kind == "gelu":
                a = jax.nn.gelu(a, approximate=True)
            elif kind == "rope":
                a = _rope_head(a, cos_ref[...], sin_ref[...])
            o_ref[:, hd * HEAD_DIM:(hd + 1) * HEAD_DIM] = a.astype(BF16)

    heads_per_tile = INPROJ_TN // HEAD_DIM
    for jt in range(IN_COLS // heads_per_tile):
        @pl.when(j == jt)
        def _(jt=jt):
            store(INPROJ_HEAD_KIND[jt * heads_per_tile:(jt + 1) * heads_per_tile])


INPROJ_HEAD_KIND = (("gelu",) * QB_COL + ("rope",) * B_HEADS + ("none",) * C_HEADS
                    + ("rope",) * B_KV_HEADS + ("none",) * (B_KV_HEADS + 2 * C_HEADS))
INPROJ_TN = 12 * HEAD_DIM


def _inproj_call(xa, g, sh2, sc2, w_all, layer, cos, sin, *, tm, n_lat):
    t, d = xa.shape
    n = w_all.shape[2]
    tn = INPROJ_TN
    assert t % tm == 0 and n == IN_COLS * HEAD_DIM and n % tn == 0 and w_all.dtype == BF16
    kern = functools.partial(_inproj_kernel, tm=tm, n_lat=n_lat)
    return pl.pallas_call(
        kern,
        grid=(t // tm, n // tn),
        in_specs=[
            pl.BlockSpec((tm, d), lambda i, j: (i, 0)),
            pl.BlockSpec((1, d), lambda i, j: (0, 0)),
            pl.BlockSpec((2, d), lambda i, j: (0, 0)),
            pl.BlockSpec((2, d), lambda i, j: (0, 0)),
            pl.BlockSpec((None, d, tn), lambda i, j: (layer, 0, j)),
            pl.BlockSpec((tm, HEAD_DIM), lambda i, j: (i, 0)),
            pl.BlockSpec((tm, HEAD_DIM), lambda i, j: (i, 0)),
        ],
        out_specs=pl.BlockSpec((tm, tn), lambda i, j: (i, j)),
        out_shape=jax.ShapeDtypeStruct((t, n), BF16),
        scratch_shapes=[pltpu.VMEM((tm, d), BF16)],
        compiler_params=_params(("parallel", "arbitrary")),
        name="in_projection",
    )(xa, g.reshape(1, d), sh2, sc2, w_all, cos, sin)


def _rope_tables(n_lat, n_ctx):
    t = jnp.arange(n_lat)
    n = HEAD_DIM // 4
    inv = ROPE_BASE ** (-(2.0 / (HEAD_DIM // 2)) * jnp.arange(n, dtype=F32))
    ang_r = (t // GRID_W).astype(F32)[:, None] * inv[None, :]
    ang_c = (t % GRID_W).astype(F32)[:, None] * inv[None, :]
    cos = jnp.concatenate([jnp.cos(ang_r)] * 2 + [jnp.cos(ang_c)] * 2, axis=1)
    sin = jnp.concatenate([-jnp.sin(ang_r), jnp.sin(ang_r), -jnp.sin(ang_c), jnp.sin(ang_c)], axis=1)
    cos = jnp.concatenate([cos, jnp.ones((n_ctx, HEAD_DIM), F32)], axis=0)
    sin = jnp.concatenate([sin, jnp.zeros((n_ctx, HEAD_DIM), F32)], axis=0)
    return cos, sin


def _sgu_kernel(u_ref, v_ref, lng_ref, w_ref, b_ref, o_ref, *, chunks):
    for c in range(chunks):
        rows = slice(c * CHUNK, (c + 1) * CHUNK)
        for g in range(A_WIDTH // HEAD_DIM):
            cols = slice(g * HEAD_DIM, (g + 1) * HEAD_DIM)
            v = v_ref[rows, cols].astype(F32)
            mu = jnp.mean(v, axis=-1, keepdims=True)
            var = jnp.mean(jnp.square(v - mu), axis=-1, keepdims=True)
            vn = (v - mu) * lax.rsqrt(var + EPS) * lng_ref[:, cols]
            s = _dot(w_ref[g].astype(BF16), vn.astype(BF16)) + b_ref[g]
            o_ref[rows, cols] = (u_ref[rows, cols].astype(F32) * s).astype(BF16)


def _sgu_call(p, ln_g, w_s, b_s, *, chunks):
    t = p.shape[0]
    rows = chunks * CHUNK
    assert t % rows == 0
    groups = A_WIDTH // HEAD_DIM
    b_full = jnp.broadcast_to(b_s[:, :, None], (groups, CHUNK, HEAD_DIM))
    kern = functools.partial(_sgu_kernel, chunks=chunks)
    return pl.pallas_call(
        kern,
        grid=(t // rows,),
        in_specs=[
            pl.BlockSpec((rows, A_WIDTH), lambda i: (i, 0)),
            pl.BlockSpec((rows, A_WIDTH), lambda i: (i, 1)),
            pl.BlockSpec((1, A_WIDTH), lambda i: (0, 0)),
            pl.BlockSpec((groups, CHUNK, CHUNK), lambda i: (0, 0, 0)),
            pl.BlockSpec((groups, CHUNK, HEAD_DIM), lambda i: (0, 0, 0)),
        ],
        out_specs=pl.BlockSpec((rows, A_WIDTH), lambda i: (i, 0)),
        out_shape=jax.ShapeDtypeStruct((t, A_WIDTH), BF16),
        compiler_params=_params(("parallel",)),
        name="spatial_gating",
    )(p, p, ln_g.reshape(1, A_WIDTH), w_s, b_full)


WIN_Q_BLOCKS = 4


def _softmax_pv(s_loc, s_ctx, sink_col, v_loc, v_ctx):
    m = jnp.maximum(jnp.max(s_loc, axis=1, keepdims=True), jnp.max(s_ctx, axis=1, keepdims=True))
    if sink_col is not None:
        m = jnp.maximum(m, sink_col)
    p_loc = jnp.exp(s_loc - m)
    p_ctx = jnp.exp(s_ctx - m)
    den = jnp.sum(p_loc, axis=1, keepdims=True) + jnp.sum(p_ctx, axis=1, keepdims=True)
    if sink_col is not None:
        den = den + jnp.exp(sink_col - m)
    o = _dot(p_loc.astype(BF16), v_loc) + _dot(p_ctx.astype(BF16), v_ctx)
    return o / den


def _win_kernel(sink_ref, q0, q1, q2, kp, km, kn, vp, vm, vn, kc, vc, o_ref, kcat, vcat, *, n_lat):
    kv = pl.program_id(0)
    s = pl.program_id(1)
    wb = WINDOW_BLOCK
    main = WIN_Q_BLOCKS * wb
    kcat[0:wb] = kp[...]
    kcat[wb:wb + main] = km[...]
    kcat[wb + main:2 * wb + main] = kn[...]
    vcat[0:wb] = vp[...]
    vcat[wb:wb + main] = vm[...]
    vcat[wb + main:2 * wb + main] = vn[...]
    scale = HEAD_DIM ** -0.5
    g = B_HEADS // B_KV_HEADS
    qs = (q0, q1, q2)
    row =lax.broadcasted_iota(jnp.int32, (g * wb, 3 * wb), 0) & (wb - 1)
    col = lax.broadcasted_iota(jnp.int32, (g * wb, 3 * wb), 1)
    rel = col - wb - row
    band = (rel >= -wb) & (rel <= wb)
    sink_col = jnp.concatenate(
        [jnp.full((wb, 1), sink_ref[kv * g + gi], F32) for gi in range(g)], axis=0)
    for b in range(WIN_Q_BLOCKS):
        n = s * WIN_Q_BLOCKS + b
        rows = slice(b * wb, (b + 1) * wb)
        q3 = jnp.concatenate([qs[gi][rows, :] for gi in range(g)], axis=0)
        keys = kcat[b * wb:(b + 3) * wb, :]
        vals = vcat[b * wb:(b + 3) * wb, :]
        kpos = (n - 1) * wb + col
        valid = band & (kpos >= 0) & (kpos < n_lat)
        s_loc = jnp.where(valid, _dot_nt(q3, keys) * scale, NEG_INF)
        s_ctx = _dot_nt(q3, kc[...]) * scale
        o = _softmax_pv(s_loc, s_ctx, sink_col, vals, vc[...])
        for gi in range(g):
            o_ref[rows, gi * HEAD_DIM:(gi + 1) * HEAD_DIM] = o[gi * wb:(gi + 1) * wb, :].astype(BF16)


def _win_call(p, sink, *, n_lat, n_ctx):
    wb = WINDOW_BLOCK
    main = WIN_Q_BLOCKS * wb
    assert n_lat % main == 0 and n_lat % n_ctx == 0
    nsb = n_lat // main
    nb = n_lat // wb
    g = B_HEADS // B_KV_HEADS
    ctx_blk = n_lat // n_ctx

    def qspec(gi):
        return pl.BlockSpec((main, HEAD_DIM), lambda kv, s: (s, QB_COL + kv * g + gi))

    def band_specs(col0):
        return [
            pl.BlockSpec((wb, HEAD_DIM), lambda kv, s: (jnp.maximum(s * WIN_Q_BLOCKS - 1, 0), col0 + kv)),
            pl.BlockSpec((main, HEAD_DIM), lambda kv, s: (s, col0 + kv)),
            pl.BlockSpec((wb, HEAD_DIM), lambda kv, s: (jnp.minimum((s + 1) * WIN_Q_BLOCKS, nb - 1), col0 + kv)),
        ]

    def ctx_spec(col0):
        return pl.BlockSpec((n_ctx, HEAD_DIM), lambda kv, s: (ctx_blk, col0 + kv))

    kern = functools.partial(_win_kernel, n_lat=n_lat)
    return pl.pallas_call(
        kern,
        grid=(B_KV_HEADS, nsb),
        in_specs=[pl.BlockSpec(memory_space=pltpu.SMEM), qspec(0), qspec(1), qspec(2)]
        + band_specs(KB_COL) + band_specs(VB_COL) + [ctx_spec(KB_COL), ctx_spec(VB_COL)],
        out_specs=pl.BlockSpec((main, g * HEAD_DIM), lambda kv, s: (s, kv)),
        out_shape=jax.ShapeDtypeStruct((n_lat, B_HEADS * HEAD_DIM), BF16),
        scratch_shapes=[pltpu.VMEM((main + 2 * wb, HEAD_DIM), BF16)] * 2,
        compiler_params=_params(("parallel", "parallel")),
        name="window_attention",
    )(sink, p, p, p, p, p, p, p, p, p, p, p)


NA_Q_ROWS = 8
NA_K_ROWS = 16


def _na_row_windows(group, n_rows):
    lead = (NA_K_ROWS - NA_Q_ROWS) // 2
    rq = group * NA_Q_ROWS + np.arange(NA_Q_ROWS)
    rk = group * NA_Q_ROWS - lead + np.arange(NA_K_ROWS)
    r0 = np.clip(rq - NA_ROWS // 2, 0, n_rows - NA_ROWS)
    valid = (rk[None, :] >= r0[:, None]) & (rk[None, :] < r0[:, None] + NA_ROWS)
    roff = rk[None, :] - rq[:, None] + (NA_ROWS - 1)
    return valid, roff


def _na_fill_bias(bt_ref, mb_ref, group, n_rows):
    valid, roff = _na_row_windows(group, n_rows)
    w = GRID_W
    left = lax.broadcasted_iota(jnp.int32, (w, 2 * w), 1) < w
    neg = jnp.full((w, 2 * w), NEG_INF, F32)
    for i in range(NA_Q_ROWS):
        for u in range(0, NA_K_ROWS, 2):
            lo = bt_ref[int(roff[i, u])] if valid[i, u] else neg
            hi = bt_ref[int(roff[i, u + 1])] if valid[i, u + 1] else neg
            blk = neg if not (valid[i, u] or valid[i, u + 1]) else jnp.where(left, lo, hi)
            mb_ref[i * w:(i + 1) * w, u * w:(u + 2) * w] = blk


def _na_kernel(q, kp, km, kn, vp, vm, vn, kc, vc, bt, o, kcat, vcat, mb, *, n_rows):
    a = pl.program_id(1)
    ng = n_rows // NA_Q_ROWS
    for group in sorted({0, min(1, ng - 1), ng - 1}):
        @pl.when(a == group)
        def _(group=group):
            _na_fill_bias(bt, mb, group, n_rows)

    half = (NA_K_ROWS - NA_Q_ROWS) // 2 * GRID_W
    main = NA_Q_ROWS * GRID_W
    kcat[0:half] = kp[...]
    kcat[half:half + main] = km[...]
    kcat[half + main:2 * half + main] = kn[...]
    vcat[0:half] = vp[...]
    vcat[half:half + main] = vm[...]
    vcat[half + main:2 * half + main] = vn[...]
    scale = HEAD_DIM ** -0.5
    qv = q[...]
    s_loc = _dot_nt(qv, kcat[...]) * scale + mb[...]
    s_ctx = _dot_nt(qv, kc[...]) * scale
    o[...] = _softmax_pv(s_loc, s_ctx, None, vcat[...], vc[...]).astype(BF16)


def _na_col_table(rpb):
    w = GRID_W
    cq = np.arange(w)
    c0 = np.clip(cq - NA_COLS // 2, 0, w - NA_COLS)
    cmask = (cq[None, :] >= c0[:, None]) & (cq[None, :] < c0[:, None] + NA_COLS)
    coff = np.clip(cq[None, :] - cq[:, None] + (NA_COLS - 1), 0, 2 * NA_COLS - 2)
    by_col = jnp.where(cmask[None, None], rpb.astype(F32)[:, :, coff], NEG_INF)
    return jnp.concatenate([by_col, by_col], axis=-1)


def _na_call(p, rpb, *, n_lat, n_ctx):
    main = NA_Q_ROWS * GRID_W
    half = (NA_K_ROWS - NA_Q_ROWS) // 2 * GRID_W
    assert n_lat % main == 0 and n_lat % n_ctx == 0 and main == 2 * half
    ng = n_lat // main
    assert ng >= 2
    nhalf = n_lat // half
    ctx_blk = n_lat // n_ctx
    bt = _na_col_table(rpb)
    kern = functools.partial(_na_kernel, n_rows=n_lat // GRID_W)

    def band_specs(col0):
        return [
            pl.BlockSpec((half, HEAD_DIM), lambda h, a: (jnp.maximum(2 * a - 1, 0), col0 + h)),
            pl.BlockSpec((main, HEAD_DIM), lambda h, a: (a, col0 + h)),
            pl.BlockSpec((half, HEAD_DIM), lambda h, a: (jnp.minimum(2 * a + 2, nhalf - 1), col0 + h)),
        ]

    return pl.pallas_call(
        kern,
        grid=(C_HEADS, ng),
        in_specs=[pl.BlockSpec((main, HEAD_DIM), lambda h, a: (a, QC_COL + h))]
        + band_specs(KC_COL) + band_specs(VC_COL)
        + [pl.BlockSpec((n_ctx, HEAD_DIM), lambda h, a: (ctx_blk, KC_COL + h)),
           pl.BlockSpec((n_ctx, HEAD_DIM), lambda h, a: (ctx_blk, VC_COL + h)),
           pl.BlockSpec((None,) + bt.shape[1:], lambda h, a: (h, 0, 0, 0))],
        out_specs=pl.BlockSpec((main, HEAD_DIM), lambda h, a: (a, h)),
        out_shape=jax.ShapeDtypeStruct((n_lat, C_HEADS * HEAD_DIM), BF16),
        scratch_shapes=[pltpu.VMEM((main + 2 * half, HEAD_DIM), BF16)] * 2
        + [pltpu.VMEM((main, NA_K_ROWS * GRID_W), F32)],
        compiler_params=_params(("arbitrary", "arbitrary")),
        name="neighbourhood_attention",
    )(p, p, p, p, p, p, p, p, p, bt)


def _ctx_attn_kernel(sink_ref, q, k, v, o):
    hh = pl.program_id(0)
    scale = HEAD_DIM ** -0.5
    s = _dot_nt(q[...], k[...]) * scale
    sink = jnp.full((s.shape[0], 1), sink_ref[hh], F32)
    m = jnp.maximum(jnp.max(s, axis=1, keepdims=True), sink)
    pr = jnp.exp(s - m)
    den = jnp.sum(pr, axis=1, keepdims=True) + jnp.exp(sink - m)
    o[...] = (_dot(pr.astype(BF16), v[...]) / den).astype(BF16)


def _ctx_attn_call(p, sink12, *, n_lat, n_ctx):
    blk = n_lat // n_ctx
    g = B_HEADS // B_KV_HEADS

    def kcol(hh):
        return jnp.where(hh < B_HEADS, KB_COL + hh // g, KC_COL + hh - B_HEADS)

    def vcol(hh):
        return jnp.where(hh < B_HEADS, VB_COL + hh // g, VC_COL + hh - B_HEADS)

    return pl.pallas_call(
        _ctx_attn_kernel,
        grid=(B_HEADS + C_HEADS,),
        in_specs=[
            pl.BlockSpec(memory_space=pltpu.SMEM),
            pl.BlockSpec((n_ctx, HEAD_DIM), lambda hh: (blk, QB_COL + hh)),
            pl.BlockSpec((n_ctx, HEAD_DIM), lambda hh: (blk, kcol(hh))),
            pl.BlockSpec((n_ctx, HEAD_DIM), lambda hh: (blk, vcol(hh))),
        ],
        out_specs=pl.BlockSpec((n_ctx, HEAD_DIM), lambda hh: (0, hh)),
        out_shape=jax.ShapeDtypeStruct((n_ctx, (B_HEADS + C_HEADS) * HEAD_DIM), BF16),
        compiler_params=_params(("parallel",)),
        name="context_attention",
    )(sink12, p, p, p)


def _outproj_kernel(ya_ref, yb_ref, yc_ref, w_ref, x_ref, g_ref, gate_ref, o_ref, *, tm, n_lat):
    i = pl.program_id(0)
    ka = ya_ref.shape[1]
    kb = yb_ref.shape[1]
    acc = _dot(ya_ref[...], w_ref[0:ka, :])
    acc += _dot(yb_ref[...], w_ref[ka:ka + kb, :])
    acc += _dot(yc_ref[...], w_ref[ka + kb:, :])
    r = _rms(acc) * g_ref[...]
    o_ref[...] = x_ref[...] + _pick_rows(gate_ref[...], i * tm, tm, n_lat) * r


def _outproj_call(ya, yb, yc, w_all_bf16, layer, xa, g, gate2, *, rows, tm, n_lat):
    d = xa.shape[1]
    assert rows % tm == 0 and ya.shape[1] + yb.shape[1] + yc.shape[1] == d
    kern = functools.partial(_outproj_kernel, tm=tm, n_lat=n_lat)
    return pl.pallas_call(
        kern,
        grid=(rows // tm,),
        in_specs=[
            pl.BlockSpec((tm, ya.shape[1]), lambda i: (i, 0)),
            pl.BlockSpec((tm, yb.shape[1]), lambda i: (i, 0)),
            pl.BlockSpec((tm, yc.shape[1]), lambda i: (i, 0)),
            pl.BlockSpec((None, d, d), lambda i: (layer, 0, 0)),
            pl.BlockSpec((tm, d), lambda i: (i, 0)),
            pl.BlockSpec((1, d), lambda i: (0, 0)),
            pl.BlockSpec((2, d), lambda i: (0, 0)),
        ],
        out_specs=pl.BlockSpec((tm, d), lambda i: (i, 0)),
        out_shape=jax.ShapeDtypeStruct((rows, d), F32),
        compiler_params=_params(("parallel",)),
        name="out_projection",
    )(ya, yb, yc, w_all_bf16, xa, g.reshape(1, d), gate2)


FFN_TF = 256


def _swiglu_accumulate(h_ref, wg_ref, wu_ref, wd_ref, o_ref):
    h = h_ref[...]
    gt = _dot(h, wg_ref[...].astype(BF16))
    up = _dot(h, wu_ref[...].astype(BF16))
    act = (jax.nn.silu(gt) * up).astype(BF16)
    o_ref[...] += _dot(act, wd_ref[...].astype(BF16))


def _ffn_weight_specs(d, tf, nf, expert_of):
    return [
        pl.BlockSpec((None, d, tf), lambda m, f, *pf: (expert_of(m, *pf), 0, f)),
        pl.BlockSpec((None, d, tf), lambda m, f, *pf: (expert_of(m, *pf), 0, nf + f)),
        pl.BlockSpec((None, tf, d), lambda m, f, *pf: (expert_of(m, *pf), f, 0)),
    ]


def _ffn_dense_kernel(x_ref, gpre_ref, sh_ref, sc_ref, wg_ref, wu_ref, wd_ref, gpost_ref, gate_ref,
                      o_ref, h_ref, *, tm, n_lat):
    m = pl.program_id(0)
    f = pl.program_id(1)

    @pl.when(f == 0)
    def _():
        h = _norm_mod(x_ref[...], gpre_ref[...], sh_ref[...], sc_ref[...], m * tm, tm, n_lat)
        h_ref[...] = h.astype(BF16)
        o_ref[...] = jnp.zeros_like(o_ref)

    _swiglu_accumulate(h_ref, wg_ref, wu_ref, wd_ref, o_ref)

    @pl.when(f == pl.num_programs(1) - 1)
    def _():
        r = _rms(o_ref[...]) * gpost_ref[...]
        o_ref[...] = x_ref[...] + _pick_rows(gate_ref[...], m * tm, tm, n_lat) * r


def _ffn_dense_call(xa, gpre, sh2, sc2, w_gu, w_down, layer_set, gpost, gate2, *, tm, n_lat):
    t, d = xa.shape
    ffn = w_gu.shape[2] // 2
    tf = FFN_TF
    nf = ffn // tf
    assert t % tm == 0 and ffn % tf == 0
    kern = functools.partial(_ffn_dense_kernel, tm=tm, n_lat=n_lat)
    vec = lambda rows: pl.BlockSpec((rows, d), lambda m, f: (0, 0))
    return pl.pallas_call(
        kern,
        grid=(t // tm, nf),
        in_specs=[pl.BlockSpec((tm, d), lambda m, f: (m, 0), pipeline_mode=pl.Buffered(1)),
                  vec(1), vec(2), vec(2)]
        + _ffn_weight_specs(d, tf, nf, lambda m: layer_set) + [vec(1), vec(2)],
        out_specs=pl.BlockSpec((tm, d), lambda m, f: (m, 0)),
        out_shape=jax.ShapeDtypeStruct((t, d), F32),
        scratch_shapes=[pltpu.VMEM((tm, d), BF16)],
        compiler_params=_params(("parallel", "arbitrary")),
        name="swiglu_ffn",
    )(xa, gpre.reshape(1, d), sh2, sc2, w_gu, w_gu, w_down, gpost.reshape(1, d), gate2)


MOE_SB = 512
MOE_NSB = 5


def _ffn_moe_kernel(te_ref, ns_ref, r0_ref, src_ref, x_hbm, gpre_ref, sh_ref, sc_ref, wg_ref, wu_ref, wd_ref, o_hbm,
                    stage, h_ref, acc, wgb, wub, wdb, gsem, osem, *, sb, nsb):
    m = pl.program_id(0)
    f = pl.program_id(1)
    n = ns_ref[m]
    n_tiles = pl.num_programs(0)

    def out_copy(mm, s):
        row = pl.multiple_of(r0_ref[mm] + s * sb, sb)
        return pltpu.make_async_copy(acc.at[pl.ds(s * sb, sb)], o_hbm.at[pl.ds(row, sb)], osem)

    def for_valid(count, body):
        for s in range(nsb):
            @pl.when(s < count)
            def _(s=s):
                body(s)

    @pl.when(f == 0)
    def _():
        def load(s):
            def row_copy(r):
                return pltpu.make_async_copy(
                    x_hbm.at[pl.ds(src_ref[r0_ref[m] + s * sb + r], 1)], stage.at[pl.ds(r, 1)], gsem)

            def issue(r, carry):
                row_copy(r).start()
                return carry

            def drain(r, carry):
                row_copy(r).wait()
                return carry

            lax.fori_loop(0, sb, issue, 0)
            lax.fori_loop(0, sb, drain, 0)
            h = _rms(stage[...]) * gpre_ref[...]
            h_ref[s * sb:(s + 1) * sb, :] = (h * (1.0 + sc_ref[0:1, :]) + sh_ref[0:1, :]).astype(BF16)

        for_valid(n, load)

        @pl.when(m > 0)
        def _():
            for_valid(ns_ref[jnp.maximum(m - 1, 0)], lambda s: out_copy(m - 1, s).wait())

        def clear(s):
            acc[s * sb:(s + 1) * sb, :] = jnp.zeros((sb, acc.shape[1]), F32)

        for_valid(n, clear)

    def accumulate(s):
        if s == 0:
            wg = wg_ref[...].astype(BF16)
            wu = wu_ref[...].astype(BF16)
            wd = wd_ref[...].astype(BF16)
            wgb[...] = wg
            wub[...] = wu
            wdb[...] = wd
        else:
            wg, wu, wd = wgb[...], wub[...], wdb[...]
        h = h_ref[s * sb:(s + 1) * sb, :]
        act = (jax.nn.silu(_dot(h, wg)) * _dot(h, wu)).astype(BF16)
        acc[s * sb:(s + 1) * sb, :] += _dot(act, wd)

    for_valid(n, accumulate)

    @pl.when(f == pl.num_programs(1) - 1)
    def _():
        for_valid(n, lambda s: out_copy(m, s).start())

        @pl.when(m == n_tiles - 1)
        def _():
            for_valid(n, lambda s: out_copy(m, s).wait())
            stage[...] = jnp.zeros(stage.shape, F32)

            def fill(g, carry):
                row = pl.multiple_of(r0_ref[n_tiles] + g * sb, sb)
                cp = pltpu.make_async_copy(stage, o_hbm.at[pl.ds(row, sb)], osem)
                cp.start()
                cp.wait()
                return carry

            lax.fori_loop(0, (o_hbm.shape[0] - r0_ref[n_tiles]) // sb, fill, 0)


def _ffn_moe_call(x, src, tile_expert, tile_nsb, tile_row0, gpre, sh2, sc2, w_gu, w_down):
    _, d = x.shape
    sb, nsb = MOE_SB, MOE_NSB
    rows = src.shape[0]
    n_tiles = tile_expert.shape[0]
    ffn = w_gu.shape[2] // 2
    tf = FFN_TF
    nf = ffn // tf
    assert rows % sb == 0 and tile_row0.shape[0] == n_tiles + 1 and ffn % tf == 0
    kern = functools.partial(_ffn_moe_kernel, sb=sb, nsb=nsb)
    vec = lambda r: pl.BlockSpec((r, d), lambda m, f, *pf: (0, 0))
    grid_spec = pltpu.PrefetchScalarGridSpec(
        num_scalar_prefetch=4,
        grid=(n_tiles, nf),
        in_specs=[pl.BlockSpec(memory_space=pl.ANY), vec(1), vec(2), vec(2)]
        + _ffn_weight_specs(d, tf, nf, lambda m, te, *pf: te[m]),
        out_specs=pl.BlockSpec(memory_space=pl.ANY),
        scratch_shapes=[pltpu.VMEM((sb, d), F32), pltpu.VMEM((sb * nsb, d), BF16), pltpu.VMEM((sb * nsb, d), F32),
                        pltpu.VMEM((d, tf), BF16), pltpu.VMEM((d, tf), BF16), pltpu.VMEM((tf, d), BF16),
                        pltpu.SemaphoreType.DMA(()), pltpu.SemaphoreType.DMA(())],
    )
    return pl.pallas_call(
        kern,
        grid_spec=grid_spec,
        out_shape=jax.ShapeDtypeStruct((rows, d), F32),
        compiler_params=_params(("arbitrary", "arbitrary")),
        name="expert_ffn",
    )(tile_expert, tile_nsb, tile_row0, src, x, gpre.reshape(1, d), sh2, sc2, w_gu, w_gu, w_down)


ROUTER_TM = 512
R_E1, R_E2, R_W1, R_W2, R_RANK1, R_RANK2 = range(6)


def _router_kernel(x_ref, gpre_ref, sh_ref, sc_ref, wr_ref, route_ref, count_ref, carry_ref, *, tm):
    i = pl.program_id(0)

    @pl.when(i == 0)
    def _():
        carry_ref[...] = jnp.zeros_like(carry_ref)

    h = _rms(x_ref[...]) * gpre_ref[...]
    h = h * (1.0 + sc_ref[0:1, :]) + sh_ref[0:1, :]
    logits = jnp.dot(h, wr_ref[...], preferred_element_type=F32, precision=lax.Precision.HIGHEST)
    lane_i = lax.broadcasted_iota(jnp.int32, logits.shape, 1)
    lane = lane_i.astype(F32)
    logits = jnp.where(lane_i < N_EXPERTS, logits, -jnp.inf)
    v1 = jnp.max(logits, axis=1, keepdims=True)
    e1 = jnp.min(jnp.where(logits == v1, lane, float(LANES)), axis=1, keepdims=True)
    rest = jnp.where(lane == e1, -jnp.inf, logits)
    v2 = jnp.max(rest, axis=1, keepdims=True)
    e2 = jnp.min(jnp.where(rest == v2, lane, float(LANES)), axis=1, keepdims=True)
    ex = jnp.exp(v2 - v1)
    w1 = 1.0 / (1.0 + ex)
    w2 = ex / (1.0 + ex)
    hit1 = lane == e1
    hit2 = lane == e2
    assign = jnp.where(hit1, 1.0, jnp.where(hit2, 1.0, 0.0))
    r = lax.broadcasted_iota(jnp.int32, (tm, tm), 0)
    c = lax.broadcasted_iota(jnp.int32, (tm, tm), 1)
    before = jnp.where(c < r, 1.0, 0.0).astype(BF16)
    prefix = _dot(before, assign.astype(BF16)) + carry_ref[0:1, :]
    rank1 = jnp.sum(jnp.where(hit1, prefix, 0.0), axis=1, keepdims=True)
    rank2 = jnp.sum(jnp.where(hit2, prefix, 0.0), axis=1, keepdims=True)
    total = carry_ref[0:1, :] + jnp.sum(assign, axis=0, keepdims=True)
    carry_ref[...] = jnp.broadcast_to(total, carry_ref.shape)
    count_ref[...] = jnp.broadcast_to(total, count_ref.shape)
    rec = jnp.zeros(logits.shape, F32)
    for k, val in ((R_E1, e1), (R_E2, e2), (R_W1, w1), (R_W2, w2), (R_RANK1, rank1), (R_RANK2, rank2)):
        rec = jnp.where(lane_i == k, val, rec)
    route_ref[...] = rec


def _router_call(x, gpre, sh2, sc2, w_router):
    t, d = x.shape
    tm = ROUTER_TM
    assert t % tm == 0
    wr = jnp.zeros((d, LANES), F32).at[:, :N_EXPERTS].set(w_router)
    kern = functools.partial(_router_kernel, tm=tm)
    return pl.pallas_call(
        kern,
        grid=(t // tm,),
        in_specs=[
            pl.BlockSpec((tm, d), lambda i: (i, 0)),
            pl.BlockSpec((1, d), lambda i: (0, 0)),
            pl.BlockSpec((2, d), lambda i: (0, 0)),
            pl.BlockSpec((2, d), lambda i: (0, 0)),
            pl.BlockSpec((d, LANES), lambda i: (0, 0)),
        ],
        out_specs=[pl.BlockSpec((tm, LANES), lambda i: (i, 0)),
                   pl.BlockSpec((8, LANES), lambda i: (0, 0))],
        out_shape=[jax.ShapeDtypeStruct((t, LANES), F32), jax.ShapeDtypeStruct((8, LANES), F32)],
        scratch_shapes=[pltpu.VMEM((8, LANES), F32)],
        compiler_params=_params(("arbitrary",)),
        name="router_top2",
    )(x, gpre.reshape(1, d), sh2, sc2, wr)


COMBINE_TM = 256


def _combine_kernel(pos_ref, x_ref, route_ref, gpost_ref, gate_ref, ys_hbm, o_ref, buf, sem, *, tm):
    i = pl.program_id(0)

    def copy(t, k):
        p = pos_ref[(i * tm + t) * 2 + k]
        return pltpu.make_async_copy(ys_hbm.at[pl.ds(p, 1)], buf.at[k, pl.ds(t, 1)], sem)

    def issue(t, carry):
        copy(t, 0).start()
        copy(t, 1).start()
        return carry

    lax.fori_loop(0, tm, issue, 0)

    def drain(t, carry):
        copy(t, 0).wait()
        copy(t, 1).wait()
        return carry

    lax.fori_loop(0, tm, drain, 0)
    rec = route_ref[...]
    y = rec[:, R_W1:R_W1 + 1] * buf[0] + rec[:, R_W2:R_W2 + 1] * buf[1]
    o_ref[...] = x_ref[...] + gate_ref[0:1, :] * (_rms(y) * gpost_ref[...])


def _combine_call(x, ys, pos_flat, route, gpost, gate2):
    t, d = x.shape
    tm = COMBINE_TM
    assert t % tm == 0
    kern = functools.partial(_combine_kernel, tm=tm)
    grid_spec = pltpu.PrefetchScalarGridSpec(
        num_scalar_prefetch=1,
        grid=(t // tm,),
        in_specs=[
            pl.BlockSpec((tm, d), lambda i, pos: (i, 0)),
            pl.BlockSpec((tm, LANES), lambda i, pos: (i, 0)),
            pl.BlockSpec((1, d), lambda i, pos: (0, 0)),
            pl.BlockSpec((2, d), lambda i, pos: (0, 0)),
            pl.BlockSpec(memory_space=pl.ANY),
        ],
        out_specs=pl.BlockSpec((tm, d), lambda i, pos: (i, 0)),
        scratch_shapes=[pltpu.VMEM((2, tm, d), F32), pltpu.SemaphoreType.DMA(())],
    )
    return pl.pallas_call(
        kern,
        grid_spec=grid_spec,
        out_shape=jax.ShapeDtypeStruct((t, d), F32),
        compiler_params=_params(("arbitrary",)),
        name="combine",
    )(pos_flat, x, route, gpost.reshape(1, d), gate2, ys)


DENSE_TM = 768
LAT_TM = 1024


def _moe_plan(route, counts):
    t = route.shape[0]
    e = N_EXPERTS
    sb, nsb = MOE_SB, MOE_NSB
    i32 = jnp.int32
    cnt = counts[0, :e].astype(i32)
    nsub = (cnt + sb - 1) // sb
    row_end = jnp.cumsum(nsub) * sb
    row_start = row_end - nsub * sb
    ntile = (nsub + nsb - 1) // nsb
    tile_end = jnp.cumsum(ntile)
    tile_start = tile_end - ntile
    ex = route[:, R_E1:R_E2 + 1].astype(i32)
    rank = route[:, R_RANK1:R_RANK2 + 1].astype(i32)
    pos = (row_start[ex] + rank).reshape(-1)
    max_sub = (2 * t + e * (sb - 1)) // sb
    max_tiles = (max_sub + e * (nsb - 1)) // nsb
    m = jnp.arange(max_tiles + 1, dtype=i32)
    used = m < tile_end[-1]
    m_used = jnp.minimum(m, tile_end[-1] - 1)
    tile_expert = jnp.minimum(jnp.sum((tile_end[None, :] <= m_used[:, None]).astype(i32), axis=1), e - 1)
    local = m_used - tile_start[tile_expert]
    tile_nsb = jnp.where(used, jnp.clip(nsub[tile_expert] - nsb * local, 0, nsb), 0)
    tile_row0 = jnp.where(used, row_start[tile_expert] + local * (nsb * sb), row_end[-1])
    tok = jnp.repeat(jnp.arange(t, dtype=i32), 2)
    src = jnp.zeros((max_sub * sb,), i32).at[pos].set(tok, unique_indices=True)
    return pos, src, tile_expert[:-1].astype(i32), tile_nsb[:-1].astype(i32), tile_row0.astype(i32)


def kernel(x, c, ctx, c_ctx, w_mod, b_mod, g_mix_pre, g_mix_post, g_ffn_pre, g_ffn_post, w_in, sgu_ln_g,
           sgu_w, sgu_b, attn_sink, na_rpb, w_out, ffn_w_gu, ffn_w_down, moe_router, moe_w_gu, moe_w_down):
    assert x.shape[0] == 1 and ctx.shape[0] == 1
    n_lat, d = x.shape[1], x.shape[2]
    n_ctx = ctx.shape[1]
    depth = w_mod.shape[0]
    t_all = n_lat + n_ctx

    c8 = jnp.zeros((8, d), F32).at[0].set(c[0]).at[1].set(c_ctx)
    mod = _mod_call(c8, w_mod, b_mod)
    cos, sin = _rope_tables(n_lat, n_ctx)
    w_in_bf16 = w_in.astype(BF16)
    w_out_bf16 = w_out.astype(BF16)
    nb = B_HEADS * HEAD_DIM

    xa = jnp.concatenate([x[0], ctx[0]], axis=0)
    for l in range(depth):
        last = l == depth - 1
        m2 = mod[l, 0:2]
        sh1, sc1, g1, sh2, sc2, g2 = (m2[:, k * d:(k + 1) * d] for k in range(6))

        p = _inproj_call(xa, g_mix_pre[l], sh1, sc1, w_in_bf16, l, cos, sin, tm=DENSE_TM, n_lat=n_lat)
        y_sgu = _sgu_call(p, sgu_ln_g[l], sgu_w[l], sgu_b[l], chunks=DENSE_TM // CHUNK)
        y_win = _win_call(p, attn_sink[l], n_lat=n_lat, n_ctx=n_ctx)
        y_na = _na_call(p, na_rpb[l], n_lat=n_lat, n_ctx=n_ctx)
        if not last:
            sink12 = jnp.concatenate([attn_sink[l], jnp.full((C_HEADS,), NEG_INF, F32)])
            y_ctx = _ctx_attn_call(p, sink12, n_lat=n_lat, n_ctx=n_ctx)
            y_win = jnp.concatenate([y_win, y_ctx[:, :nb]], axis=0)
            y_na = jnp.concatenate([y_na, y_ctx[:, nb:]], axis=0)
            xa = _outproj_call(y_sgu, y_win, y_na, w_out_bf16, l, xa, g_mix_post[l], g1,
                               rows=t_all, tm=DENSE_TM, n_lat=n_lat)
        else:
            xa = _outproj_call(y_sgu, y_win, y_na, w_out_bf16, l, xa, g_mix_post[l], g1,
                               rows=n_lat, tm=LAT_TM, n_lat=n_lat)

        if l % 2 == 0:
            xa = _ffn_dense_call(xa, g_ffn_pre[l], sh2, sc2, ffn_w_gu, ffn_w_down, l // 2,
                                 g_ffn_post[l], g2, tm=DENSE_TM, n_lat=n_lat)
        else:
            assert last, "expert layers are only supported as the last layer (latent rows only)"
            route, counts = _router_call(xa, g_ffn_pre[l], sh2, sc2, moe_router[l // 2])
            pos, src, tile_expert, tile_nsb, tile_row0 = _moe_plan(route, counts)
            ys = _ffn_moe_call(xa, src, tile_expert, tile_nsb, tile_row0, g_ffn_pre[l], sh2, sc2,
                               moe_w_gu[l // 2], moe_w_down[l // 2])
            xa = _combine_call(xa, ys, pos, route, g_ffn_post[l], g2)
    return xa[:n_lat][None]
```

```python
import functools

import numpy as np
import jax
import jax.numpy as jnp
from jax import lax
from jax.experimental import pallas as pl
from jax.experimental.pallas import tpu as pltpu

F32 = jnp.float32
BF16 = jnp.bfloat16

HEAD_DIM = 128
GRID_W = 64
CHUNK = 128
WINDOW_BLOCK = 128
B_HEADS = 6
B_KV_HEADS = 2
C_HEADS = 6
NA_ROWS = 8
NA_COLS = 16
N_EXPERTS = 8
ROPE_BASE = 10000.0
EPS = 1e-6
NEG_INF = -1e30

V7X_VMEM_BYTES = 64 * 1024 * 1024
VMEM_LIMIT = V7X_VMEM_BYTES - 6 * 1024 * 1024
LANES = 128

A_WIDTH = 512
QB_COL = 2 * A_WIDTH // HEAD_DIM
QC_COL = QB_COL + B_HEADS
KB_COL = QC_COL + C_HEADS
VB_COL = KB_COL + B_KV_HEADS
KC_COL = VB_COL + B_KV_HEADS
VC_COL = KC_COL + C_HEADS
IN_COLS = VC_COL + C_HEADS


def _params(sem, vmem=VMEM_LIMIT):
    return pltpu.CompilerParams(dimension_semantics=sem, vmem_limit_bytes=vmem)


def _rms(x):
    return x * lax.rsqrt(jnp.mean(x * x, axis=-1, keepdims=True) + EPS)


def _pick_rows(v2, row0, tm, n_lat):
    rows = row0 + lax.broadcasted_iota(jnp.int32, (tm, 1), 0)
    return jnp.where(rows >= n_lat, v2[1:2, :], v2[0:1, :])


def _norm_mod(x, g, sh2, sc2, row0, tm, n_lat):
    y = _rms(x) * g
    return y * (1.0 + _pick_rows(sc2, row0, tm, n_lat)) + _pick_rows(sh2, row0, tm, n_lat)


def _dot(a, b):
    return jnp.dot(a, b, preferred_element_type=F32)


def _dot_nt(a, b):
    return lax.dot_general(a, b, (((1,), (1,)), ((), ())), preferred_element_type=F32)


def _mod_kernel(c_ref, w_ref, b_ref, o_ref):
    a = jax.nn.silu(c_ref[...]).astype(BF16)
    o_ref[...] = _dot(a, w_ref[...].astype(BF16)) + b_ref[...]


def _mod_call(c8, w_mod, b_mod):
    depth, d, n = w_mod.shape
    tn = 1536
    return pl.pallas_call(
        _mod_kernel,
        grid=(depth, n // tn),
        in_specs=[
            pl.BlockSpec((8, d), lambda l, j: (0, 0)),
            pl.BlockSpec((None, d, tn), lambda l, j: (l, 0, j)),
            pl.BlockSpec((None, 1, tn), lambda l, j: (l, 0, j)),
        ],
        out_specs=pl.BlockSpec((None, 8, tn), lambda l, j: (l, 0, j)),
        out_shape=jax.ShapeDtypeStruct((depth, 8, n), F32),
        compiler_params=_params(("parallel", "parallel")),
        name="modulation",
    )(c8, w_mod, b_mod.reshape(depth, 1, n))


def _rope_head(x, cos, sin):
    lane = lax.broadcasted_iota(jnp.int32, x.shape, 1)
    first = (lane & 63) < 32
    partner = jnp.where(first, pltpu.roll(x, 96, 1), pltpu.roll(x, 32, 1))
    return x * cos + partner * sin


def _inproj_kernel(x_ref, g_ref, sh_ref, sc_ref, w_ref, cos_ref, sin_ref, o_ref, h_ref, *, tm, n_lat):
    i = pl.program_id(0)
    j = pl.program_id(1)

    @pl.when(j == 0)
    def _():
        h = _norm_mod(x_ref[...], g_ref[...], sh_ref[...], sc_ref[...], i * tm, tm, n_lat)
        h_ref[...] = h.astype(BF16)

    acc = _dot(h_ref[...], w_ref[...])

    def store(kinds):
        for hd, kind in enumerate(kinds):
            a = acc[:, hd * HEAD_DIM:(hd + 1) * HEAD_DIM]
            if kind == "gelu":
                a = jax.nn.gelu(a, approximate=True)
            elif kind == "rope":
                a = _rope_head(a, cos_ref[...], sin_ref[...])
            o_ref[:, hd * HEAD_DIM:(hd + 1) * HEAD_DIM] = a.astype(BF16)

    heads_per_tile = INPROJ_TN // HEAD_DIM
    for jt in range(IN_COLS // heads_per_tile):
        @pl.when(j == jt)
        def _(jt=jt):
            store(INPROJ_HEAD_KIND[jt * heads_per_tile:(jt + 1) * heads_per_tile])


INPROJ_HEAD_KIND = (("gelu",) * QB_COL + ("rope",) * B_HEADS + ("none",) * C_HEADS
                    + ("rope",) * B_KV_HEADS + ("none",) * (B_KV_HEADS + 2 * C_HEADS))
INPROJ_TN = 12 * HEAD_DIM


def _inproj_call(xa, g, sh2, sc2, w_all, layer, cos, sin, *, tm, n_lat):
    t, d = xa.shape
    n = w_all.shape[2]
    tn = INPROJ_TN
    assert t % tm == 0 and n == IN_COLS * HEAD_DIM and n % tn == 0 and w_all.dtype == BF16
    kern = functools.partial(_inproj_kernel, tm=tm, n_lat=n_lat)
    return pl.pallas_call(
        kern,
        grid=(t // tm, n // tn),
        in_specs=[
            pl.BlockSpec((tm, d), lambda i, j: (i, 0)),
            pl.BlockSpec((1, d), lambda i, j: (0, 0)),
            pl.BlockSpec((2, d), lambda i, j: (0, 0)),
            pl.BlockSpec((2, d), lambda i, j: (0, 0)),
            pl.BlockSpec((None, d, tn), lambda i, j: (layer, 0, j)),
            pl.BlockSpec((tm, HEAD_DIM), lambda i, j: (i, 0)),
            pl.BlockSpec((tm, HEAD_DIM), lambda i, j: (i, 0)),
        ],
        out_specs=pl.BlockSpec((tm, tn), lambda i, j: (i, j)),
        out_shape=jax.ShapeDtypeStruct((t, n), BF16),
        scratch_shapes=[pltpu.VMEM((tm, d), BF16)],
        compiler_params=_params(("parallel", "arbitrary")),
        name="in_projection",
    )(xa, g.reshape(1, d), sh2, sc2, w_all, cos, sin)


def _rope_tables(n_lat, n_ctx):
    t = np.arange(n_lat)
    n = HEAD_DIM // 4
    inv = (ROPE_BASE ** (-(2.0 / (HEAD_DIM // 2)) * np.arange(n, dtype=np.float32))).astype(np.float32)
    ang_r = (t // GRID_W).astype(np.float32)[:, None] * inv[None, :]
    ang_c = (t % GRID_W).astype(np.float32)[:, None] * inv[None, :]
    cos = np.concatenate([np.cos(ang_r)] * 2 + [np.cos(ang_c)] * 2, axis=1)
    sin = np.concatenate([-np.sin(ang_r), np.sin(ang_r), -np.sin(ang_c), np.sin(ang_c)], axis=1)
    cos = np.concatenate([cos, np.ones((n_ctx, HEAD_DIM), np.float32)], axis=0)
    sin = np.concatenate([sin, np.zeros((n_ctx, HEAD_DIM), np.float32)], axis=0)
    return jnp.asarray(cos, F32), jnp.asarray(sin, F32)


def _sgu_kernel(u_ref, v_ref, lng_ref, w_ref, b_ref, o_ref, *, chunks):
    for c in range(chunks):
        rows = slice(c * CHUNK, (c + 1) * CHUNK)
        for g in range(A_WIDTH // HEAD_DIM):
            cols = slice(g * HEAD_DIM, (g + 1) * HEAD_DIM)
            v = v_ref[rows, cols].astype(F32)
            mu = jnp.mean(v, axis=-1, keepdims=True)
            var = jnp.mean(jnp.square(v - mu), axis=-1, keepdims=True)
            vn = (v - mu) * lax.rsqrt(var + EPS) * lng_ref[:, cols]
            s = _dot(w_ref[g].astype(BF16), vn.astype(BF16)) + b_ref[g]
            o_ref[rows, cols] = (u_ref[rows, cols].astype(F32) * s).astype(BF16)


def _sgu_call(p, ln_g, w_s, b_s, *, chunks):
    t = p.shape[0]
    rows = chunks * CHUNK
    assert t % rows == 0
    groups = A_WIDTH // HEAD_DIM
    b_full = jnp.broadcast_to(b_s[:, :, None], (groups, CHUNK, HEAD_DIM))
    kern = functools.partial(_sgu_kernel, chunks=chunks)
    return pl.pallas_call(
        kern,
        grid=(t // rows,),
        in_specs=[
            pl.BlockSpec((rows, A_WIDTH), lambda i: (i, 0)),
            pl.BlockSpec((rows, A_WIDTH), lambda i: (i, 1)),
            pl.BlockSpec((1, A_WIDTH), lambda i: (0, 0)),
            pl.BlockSpec((groups, CHUNK, CHUNK), lambda i: (0, 0, 0)),
            pl.BlockSpec((groups, CHUNK, HEAD_DIM), lambda i: (0, 0, 0)),
        ],
        out_specs=pl.BlockSpec((rows, A_WIDTH), lambda i: (i, 0)),
        out_shape=jax.ShapeDtypeStruct((t, A_WIDTH), BF16),
        compiler_params=_params(("parallel",)),
        name="spatial_gating",
    )(p, p, ln_g.reshape(1, A_WIDTH), w_s, b_full)


WIN_Q_BLOCKS = 4


def _softmax_pv(s_loc, s_ctx, sink_col, v_loc, v_ctx):
    n_loc = s_loc.shape[1]
    s = jnp.concatenate([s_loc, s_ctx], axis=1)
    m = jnp.max(s, axis=1, keepdims=True)
    if sink_col is not None:
        m = jnp.maximum(m, sink_col)
    p = jnp.exp(s - m)
    den = jnp.sum(p, axis=1, keepdims=True)
    if sink_col is not None:
        den = den + jnp.exp(sink_col - m)
    p = p.astype(BF16)
    o = _dot(p[:, :n_loc], v_loc) + _dot(p[:, n_loc:], v_ctx)
    return o / den


def _win_kernel(sink_ref, q0, q1, q2, kp, km, kn, vp, vm, vn, kc, vc, o_ref, kcat, vcat, *, n_lat):
    kv = pl.program_id(0)
    s = pl.program_id(1)
    wb = WINDOW_BLOCK
    main = WIN_Q_BLOCKS * wb
    kcat[0:wb] = kp[...]
    kcat[wb:wb + main] = km[...]
    kcat[wb + main:2 * wb + main] = kn[...]
    vcat[0:wb] = vp[...]
    vcat[wb:wb + main] = vm[...]
    vcat[wb + main:2 * wb + main] = vn[...]
    scale = HEAD_DIM ** -0.5
    g = B_HEADS // B_KV_HEADS
    qs = (q0, q1, q2)
    row = lax.broadcasted_iota(jnp.int32, (g * wb, 3 * wb), 0) & (wb - 1)
    col = lax.broadcasted_iota(jnp.int32, (g * wb, 3 * wb), 1)
    rel = col - wb - row
    band_bias = jnp.where(rel < -wb, NEG_INF, jnp.where(rel > wb, NEG_INF, 0.0))
    col1 = lax.broadcasted_iota(jnp.int32, (1, 3 * wb), 1)
    sink_col = jnp.concatenate(
        [jnp.full((wb, 1), sink_ref[kv * g + gi], F32) for gi in range(g)], axis=0)
    n_blocks = n_lat // wb
    for b in range(WIN_Q_BLOCKS):
        n = s * WIN_Q_BLOCKS + b
        rows = slice(b * wb, (b + 1) * wb)
        q3 = jnp.concatenate([qs[gi][rows, :] for gi in range(g)], axis=0)
        keys = kcat[b * wb:(b + 3) * wb, :]
        vals = vcat[b * wb:(b + 3) * wb, :]
        off_start = jnp.where(n == 0, NEG_INF, 0.0)
        off_end = jnp.where(n == n_blocks - 1, NEG_INF, 0.0)
        edge = jnp.where(col1 < wb, off_start, jnp.where(col1 >= 2 * wb, off_end, 0.0))
        s_loc = _dot_nt(q3, keys) * scale + band_bias + edge
        s_ctx = _dot_nt(q3, kc[...]) * scale
        o = _softmax_pv(s_loc, s_ctx, sink_col, vals, vc[...])
        for gi in range(g):
            o_ref[rows, gi * HEAD_DIM:(gi + 1) * HEAD_DIM] = o[gi * wb:(gi + 1) * wb, :].astype(BF16)


def _win_call(p, sink, *, n_lat, n_ctx):
    wb = WINDOW_BLOCK
    main = WIN_Q_BLOCKS * wb
    assert n_lat % main == 0 and n_lat % n_ctx == 0
    nsb = n_lat // main
    nb = n_lat // wb
    g = B_HEADS // B_KV_HEADS
    ctx_blk = n_lat // n_ctx

    def qspec(gi):
        return pl.BlockSpec((main, HEAD_DIM), lambda kv, s: (s, QB_COL + kv * g + gi))

    def band_specs(col0):
        return [
            pl.BlockSpec((wb, HEAD_DIM), lambda kv, s: (jnp.maximum(s * WIN_Q_BLOCKS - 1, 0), col0 + kv)),
            pl.BlockSpec((main, HEAD_DIM), lambda kv, s: (s, col0 + kv)),
            pl.BlockSpec((wb, HEAD_DIM), lambda kv, s: (jnp.minimum((s + 1) * WIN_Q_BLOCKS, nb - 1), col0 + kv)),
        ]

    def ctx_spec(col0):
        return pl.BlockSpec((n_ctx, HEAD_DIM), lambda kv, s: (ctx_blk, col0 + kv))

    kern = functools.partial(_win_kernel, n_lat=n_lat)
    return pl.pallas_call(
        kern,
        grid=(B_KV_HEADS, nsb),
        in_specs=[pl.BlockSpec(memory_space=pltpu.SMEM), qspec(0), qspec(1), qspec(2)]
        + band_specs(KB_COL) + band_specs(VB_COL) + [ctx_spec(KB_COL), ctx_spec(VB_COL)],
        out_specs=pl.BlockSpec((main, g * HEAD_DIM), lambda kv, s: (s, kv)),
        out_shape=jax.ShapeDtypeStruct((n_lat, B_HEADS * HEAD_DIM), BF16),
        scratch_shapes=[pltpu.VMEM((main + 2 * wb, HEAD_DIM), BF16)] * 2,
        compiler_params=_params(("parallel", "parallel")),
        name="window_attention",
    )(sink, p, p, p, p, p, p, p, p, p, p, p)


NA_Q_ROWS = 8
NA_K_ROWS = 16


def _na_row_windows(group, n_rows):
    lead = (NA_K_ROWS - NA_Q_ROWS) // 2
    rq = group * NA_Q_ROWS + np.arange(NA_Q_ROWS)
    rk = group * NA_Q_ROWS - lead + np.arange(NA_K_ROWS)
    r0 = np.clip(rq - NA_ROWS // 2, 0, n_rows - NA_ROWS)
    valid = (rk[None, :] >= r0[:, None]) & (rk[None, :] < r0[:, None] + NA_ROWS)
    roff = rk[None, :] - rq[:, None] + (NA_ROWS - 1)
    return valid, roff


def _na_fill_bias(bt_ref, mb_ref, group, n_rows):
    valid, roff = _na_row_windows(group, n_rows)
    w = GRID_W
    left = lax.broadcasted_iota(jnp.int32, (w, 2 * w), 1) < w
    neg = jnp.full((w, 2 * w), NEG_INF, F32)
    for i in range(NA_Q_ROWS):
        for u in range(0, NA_K_ROWS, 2):
            lo = bt_ref[int(roff[i, u])] if valid[i, u] else neg
            hi = bt_ref[int(roff[i, u + 1])] if valid[i, u + 1] else neg
            blk = neg if not (valid[i, u] or valid[i, u + 1]) else jnp.where(left, lo, hi)
            mb_ref[i * w:(i + 1) * w, u * w:(u + 2) * w] = blk


def _na_kernel(q, kp, km, kn, vp, vm, vn, kc, vc, bt, o, kcat, vcat, mb, *, n_rows):
    a = pl.program_id(1)
    ng = n_rows // NA_Q_ROWS
    for group in sorted({0, min(1, ng - 1), ng - 1}):
        @pl.when(a == group)
        def _(group=group):
            _na_fill_bias(bt, mb, group, n_rows)

    half = (NA_K_ROWS - NA_Q_ROWS) // 2 * GRID_W
    main = NA_Q_ROWS * GRID_W
    kcat[0:half] = kp[...]
    kcat[half:half + main] = km[...]
    kcat[half + main:2 * half + main] = kn[...]
    vcat[0:half] = vp[...]
    vcat[half:half + main] = vm[...]
    vcat[half + main:2 * half + main] = vn[...]
    scale = HEAD_DIM ** -0.5
    qv = q[...]
    s_loc = _dot_nt(qv, kcat[...]) * scale + mb[...]
    s_ctx = _dot_nt(qv, kc[...]) * scale
    o[...] = _softmax_pv(s_loc, s_ctx, None, vcat[...], vc[...]).astype(BF16)


def _na_col_table(rpb):
    w = GRID_W
    cq = np.arange(w)
    c0 = np.clip(cq - NA_COLS // 2, 0, w - NA_COLS)
    cmask = (cq[None, :] >= c0[:, None]) & (cq[None, :] < c0[:, None] + NA_COLS)
    coff = np.clip(cq[None, :] - cq[:, None] + (NA_COLS - 1), 0, 2 * NA_COLS - 2)
    onehot = (coff[None] == np.arange(2 * NA_COLS - 1)[:, None, None]).astype(np.float32)
    picked = jnp.einsum("hrj,jqk->hrqk", rpb.astype(F32), onehot, precision=lax.Precision.HIGHEST)
    by_col = jnp.where(cmask[None, None], picked, NEG_INF)
    return jnp.concatenate([by_col, by_col], axis=-1)


def _na_call(p, rpb, *, n_lat, n_ctx):
    main = NA_Q_ROWS * GRID_W
    half = (NA_K_ROWS - NA_Q_ROWS) // 2 * GRID_W
    assert n_lat % main == 0 and n_lat % n_ctx == 0 and main == 2 * half
    ng = n_lat // main
    assert ng >= 2
    nhalf = n_lat // half
    ctx_blk = n_lat // n_ctx
    bt = _na_col_table(rpb)
    kern = functools.partial(_na_kernel, n_rows=n_lat // GRID_W)

    def band_specs(col0):
        return [
            pl.BlockSpec((half, HEAD_DIM), lambda h, a: (jnp.maximum(2 * a - 1, 0), col0 + h)),
            pl.BlockSpec((main, HEAD_DIM), lambda h, a: (a, col0 + h)),
            pl.BlockSpec((half, HEAD_DIM), lambda h, a: (jnp.minimum(2 * a + 2, nhalf - 1), col0 + h)),
        ]

    return pl.pallas_call(
        kern,
        grid=(C_HEADS, ng),
        in_specs=[pl.BlockSpec((main, HEAD_DIM), lambda h, a: (a, QC_COL + h))]
        + band_specs(KC_COL) + band_specs(VC_COL)
        + [pl.BlockSpec((n_ctx, HEAD_DIM), lambda h, a: (ctx_blk, KC_COL + h)),
           pl.BlockSpec((n_ctx, HEAD_DIM), lambda h, a: (ctx_blk, VC_COL + h)),
           pl.BlockSpec((None,) + bt.shape[1:], lambda h, a: (h, 0, 0, 0))],
        out_specs=pl.BlockSpec((main, HEAD_DIM), lambda h, a: (a, h)),
        out_shape=jax.ShapeDtypeStruct((n_lat, C_HEADS * HEAD_DIM), BF16),
        scratch_shapes=[pltpu.VMEM((main + 2 * half, HEAD_DIM), BF16)] * 2
        + [pltpu.VMEM((main, NA_K_ROWS * GRID_W), F32)],
        compiler_params=_params(("arbitrary", "arbitrary")),
        name="neighbourhood_attention",
    )(p, p, p, p, p, p, p, p, p, bt)


def _ctx_attn_kernel(sink_ref, q, k, v, o):
    hh = pl.program_id(0)
    scale = HEAD_DIM ** -0.5
    s = _dot_nt(q[...], k[...]) * scale
    sink = jnp.full((s.shape[0], 1), sink_ref[hh], F32)
    m = jnp.maximum(jnp.max(s, axis=1, keepdims=True), sink)
    pr = jnp.exp(s - m)
    den = jnp.sum(pr, axis=1, keepdims=True) + jnp.exp(sink - m)
    o[...] = (_dot(pr.astype(BF16), v[...]) / den).astype(BF16)


def _ctx_attn_call(p, sink12, *, n_lat, n_ctx):
    blk = n_lat // n_ctx
    g = B_HEADS // B_KV_HEADS

    def kcol(hh):
        return jnp.where(hh < B_HEADS, KB_COL + hh // g, KC_COL + hh - B_HEADS)

    def vcol(hh):
        return jnp.where(hh < B_HEADS, VB_COL + hh // g, VC_COL + hh - B_HEADS)

    return pl.pallas_call(
        _ctx_attn_kernel,
        grid=(B_HEADS + C_HEADS,),
        in_specs=[
            pl.BlockSpec(memory_space=pltpu.SMEM),
            pl.BlockSpec((n_ctx, HEAD_DIM), lambda hh: (blk, QB_COL + hh)),
            pl.BlockSpec((n_ctx, HEAD_DIM), lambda hh: (blk, kcol(hh))),
            pl.BlockSpec((n_ctx, HEAD_DIM), lambda hh: (blk, vcol(hh))),
        ],
        out_specs=pl.BlockSpec((n_ctx, HEAD_DIM), lambda hh: (0, hh)),
        out_shape=jax.ShapeDtypeStruct((n_ctx, (B_HEADS + C_HEADS) * HEAD_DIM), BF16),
        compiler_params=_params(("parallel",)),
        name="context_attention",
    )(sink12, p, p, p)


def _outproj_kernel(ya_ref, yb_ref, yc_ref, w_ref, x_ref, g_ref, gate_ref, o_ref, *, tm, n_lat):
    i = pl.program_id(0)
    ka = ya_ref.shape[1]
    kb = yb_ref.shape[1]
    acc = _dot(ya_ref[...], w_ref[0:ka, :])
    acc += _dot(yb_ref[...], w_ref[ka:ka + kb, :])
    acc += _dot(yc_ref[...], w_ref[ka + kb:, :])
    r = _rms(acc) * g_ref[...]
    o_ref[...] = x_ref[...] + _pick_rows(gate_ref[...], i * tm, tm, n_lat) * r


def _outproj_call(ya, yb, yc, w_all_bf16, layer, xa, g, gate2, *, rows, tm, n_lat):
    d = xa.shape[1]
    assert rows % tm == 0 and ya.shape[1] + yb.shape[1] + yc.shape[1] == d
    kern = functools.partial(_outproj_kernel, tm=tm, n_lat=n_lat)
    return pl.pallas_call(
        kern,
        grid=(rows // tm,),
        in_specs=[
            pl.BlockSpec((tm, ya.shape[1]), lambda i: (i, 0)),
            pl.BlockSpec((tm, yb.shape[1]), lambda i: (i, 0)),
            pl.BlockSpec((tm, yc.shape[1]), lambda i: (i, 0)),
            pl.BlockSpec((None, d, d), lambda i: (layer, 0, 0)),
            pl.BlockSpec((tm, d), lambda i: (i, 0)),
            pl.BlockSpec((1, d), lambda i: (0, 0)),
            pl.BlockSpec((2, d), lambda i: (0, 0)),
        ],
        out_specs=pl.BlockSpec((tm, d), lambda i: (i, 0)),
        out_shape=jax.ShapeDtypeStruct((rows, d), F32),
        compiler_params=_params(("parallel",)),
        name="out_projection",
    )(ya, yb, yc, w_all_bf16, xa, g.reshape(1, d), gate2)


FFN_TF = 256
DENSE_FFN_TF = 512


def _swiglu_accumulate(h_ref, wg_ref, wu_ref, wd_ref, o_ref):
    h = h_ref[...]
    gt = _dot(h, wg_ref[...].astype(BF16))
    up = _dot(h, wu_ref[...].astype(BF16))
    act = (jax.nn.silu(gt) * up).astype(BF16)
    o_ref[...] += _dot(act, wd_ref[...].astype(BF16))


def _ffn_weight_specs(d, tf, nf, expert_of, chunk_of=lambda m, f, *pf: f):
    return [
        pl.BlockSpec((None, d, tf), lambda m, f, *pf: (expert_of(m, *pf), 0, chunk_of(m, f, *pf))),
        pl.BlockSpec((None, d, tf), lambda m, f, *pf: (expert_of(m, *pf), 0, nf + chunk_of(m, f, *pf))),
        pl.BlockSpec((None, tf, d), lambda m, f, *pf: (expert_of(m, *pf), chunk_of(m, f, *pf), 0)),
    ]


def _ffn_dense_kernel(x_ref, gpre_ref, sh_ref, sc_ref, wg_ref, wu_ref, wd_ref, gpost_ref, gate_ref,
                      o_ref, h_ref, *, tm, n_lat):
    m = pl.program_id(0)
    f = pl.program_id(1)

    @pl.when(f == 0)
    def _():
        h = _norm_mod(x_ref[...], gpre_ref[...], sh_ref[...], sc_ref[...], m * tm, tm, n_lat)
        h_ref[...] = h.astype(BF16)
        o_ref[...] = jnp.zeros_like(o_ref)

    _swiglu_accumulate(h_ref, wg_ref, wu_ref, wd_ref, o_ref)

    @pl.when(f == pl.num_programs(1) - 1)
    def _():
        r = _rms(o_ref[...]) * gpost_ref[...]
        o_ref[...] = x_ref[...] + _pick_rows(gate_ref[...], m * tm, tm, n_lat) * r


def _ffn_dense_call(xa, gpre, sh2, sc2, w_gu, w_down, layer_set, gpost, gate2, *, tm, n_lat):
    t, d = xa.shape
    ffn = w_gu.shape[2] // 2
    tf = DENSE_FFN_TF
    nf = ffn // tf
    assert t % tm == 0 and ffn % tf == 0
    kern = functools.partial(_ffn_dense_kernel, tm=tm, n_lat=n_lat)
    vec = lambda rows: pl.BlockSpec((rows, d), lambda m, f: (0, 0))
    return pl.pallas_call(
        kern,
        grid=(t // tm, nf),
        in_specs=[pl.BlockSpec((tm, d), lambda m, f: (m, 0), pipeline_mode=pl.Buffered(1)),
                  vec(1), vec(2), vec(2)]
        + _ffn_weight_specs(d, tf, nf, lambda m: layer_set) + [vec(1), vec(2)],
        out_specs=pl.BlockSpec((tm, d), lambda m, f: (m, 0)),
        out_shape=jax.ShapeDtypeStruct((t, d), F32),
        scratch_shapes=[pltpu.VMEM((tm, d), BF16)],
        compiler_params=_params(("parallel", "arbitrary")),
        name="swiglu_ffn",
    )(xa, gpre.reshape(1, d), sh2, sc2, w_gu, w_gu, w_down, gpost.reshape(1, d), gate2)


MOE_SB = 512
MOE_NSB = 5


def _ffn_moe_kernel(te_ref, ns_ref, r0_ref, src_ref, x_hbm, gpre_ref, sh_ref, sc_ref, wg_ref, wu_ref, wd_ref, o_hbm,
                    stage, h_ref, acc, wgb, wub, wdb, gsem, osem, *, sb, nsb):
    m = pl.program_id(0)
    f = pl.program_id(1)
    n = ns_ref[m]
    n_tiles = pl.num_programs(0)

    def out_copy(mm, s):
        row = pl.multiple_of(r0_ref[mm] + s * sb, sb)
        return pltpu.make_async_copy(acc.at[pl.ds(s * sb, sb)], o_hbm.at[pl.ds(row, sb)], osem)

    def for_valid(count, body):
        for s in range(nsb):
            @pl.when(s < count)
            def _(s=s):
                body(s)

    @pl.when(f == 0)
    def _():
        def load(s):
            def row_copy(r):
                return pltpu.make_async_copy(
                    x_hbm.at[pl.ds(src_ref[r0_ref[m] + s * sb + r], 1)], stage.at[pl.ds(r, 1)], gsem)

            def issue(r, carry):
                row_copy(r).start()
                return carry

            def drain(r, carry):
                row_copy(r).wait()
                return carry

            lax.fori_loop(0, sb, issue, 0, unroll=8)
            lax.fori_loop(0, sb, drain, 0, unroll=8)
            h = _rms(stage[...]) * gpre_ref[...]
            h_ref[s * sb:(s + 1) * sb, :] = (h * (1.0 + sc_ref[0:1, :]) + sh_ref[0:1, :]).astype(BF16)

        for_valid(n, load)

        @pl.when(m > 0)
        def _():
            for_valid(ns_ref[jnp.maximum(m - 1, 0)], lambda s: out_copy(m - 1, s).wait())

        def clear(s):
            acc[s * sb:(s + 1) * sb, :] = jnp.zeros((sb, acc.shape[1]), F32)

        for_valid(n, clear)

    def accumulate(s):
        if s == 0:
            wg = wg_ref[...].astype(BF16)
            wu = wu_ref[...].astype(BF16)
            wd = wd_ref[...].astype(BF16)
            wgb[...] = wg
            wub[...] = wu
            wdb[...] = wd
        else:
            wg, wu, wd = wgb[...], wub[...], wdb[...]
        h = h_ref[s * sb:(s + 1) * sb, :]
        act = (jax.nn.silu(_dot(h, wg)) * _dot(h, wu)).astype(BF16)
        acc[s * sb:(s + 1) * sb, :] += _dot(act, wd)

    for_valid(n, accumulate)

    @pl.when(f == pl.num_programs(1) - 1)
    def _():
        for_valid(n, lambda s: out_copy(m, s).start())

        @pl.when(m == n_tiles - 1)
        def _():
            for_valid(n, lambda s: out_copy(m, s).wait())
            stage[...] = jnp.zeros(stage.shape, F32)

            def fill(g, carry):
                row = pl.multiple_of(r0_ref[n_tiles] + g * sb, sb)
                cp = pltpu.make_async_copy(stage, o_hbm.at[pl.ds(row, sb)], osem)
                cp.start()
                cp.wait()
                return carry

            lax.fori_loop(0, (o_hbm.shape[0] - r0_ref[n_tiles]) // sb, fill, 0)


def _ffn_moe_call(x, src, tile_expert, tile_nsb, tile_row0, gpre, sh2, sc2, w_gu, w_down):
    _, d = x.shape
    sb, nsb = MOE_SB, MOE_NSB
    rows = src.shape[0]
    n_tiles = tile_expert.shape[0]
    ffn = w_gu.shape[2] // 2
    tf = FFN_TF
    nf = ffn // tf
    assert rows % sb == 0 and tile_row0.shape[0] == n_tiles + 1 and ffn % tf == 0
    kern = functools.partial(_ffn_moe_kernel, sb=sb, nsb=nsb)
    vec = lambda r: pl.BlockSpec((r, d), lambda m, f, *pf: (0, 0))
    grid_spec = pltpu.PrefetchScalarGridSpec(
        num_scalar_prefetch=4,
        grid=(n_tiles, nf),
        in_specs=[pl.BlockSpec(memory_space=pl.ANY), vec(1), vec(2), vec(2)]
        + _ffn_weight_specs(d, tf, nf, lambda m, te, *pf: te[m],
                            lambda m, f, te, ns, *pf: jnp.where(ns[m] > 0, f, nf - 1)),
        out_specs=pl.BlockSpec(memory_space=pl.ANY),
        scratch_shapes=[pltpu.VMEM((sb, d), F32), pltpu.VMEM((sb * nsb, d), BF16), pltpu.VMEM((sb * nsb, d), F32),
                        pltpu.VMEM((d, tf), BF16), pltpu.VMEM((d, tf), BF16), pltpu.VMEM((tf, d), BF16),
                        pltpu.SemaphoreType.DMA(()), pltpu.SemaphoreType.DMA(())],
    )
    return pl.pallas_call(
        kern,
        grid_spec=grid_spec,
        out_shape=jax.ShapeDtypeStruct((rows, d), F32),
        compiler_params=_params(("arbitrary", "arbitrary")),
        name="expert_ffn",
    )(tile_expert, tile_nsb, tile_row0, src, x, gpre.reshape(1, d), sh2, sc2, w_gu, w_gu, w_down)


ROUTER_TM = 512
R_E1, R_E2, R_W1, R_W2, R_RANK1, R_RANK2 = range(6)


def _router_kernel(x_ref, gpre_ref, sh_ref, sc_ref, wr_ref, route_ref, count_ref, carry_ref, *, tm):
    i = pl.program_id(0)

    @pl.when(i == 0)
    def _():
        carry_ref[...] = jnp.zeros_like(carry_ref)

    h = _rms(x_ref[...]) * gpre_ref[...]
    h = h * (1.0 + sc_ref[0:1, :]) + sh_ref[0:1, :]
    w = wr_ref[...]
    h_hi = h.astype(BF16)
    h_lo = (h - h_hi.astype(F32)).astype(BF16)
    w_hi = w.astype(BF16)
    w_lo = (w - w_hi.astype(F32)).astype(BF16)
    logits = _dot(h_hi, w_hi) + (_dot(h_hi, w_lo) + _dot(h_lo, w_hi))
    lane_i = lax.broadcasted_iota(jnp.int32, logits.shape, 1)
    lane = lane_i.astype(F32)
    logits = jnp.where(lane_i < N_EXPERTS, logits, -jnp.inf)
    v1 = jnp.max(logits, axis=1, keepdims=True)
    e1 = jnp.min(jnp.where(logits == v1, lane, float(LANES)), axis=1, keepdims=True)
    rest = jnp.where(lane == e1, -jnp.inf, logits)
    v2 = jnp.max(rest, axis=1, keepdims=True)
    e2 = jnp.min(jnp.where(rest == v2, lane, float(LANES)), axis=1, keepdims=True)
    ex = jnp.exp(v2 - v1)
    w1 = 1.0 / (1.0 + ex)
    w2 = ex / (1.0 + ex)
    hit1 = lane == e1
    hit2 = lane == e2
    assign = jnp.where(hit1, 1.0, jnp.where(hit2, 1.0, 0.0))
    r = lax.broadcasted_iota(jnp.int32, (tm, tm), 0)
    c = lax.broadcasted_iota(jnp.int32, (tm, tm), 1)
    before = jnp.where(c < r, 1.0, 0.0).astype(BF16)
    prefix = _dot(before, assign.astype(BF16)) + carry_ref[0:1, :]
    rank1 = jnp.sum(jnp.where(hit1, prefix, 0.0), axis=1, keepdims=True)
    rank2 = jnp.sum(jnp.where(hit2, prefix, 0.0), axis=1, keepdims=True)
    total = carry_ref[0:1, :] + jnp.sum(assign, axis=0, keepdims=True)
    carry_ref[...] = jnp.broadcast_to(total, carry_ref.shape)
    count_ref[...] = jnp.broadcast_to(total, count_ref.shape)
    rec = jnp.zeros(logits.shape, F32)
    for k, val in ((R_E1, e1), (R_E2, e2), (R_W1, w1), (R_W2, w2), (R_RANK1, rank1), (R_RANK2, rank2)):
        rec = jnp.where(lane_i == k, val, rec)
    route_ref[...] = rec


def _router_call(x, gpre, sh2, sc2, w_router):
    t, d = x.shape
    tm = ROUTER_TM
    assert t % tm == 0
    wr = jnp.zeros((d, LANES), F32).at[:, :N_EXPERTS].set(w_router)
    kern = functools.partial(_router_kernel, tm=tm)
    return pl.pallas_call(
        kern,
        grid=(t // tm,),
        in_specs=[
            pl.BlockSpec((tm, d), lambda i: (i, 0)),
            pl.BlockSpec((1, d), lambda i: (0, 0)),
            pl.BlockSpec((2, d), lambda i: (0, 0)),
            pl.BlockSpec((2, d), lambda i: (0, 0)),
            pl.BlockSpec((d, LANES), lambda i: (0, 0)),
        ],
        out_specs=[pl.BlockSpec((tm, LANES), lambda i: (i, 0)),
                   pl.BlockSpec((8, LANES), lambda i: (0, 0))],
        out_shape=[jax.ShapeDtypeStruct((t, LANES), F32), jax.ShapeDtypeStruct((8, LANES), F32)],
        scratch_shapes=[pltpu.VMEM((8, LANES), F32)],
        compiler_params=_params(("arbitrary",)),
        name="router_top2",
    )(x, gpre.reshape(1, d), sh2, sc2, wr)


COMBINE_TM = 256


def _combine_kernel(pos_ref, x_ref, route_ref, gpost_ref, gate_ref, ys_hbm, o_ref, buf, sem, *, tm):
    i = pl.program_id(0)
    slot = i % 2

    def copy(step, t, k):
        p = pos_ref[(step * tm + t) * 2 + k]
        return pltpu.make_async_copy(ys_hbm.at[pl.ds(p, 1)], buf.at[step % 2, k, pl.ds(t, 1)], sem.at[step % 2])

    def fetch(step):
        def issue(t, carry):
            copy(step, t, 0).start()
            copy(step, t, 1).start()
            return carry

        lax.fori_loop(0, tm, issue, 0, unroll=8)

    @pl.when(i == 0)
    def _():
        fetch(i)

    @pl.when(i + 1 < pl.num_programs(0))
    def _():
        fetch(i + 1)

    def drain(t, carry):
        copy(i, t, 0).wait()
        copy(i, t, 1).wait()
        return carry

    lax.fori_loop(0, tm, drain, 0, unroll=8)
    rec = route_ref[...]
    y = rec[:, R_W1:R_W1 + 1] * buf[slot, 0] + rec[:, R_W2:R_W2 + 1] * buf[slot, 1]
    o_ref[...] = x_ref[...] + gate_ref[0:1, :] * (_rms(y) * gpost_ref[...])


def _combine_call(x, ys, pos_flat, route, gpost, gate2):
    t, d = x.shape
    tm = COMBINE_TM
    assert t % tm == 0
    kern = functools.partial(_combine_kernel, tm=tm)
    grid_spec = pltpu.PrefetchScalarGridSpec(
        num_scalar_prefetch=1,
        grid=(t // tm,),
        in_specs=[
            pl.BlockSpec((tm, d), lambda i, pos: (i, 0)),
            pl.BlockSpec((tm, LANES), lambda i, pos: (i, 0)),
            pl.BlockSpec((1, d), lambda i, pos: (0, 0)),
            pl.BlockSpec((2, d), lambda i, pos: (0, 0)),
            pl.BlockSpec(memory_space=pl.ANY),
        ],
        out_specs=pl.BlockSpec((tm, d), lambda i, pos: (i, 0)),
        scratch_shapes=[pltpu.VMEM((2, 2, tm, d), F32), pltpu.SemaphoreType.DMA((2,))],
    )
    return pl.pallas_call(
        kern,
        grid_spec=grid_spec,
        out_shape=jax.ShapeDtypeStruct((t, d), F32),
        compiler_params=_params(("arbitrary",)),
        name="combine",
    )(pos_flat, x, route, gpost.reshape(1, d), gate2, ys)


DENSE_TM = 768
LAT_TM = 1024


def _moe_plan(route, counts):
    t = route.shape[0]
    e = N_EXPERTS
    sb, nsb = MOE_SB, MOE_NSB
    i32 = jnp.int32
    cnt = counts[0, :e].astype(i32)
    nsub = (cnt + sb - 1) // sb
    row_end = jnp.cumsum(nsub) * sb
    row_start = row_end - nsub * sb
    ntile = (nsub + nsb - 1) // nsb
    tile_end = jnp.cumsum(ntile)
    tile_start = tile_end - ntile
    ex = route[:, R_E1:R_E2 + 1].astype(i32)
    rank = route[:, R_RANK1:R_RANK2 + 1].astype(i32)
    pos = (row_start[ex] + rank).reshape(-1)
    max_sub = (2 * t + e * (sb - 1)) // sb
    max_tiles = (max_sub + e * (nsb - 1)) // nsb
    m = jnp.arange(max_tiles + 1, dtype=i32)
    used = m < tile_end[-1]
    m_used = jnp.minimum(m, tile_end[-1] - 1)
    tile_expert = jnp.minimum(jnp.sum((tile_end[None, :] <= m_used[:, None]).astype(i32), axis=1), e - 1)
    local = m_used - tile_start[tile_expert]
    tile_nsb = jnp.where(used, jnp.clip(nsub[tile_expert] - nsb * local, 0, nsb), 0)
    tile_row0 = jnp.where(used, row_start[tile_expert] + local * (nsb * sb), row_end[-1])
    tok = jnp.repeat(jnp.arange(t, dtype=i32), 2)
    src = jnp.zeros((max_sub * sb,), i32).at[pos].set(tok, unique_indices=True)
    return pos, src, tile_expert[:-1].astype(i32), tile_nsb[:-1].astype(i32), tile_row0.astype(i32)


def kernel(x, c, ctx, c_ctx, w_mod, b_mod, g_mix_pre, g_mix_post, g_ffn_pre, g_ffn_post, w_in, sgu_ln_g,
           sgu_w, sgu_b, attn_sink, na_rpb, w_out, ffn_w_gu, ffn_w_down, moe_router, moe_w_gu, moe_w_down):
    assert x.shape[0] == 1 and ctx.shape[0] == 1
    n_lat, d = x.shape[1], x.shape[2]
    n_ctx = ctx.shape[1]
    depth = w_mod.shape[0]
    t_all = n_lat + n_ctx

    c8 = jnp.zeros((8, d), F32).at[0].set(c[0]).at[1].set(c_ctx)
    mod = _mod_call(c8, w_mod, b_mod)
    cos, sin = _rope_tables(n_lat, n_ctx)
    w_in_bf16 = w_in.astype(BF16)
    w_out_bf16 = w_out.astype(BF16)
    ffn_w_gu_bf16 = ffn_w_gu.astype(BF16)
    ffn_w_down_bf16 = ffn_w_down.astype(BF16)
    nb = B_HEADS * HEAD_DIM

    xa = jnp.concatenate([x[0], ctx[0]], axis=0)
    for l in range(depth):
        last = l == depth - 1
        m2 = mod[l, 0:2]
        sh1, sc1, g1, sh2, sc2, g2 = (m2[:, k * d:(k + 1) * d] for k in range(6))

        p = _inproj_call(xa, g_mix_pre[l], sh1, sc1, w_in_bf16, l, cos, sin, tm=DENSE_TM, n_lat=n_lat)
        y_sgu = _sgu_call(p, sgu_ln_g[l], sgu_w[l], sgu_b[l], chunks=DENSE_TM // CHUNK)
        y_win = _win_call(p, attn_sink[l], n_lat=n_lat, n_ctx=n_ctx)
        y_na = _na_call(p, na_rpb[l], n_lat=n_lat, n_ctx=n_ctx)
        if not last:
            sink12 = jnp.concatenate([attn_sink[l], jnp.full((C_HEADS,), NEG_INF, F32)])
            y_ctx = _ctx_attn_call(p, sink12, n_lat=n_lat, n_ctx=n_ctx)
            y_win = jnp.concatenate([y_win, y_ctx[:, :nb]], axis=0)
            y_na = jnp.concatenate([y_na, y_ctx[:, nb:]], axis=0)
            xa = _outproj_call(y_sgu, y_win, y_na, w_out_bf16, l, xa, g_mix_post[l], g1,
                               rows=t_all, tm=DENSE_TM, n_lat=n_lat)
        else:
            xa = _outproj_call(y_sgu, y_win, y_na, w_out_bf16, l, xa, g_mix_post[l], g1,
                               rows=n_lat, tm=LAT_TM, n_lat=n_lat)

        if l % 2 == 0:
            xa = _ffn_dense_call(xa, g_ffn_pre[l], sh2, sc2, ffn_w_gu_bf16, ffn_w_down_bf16, l // 2,
                                 g_ffn_post[l], g2, tm=DENSE_TM, n_lat=n_lat)
        else:
            assert last, "expert layers are only supported as the last layer (latent rows only)"
            route, counts = _router_call(xa, g_ffn_pre[l], sh2, sc2, moe_router[l // 2])
            pos, src, tile_expert, tile_nsb, tile_row0 = _moe_plan(route, counts)
            ys = _ffn_moe_call(xa, src, tile_expert, tile_nsb, tile_row0, g_ffn_pre[l], sh2, sc2,
                               moe_w_gu[l // 2], moe_w_down[l // 2])
            xa = _combine_call(xa, ys, pos, route, g_ffn_post[l], g2)
    return xa[:n_lat][None]
```

```python
import functools

import numpy as np
import jax
import jax.numpy as jnp
from jax import lax
from jax.experimental import pallas as pl
from jax.experimental.pallas import tpu as pltpu

F32 = jnp.float32
BF16 = jnp.bfloat16

HEAD_DIM = 128
GRID_W = 64
CHUNK = 128
WINDOW_BLOCK = 128
B_HEADS = 6
B_KV_HEADS = 2
C_HEADS = 6
NA_ROWS = 8
NA_COLS = 16
N_EXPERTS = 8
ROPE_BASE = 10000.0
EPS = 1e-6
NEG_INF = -1e30

V7X_VMEM_BYTES = 64 * 1024 * 1024
VMEM_LIMIT = V7X_VMEM_BYTES - 6 * 1024 * 1024
LANES = 128

A_WIDTH = 512
QB_COL = 2 * A_WIDTH // HEAD_DIM
QC_COL = QB_COL + B_HEADS
KB_COL = QC_COL + C_HEADS
VB_COL = KB_COL + B_KV_HEADS
KC_COL = VB_COL + B_KV_HEADS
VC_COL = KC_COL + C_HEADS
IN_COLS = VC_COL + C_HEADS


def _params(sem, vmem=VMEM_LIMIT):
    return pltpu.CompilerParams(dimension_semantics=sem, vmem_limit_bytes=vmem)


def _rms(x):
    return x * lax.rsqrt(jnp.mean(x * x, axis=-1, keepdims=True) + EPS)


def _pick_rows(v2, row0, tm, n_lat):
    rows = row0 + lax.broadcasted_iota(jnp.int32, (tm, 1), 0)
    return jnp.where(rows >= n_lat, v2[1:2, :], v2[0:1, :])


def _norm_mod(x, g, sh2, sc2, row0, tm, n_lat):
    y = _rms(x) * g
    return y * (1.0 + _pick_rows(sc2, row0, tm, n_lat)) + _pick_rows(sh2, row0, tm, n_lat)


def _dot(a, b):
    return jnp.dot(a, b, preferred_element_type=F32)


def _dot_nt(a, b):
    return lax.dot_general(a, b, (((1,), (1,)), ((), ())), preferred_element_type=F32)


def _mod_kernel(c_ref, w_ref, b_ref, o_ref):
    a = jax.nn.silu(c_ref[...]).astype(BF16)
    o_ref[...] = _dot(a, w_ref[...].astype(BF16)) + b_ref[...]


def _mod_call(c8, w_mod, b_mod):
    depth, d, n = w_mod.shape
    tn = 1536
    return pl.pallas_call(
        _mod_kernel,
        grid=(depth, n // tn),
        in_specs=[
            pl.BlockSpec((8, d), lambda l, j: (0, 0)),
            pl.BlockSpec((None, d, tn), lambda l, j: (l, 0, j)),
            pl.BlockSpec((None, 1, tn), lambda l, j: (l, 0, j)),
        ],
        out_specs=pl.BlockSpec((None, 8, tn), lambda l, j: (l, 0, j)),
        out_shape=jax.ShapeDtypeStruct((depth, 8, n), F32),
        compiler_params=_params(("parallel", "parallel")),
        name="modulation",
    )(c8, w_mod, b_mod.reshape(depth, 1, n))


def _rope_head(x, cos, sin):
    lane = lax.broadcasted_iota(jnp.int32, x.shape, 1)
    first = (lane & 63) < 32
    partner = jnp.where(first, pltpu.roll(x, 96, 1), pltpu.roll(x, 32, 1))
    return x * cos + partner * sin


def _inproj_kernel(x_ref, g_ref, sh_ref, sc_ref, w_ref, cos_ref, sin_ref, o_ref, h_ref, *, tm, n_lat):
    i = pl.program_id(0)
    j = pl.program_id(1)

    @pl.when(j == 0)
    def _():
        h = _norm_mod(x_ref[...], g_ref[...], sh_ref[...], sc_ref[...], i * tm, tm, n_lat)
        h_ref[...] = h.astype(BF16)

    acc = _dot(h_ref[...], w_ref[...])

    def store(kinds):
        for hd, kind in enumerate(kinds):
            a = acc[:, hd * HEAD_DIM:(hd + 1) * HEAD_DIM]
            if kind == "gelu":
                a = jax.nn.gelu(a, approximate=True)
            elif kind == "rope":
                a = _rope_head(a, cos_ref[...], sin_ref[...])
            o_ref[:, hd * HEAD_DIM:(hd + 1) * HEAD_DIM] = a.astype(BF16)

    heads_per_tile = INPROJ_TN // HEAD_DIM
    for jt in range(IN_COLS // heads_per_tile):
        @pl.when(j == jt)
        def _(jt=jt):
            store(INPROJ_HEAD_KIND[jt * heads_per_tile:(jt + 1) * heads_per_tile])


INPROJ_HEAD_KIND = (("gelu",) * QB_COL + ("rope",) * B_HEADS + ("none",) * C_HEADS
                    + ("rope",) * B_KV_HEADS + ("none",) * (B_KV_HEADS + 2 * C_HEADS))
INPROJ_TN = 12 * HEAD_DIM


def _inproj_call(xa, g, sh2, sc2, w_all, layer, cos, sin, *, tm, n_lat):
    t, d = xa.shape
    n = w_all.shape[2]
    tn = INPROJ_TN
    assert t % tm == 0 and n == IN_COLS * HEAD_DIM and n % tn == 0 and w_all.dtype == BF16
    kern = functools.partial(_inproj_kernel, tm=tm, n_lat=n_lat)
    return pl.pallas_call(
        kern,
        grid=(t // tm, n // tn),
        in_specs=[
            pl.BlockSpec((tm, d), lambda i, j: (i, 0)),
            pl.BlockSpec((1, d), lambda i, j: (0, 0)),
            pl.BlockSpec((2, d), lambda i, j: (0, 0)),
            pl.BlockSpec((2, d), lambda i, j: (0, 0)),
            pl.BlockSpec((None, d, tn), lambda i, j: (layer, 0, j)),
            pl.BlockSpec((tm, HEAD_DIM), lambda i, j: (i, 0)),
            pl.BlockSpec((tm, HEAD_DIM), lambda i, j: (i, 0)),
        ],
        out_specs=pl.BlockSpec((tm, tn), lambda i, j: (i, j)),
        out_shape=jax.ShapeDtypeStruct((t, n), BF16),
        scratch_shapes=[pltpu.VMEM((tm, d), BF16)],
        compiler_params=_params(("parallel", "arbitrary")),
        name="in_projection",
    )(xa, g.reshape(1, d), sh2, sc2, w_all, cos, sin)


def _rope_tables(n_lat, n_ctx):
    t = np.arange(n_lat)
    n = HEAD_DIM // 4
    inv = (ROPE_BASE ** (-(2.0 / (HEAD_DIM // 2)) * np.arange(n, dtype=np.float32))).astype(np.float32)
    ang_r = (t // GRID_W).astype(np.float32)[:, None] * inv[None, :]
    ang_c = (t % GRID_W).astype(np.float32)[:, None] * inv[None, :]
    cos = np.concatenate([np.cos(ang_r)] * 2 + [np.cos(ang_c)] * 2, axis=1)
    sin = np.concatenate([-np.sin(ang_r), np.sin(ang_r), -np.sin(ang_c), np.sin(ang_c)], axis=1)
    cos = np.concatenate([cos, np.ones((n_ctx, HEAD_DIM), np.float32)], axis=0)
    sin = np.concatenate([sin, np.zeros((n_ctx, HEAD_DIM), np.float32)], axis=0)
    return jnp.asarray(cos, F32), jnp.asarray(sin, F32)


def _sgu_kernel(u_ref, v_ref, lng_ref, w_ref, b_ref, o_ref, *, chunks):
    for c in range(chunks):
        rows = slice(c * CHUNK, (c + 1) * CHUNK)
        for g in range(A_WIDTH // HEAD_DIM):
            cols = slice(g * HEAD_DIM, (g + 1) * HEAD_DIM)
            v = v_ref[rows, cols].astype(F32)
            mu = jnp.mean(v, axis=-1, keepdims=True)
            var = jnp.mean(jnp.square(v - mu), axis=-1, keepdims=True)
            vn = (v - mu) * lax.rsqrt(var + EPS) * lng_ref[:, cols]
            s = _dot(w_ref[g].astype(BF16), vn.astype(BF16)) + b_ref[g]
            o_ref[rows, cols] = (u_ref[rows, cols].astype(F32) * s).astype(BF16)


def _sgu_call(p, ln_g, w_s, b_s, *, chunks):
    t = p.shape[0]
    rows = chunks * CHUNK
    assert t % rows == 0
    groups = A_WIDTH // HEAD_DIM
    b_full = jnp.broadcast_to(b_s[:, :, None], (groups, CHUNK, HEAD_DIM))
    kern = functools.partial(_sgu_kernel, chunks=chunks)
    return pl.pallas_call(
        kern,
        grid=(t // rows,),
        in_specs=[
            pl.BlockSpec((rows, A_WIDTH), lambda i: (i, 0)),
            pl.BlockSpec((rows, A_WIDTH), lambda i: (i, 1)),
            pl.BlockSpec((1, A_WIDTH), lambda i: (0, 0)),
            pl.BlockSpec((groups, CHUNK, CHUNK), lambda i: (0, 0, 0)),
            pl.BlockSpec((groups, CHUNK, HEAD_DIM), lambda i: (0, 0, 0)),
        ],
        out_specs=pl.BlockSpec((rows, A_WIDTH), lambda i: (i, 0)),
        out_shape=jax.ShapeDtypeStruct((t, A_WIDTH), BF16),
        compiler_params=_params(("parallel",)),
        name="spatial_gating",
    )(p, p, ln_g.reshape(1, A_WIDTH), w_s, b_full)


WIN_Q_BLOCKS = 4


def _softmax_pv(s_loc, s_ctx, sink_col, v_loc, v_ctx):
    n_loc = s_loc.shape[1]
    s = jnp.concatenate([s_loc, s_ctx], axis=1)
    m = jnp.max(s, axis=1, keepdims=True)
    if sink_col is not None:
        m = jnp.maximum(m, sink_col)
    p = jnp.exp(s - m)
    den = jnp.sum(p, axis=1, keepdims=True)
    if sink_col is not None:
        den = den + jnp.exp(sink_col - m)
    p = p.astype(BF16)
    o = _dot(p[:, :n_loc], v_loc) + _dot(p[:, n_loc:], v_ctx)
    return o / den


def _win_kernel(sink_ref, q0, q1, q2, kp, km, kn, vp, vm, vn, kc, vc, o_ref, kcat, vcat, *, n_lat):
    kv = pl.program_id(0)
    s = pl.program_id(1)
    wb = WINDOW_BLOCK
    main = WIN_Q_BLOCKS * wb
    kcat[0:wb] = kp[...]
    kcat[wb:wb + main] = km[...]
    kcat[wb + main:2 * wb + main] = kn[...]
    vcat[0:wb] = vp[...]
    vcat[wb:wb + main] = vm[...]
    vcat[wb + main:2 * wb + main] = vn[...]
    scale = HEAD_DIM ** -0.5
    g = B_HEADS // B_KV_HEADS
    qs = (q0, q1, q2)
    row = lax.broadcasted_iota(jnp.int32, (g * wb, 3 * wb), 0) & (wb - 1)
    col = lax.broadcasted_iota(jnp.int32, (g * wb, 3 * wb), 1)
    rel = col - wb - row
    band_bias = jnp.where(rel < -wb, NEG_INF, jnp.where(rel > wb, NEG_INF, 0.0))
    col1 = lax.broadcasted_iota(jnp.int32, (1, 3 * wb), 1)
    sink_col = jnp.concatenate(
        [jnp.full((wb, 1), sink_ref[kv * g + gi], F32) for gi in range(g)], axis=0)
    n_blocks = n_lat // wb
    for b in range(WIN_Q_BLOCKS):
        n = s * WIN_Q_BLOCKS + b
        rows = slice(b * wb, (b + 1) * wb)
        q3 = jnp.concatenate([qs[gi][rows, :] for gi in range(g)], axis=0)
        keys = kcat[b * wb:(b + 3) * wb, :]
        vals = vcat[b * wb:(b + 3) * wb, :]
        off_start = jnp.where(n == 0, NEG_INF, 0.0)
        off_end = jnp.where(n == n_blocks - 1, NEG_INF, 0.0)
        edge = jnp.where(col1 < wb, off_start, jnp.where(col1 >= 2 * wb, off_end, 0.0))
        s_loc = _dot_nt(q3, keys) * scale + band_bias + edge
        s_ctx = _dot_nt(q3, kc[...]) * scale
        o = _softmax_pv(s_loc, s_ctx, sink_col, vals, vc[...])
        for gi in range(g):
            o_ref[rows, gi * HEAD_DIM:(gi + 1) * HEAD_DIM] = o[gi * wb:(gi + 1) * wb, :].astype(BF16)


def _win_call(p, sink, *, n_lat, n_ctx):
    wb = WINDOW_BLOCK
    main = WIN_Q_BLOCKS * wb
    assert n_lat % main == 0 and n_lat % n_ctx == 0
    nsb = n_lat // main
    nb = n_lat // wb
    g = B_HEADS // B_KV_HEADS
    ctx_blk = n_lat // n_ctx

    def qspec(gi):
        return pl.BlockSpec((main, HEAD_DIM), lambda kv, s: (s, QB_COL + kv * g + gi))

    def band_specs(col0):
        return [
            pl.BlockSpec((wb, HEAD_DIM), lambda kv, s: (jnp.maximum(s * WIN_Q_BLOCKS - 1, 0), col0 + kv)),
            pl.BlockSpec((main, HEAD_DIM), lambda kv, s: (s, col0 + kv)),
            pl.BlockSpec((wb, HEAD_DIM), lambda kv, s: (jnp.minimum((s + 1) * WIN_Q_BLOCKS, nb - 1), col0 + kv)),
        ]

    def ctx_spec(col0):
        return pl.BlockSpec((n_ctx, HEAD_DIM), lambda kv, s: (ctx_blk, col0 + kv))

    kern = functools.partial(_win_kernel, n_lat=n_lat)
    return pl.pallas_call(
        kern,
        grid=(B_KV_HEADS, nsb),
        in_specs=[pl.BlockSpec(memory_space=pltpu.SMEM), qspec(0), qspec(1), qspec(2)]
        + band_specs(KB_COL) + band_specs(VB_COL) + [ctx_spec(KB_COL), ctx_spec(VB_COL)],
        out_specs=pl.BlockSpec((main, g * HEAD_DIM), lambda kv, s: (s, kv)),
        out_shape=jax.ShapeDtypeStruct((n_lat, B_HEADS * HEAD_DIM), BF16),
        scratch_shapes=[pltpu.VMEM((main + 2 * wb, HEAD_DIM), BF16)] * 2,
        compiler_params=_params(("parallel", "parallel")),
        name="window_attention",
    )(sink, p, p, p, p, p, p, p, p, p, p, p)


NA_Q_ROWS = 8
NA_K_ROWS = 16


def _na_row_windows(group, n_rows):
    lead = (NA_K_ROWS - NA_Q_ROWS) // 2
    rq = group * NA_Q_ROWS + np.arange(NA_Q_ROWS)
    rk = group * NA_Q_ROWS - lead + np.arange(NA_K_ROWS)
    r0 = np.clip(rq - NA_ROWS // 2, 0, n_rows - NA_ROWS)
    valid = (rk[None, :] >= r0[:, None]) & (rk[None, :] < r0[:, None] + NA_ROWS)
    roff = rk[None, :] - rq[:, None] + (NA_ROWS - 1)
    return valid, roff


def _na_fill_bias(bt_ref, mb_ref, group, n_rows):
    valid, roff = _na_row_windows(group, n_rows)
    w = GRID_W
    left = lax.broadcasted_iota(jnp.int32, (w, 2 * w), 1) < w
    neg = jnp.full((w, 2 * w), NEG_INF, F32)
    for i in range(NA_Q_ROWS):
        for u in range(0, NA_K_ROWS, 2):
            lo = bt_ref[int(roff[i, u])] if valid[i, u] else neg
            hi = bt_ref[int(roff[i, u + 1])] if valid[i, u + 1] else neg
            blk = neg if not (valid[i, u] or valid[i, u + 1]) else jnp.where(left, lo, hi)
            mb_ref[i * w:(i + 1) * w, u * w:(u + 2) * w] = blk


def _na_kernel(q, kp, km, kn, vp, vm, vn, kc, vc, bt, o, kcat, vcat, mb, *, n_rows):
    a = pl.program_id(1)
    ng = n_rows // NA_Q_ROWS
    for group in sorted({0, min(1, ng - 1), ng - 1}):
        @pl.when(a == group)
        def _(group=group):
            _na_fill_bias(bt, mb, group, n_rows)

    half = (NA_K_ROWS - NA_Q_ROWS) // 2 * GRID_W
    main = NA_Q_ROWS * GRID_W
    kcat[0:half] = kp[...]
    kcat[half:half + main] = km[...]
    kcat[half + main:2 * half + main] = kn[...]
    vcat[0:half] = vp[...]
    vcat[half:half + main] = vm[...]
    vcat[half + main:2 * half + main] = vn[...]
    scale = HEAD_DIM ** -0.5
    qv = q[...]
    s_loc = _dot_nt(qv, kcat[...]) * scale + mb[...]
    s_ctx = _dot_nt(qv, kc[...]) * scale
    o[...] = _softmax_pv(s_loc, s_ctx, None, vcat[...], vc[...]).astype(BF16)


def _na_col_table(rpb):
    w = GRID_W
    cq = np.arange(w)
    c0 = np.clip(cq - NA_COLS // 2, 0, w - NA_COLS)
    cmask = (cq[None, :] >= c0[:, None]) & (cq[None, :] < c0[:, None] + NA_COLS)
    coff = np.clip(cq[None, :] - cq[:, None] + (NA_COLS - 1), 0, 2 * NA_COLS - 2)
    onehot = (coff[None] == np.arange(2 * NA_COLS - 1)[:, None, None]).astype(np.float32)
    picked = jnp.einsum("hrj,jqk->hrqk", rpb.astype(F32), onehot, precision=lax.Precision.HIGHEST)
    by_col = jnp.where(cmask[None, None], picked, NEG_INF)
    return jnp.concatenate([by_col, by_col], axis=-1)


def _na_call(p, rpb, *, n_lat, n_ctx):
    main = NA_Q_ROWS * GRID_W
    half = (NA_K_ROWS - NA_Q_ROWS) // 2 * GRID_W
    assert n_lat % main == 0 and n_lat % n_ctx == 0 and main == 2 * half
    ng = n_lat // main
    assert ng >= 2
    nhalf = n_lat // half
    ctx_blk = n_lat // n_ctx
    bt = _na_col_table(rpb)
    kern = functools.partial(_na_kernel, n_rows=n_lat // GRID_W)

    def band_specs(col0):
        return [
            pl.BlockSpec((half, HEAD_DIM), lambda h, a: (jnp.maximum(2 * a - 1, 0), col0 + h)),
            pl.BlockSpec((main, HEAD_DIM), lambda h, a: (a, col0 + h)),
            pl.BlockSpec((half, HEAD_DIM), lambda h, a: (jnp.minimum(2 * a + 2, nhalf - 1), col0 + h)),
        ]

    return pl.pallas_call(
        kern,
        grid=(C_HEADS, ng),
        in_specs=[pl.BlockSpec((main, HEAD_DIM), lambda h, a: (a, QC_COL + h))]
        + band_specs(KC_COL) + band_specs(VC_COL)
        + [pl.BlockSpec((n_ctx, HEAD_DIM), lambda h, a: (ctx_blk, KC_COL + h)),
           pl.BlockSpec((n_ctx, HEAD_DIM), lambda h, a: (ctx_blk, VC_COL + h)),
           pl.BlockSpec((None,) + bt.shape[1:], lambda h, a: (h, 0, 0, 0))],
        out_specs=pl.BlockSpec((main, HEAD_DIM), lambda h, a: (a, h)),
        out_shape=jax.ShapeDtypeStruct((n_lat, C_HEADS * HEAD_DIM), BF16),
        scratch_shapes=[pltpu.VMEM((main + 2 * half, HEAD_DIM), BF16)] * 2
        + [pltpu.VMEM((main, NA_K_ROWS * GRID_W), F32)],
        compiler_params=_params(("arbitrary", "arbitrary")),
        name="neighbourhood_attention",
    )(p, p, p, p, p, p, p, p, p, bt)


def _ctx_attn_kernel(sink_ref, q, k, v, o):
    hh = pl.program_id(0)
    scale = HEAD_DIM ** -0.5
    s = _dot_nt(q[...], k[...]) * scale
    sink = jnp.full((s.shape[0], 1), sink_ref[hh], F32)
    m = jnp.maximum(jnp.max(s, axis=1, keepdims=True), sink)
    pr = jnp.exp(s - m)
    den = jnp.sum(pr, axis=1, keepdims=True) + jnp.exp(sink - m)
    o[...] = (_dot(pr.astype(BF16), v[...]) / den).astype(BF16)


def _ctx_attn_call(p, sink12, *, n_lat, n_ctx):
    blk = n_lat // n_ctx
    g = B_HEADS // B_KV_HEADS

    def kcol(hh):
        return jnp.where(hh < B_HEADS, KB_COL + hh // g, KC_COL + hh - B_HEADS)

    def vcol(hh):
        return jnp.where(hh < B_HEADS, VB_COL + hh // g, VC_COL + hh - B_HEADS)

    return pl.pallas_call(
        _ctx_attn_kernel,
        grid=(B_HEADS + C_HEADS,),
        in_specs=[
            pl.BlockSpec(memory_space=pltpu.SMEM),
            pl.BlockSpec((n_ctx, HEAD_DIM), lambda hh: (blk, QB_COL + hh)),
            pl.BlockSpec((n_ctx, HEAD_DIM), lambda hh: (blk, kcol(hh))),
            pl.BlockSpec((n_ctx, HEAD_DIM), lambda hh: (blk, vcol(hh))),
        ],
        out_specs=pl.BlockSpec((n_ctx, HEAD_DIM), lambda hh: (0, hh)),
        out_shape=jax.ShapeDtypeStruct((n_ctx, (B_HEADS + C_HEADS) * HEAD_DIM), BF16),
        compiler_params=_params(("parallel",)),
        name="context_attention",
    )(sink12, p, p, p)


def _outproj_kernel(ya_ref, yb_ref, yc_ref, w_ref, x_ref, g_ref, gate_ref, o_ref, *, tm, n_lat):
    i = pl.program_id(0)
    ka = ya_ref.shape[1]
    kb = yb_ref.shape[1]
    acc = _dot(ya_ref[...], w_ref[0:ka, :])
    acc += _dot(yb_ref[...], w_ref[ka:ka + kb, :])
    acc += _dot(yc_ref[...], w_ref[ka + kb:, :])
    r = _rms(acc) * g_ref[...]
    o_ref[...] = x_ref[...] + _pick_rows(gate_ref[...], i * tm, tm, n_lat) * r


def _outproj_call(ya, yb, yc, w_all_bf16, layer, xa, g, gate2, *, rows, tm, n_lat):
    d = xa.shape[1]
    assert rows % tm == 0 and ya.shape[1] + yb.shape[1] + yc.shape[1] == d
    kern = functools.partial(_outproj_kernel, tm=tm, n_lat=n_lat)
    return pl.pallas_call(
        kern,
        grid=(rows // tm,),
        in_specs=[
            pl.BlockSpec((tm, ya.shape[1]), lambda i: (i, 0)),
            pl.BlockSpec((tm, yb.shape[1]), lambda i: (i, 0)),
            pl.BlockSpec((tm, yc.shape[1]), lambda i: (i, 0)),
            pl.BlockSpec((None, d, d), lambda i: (layer, 0, 0)),
            pl.BlockSpec((tm, d), lambda i: (i, 0)),
            pl.BlockSpec((1, d), lambda i: (0, 0)),
            pl.BlockSpec((2, d), lambda i: (0, 0)),
        ],
        out_specs=pl.BlockSpec((tm, d), lambda i: (i, 0)),
        out_shape=jax.ShapeDtypeStruct((rows, d), F32),
        compiler_params=_params(("parallel",)),
        name="out_projection",
    )(ya, yb, yc, w_all_bf16, xa, g.reshape(1, d), gate2)


FFN_TF = 256


def _swiglu_accumulate(h_ref, wg_ref, wu_ref, wd_ref, o_ref):
    h = h_ref[...]
    gt = _dot(h, wg_ref[...].astype(BF16))
    up = _dot(h, wu_ref[...].astype(BF16))
    act = (jax.nn.silu(gt) * up).astype(BF16)
    o_ref[...] += _dot(act, wd_ref[...].astype(BF16))


def _ffn_weight_specs(d, tf, nf, expert_of, chunk_of=lambda m, f, *pf: f):
    return [
        pl.BlockSpec((None, d, tf), lambda m, f, *pf: (expert_of(m, *pf), 0, chunk_of(m, f, *pf))),
        pl.BlockSpec((None, d, tf), lambda m, f, *pf: (expert_of(m, *pf), 0, nf + chunk_of(m, f, *pf))),
        pl.BlockSpec((None, tf, d), lambda m, f, *pf: (expert_of(m, *pf), chunk_of(m, f, *pf), 0)),
    ]


def _ffn_dense_kernel(x_ref, gpre_ref, sh_ref, sc_ref, wg_ref, wu_ref, wd_ref, gpost_ref, gate_ref,
                      o_ref, h_ref, *, tm, n_lat):
    m = pl.program_id(0)
    f = pl.program_id(1)

    @pl.when(f == 0)
    def _():
        h = _norm_mod(x_ref[...], gpre_ref[...], sh_ref[...], sc_ref[...], m * tm, tm, n_lat)
        h_ref[...] = h.astype(BF16)
        o_ref[...] = jnp.zeros_like(o_ref)

    _swiglu_accumulate(h_ref, wg_ref, wu_ref, wd_ref, o_ref)

    @pl.when(f == pl.num_programs(1) - 1)
    def _():
        r = _rms(o_ref[...]) * gpost_ref[...]
        o_ref[...] = x_ref[...] + _pick_rows(gate_ref[...], m * tm, tm, n_lat) * r


def _ffn_dense_call(xa, gpre, sh2, sc2, w_gu, w_down, layer_set, gpost, gate2, *, tm, n_lat):
    t, d = xa.shape
    ffn = w_gu.shape[2] // 2
    tf = FFN_TF
    nf = ffn // tf
    assert t % tm == 0 and ffn % tf == 0
    kern = functools.partial(_ffn_dense_kernel, tm=tm, n_lat=n_lat)
    vec = lambda rows: pl.BlockSpec((rows, d), lambda m, f: (0, 0))
    return pl.pallas_call(
        kern,
        grid=(t // tm, nf),
        in_specs=[pl.BlockSpec((tm, d), lambda m, f: (m, 0), pipeline_mode=pl.Buffered(1)),
                  vec(1), vec(2), vec(2)]
        + _ffn_weight_specs(d, tf, nf, lambda m: layer_set) + [vec(1), vec(2)],
        out_specs=pl.BlockSpec((tm, d), lambda m, f: (m, 0)),
        out_shape=jax.ShapeDtypeStruct((t, d), F32),
        scratch_shapes=[pltpu.VMEM((tm, d), BF16)],
        compiler_params=_params(("parallel", "arbitrary")),
        name="swiglu_ffn",
    )(xa, gpre.reshape(1, d), sh2, sc2, w_gu, w_gu, w_down, gpost.reshape(1, d), gate2)


MOE_SB = 512
MOE_NSB = 4


DISPATCH_TM = 512


def _dispatch_kernel(pos_ref, pad0_ref, padn_ref, tail_ref, x_ref, gpre_ref, sh_ref, sc_ref, xs_hbm, hbuf, sems,
                     *, tm, sb):
    i = pl.program_id(0)
    last = pl.num_programs(0) - 1
    h = _rms(x_ref[...]) * gpre_ref[...]
    hbuf[i % 2] = h * (1.0 + sc_ref[0:1, :]) + sh_ref[0:1, :]

    def copy(step, t, k):
        p = pos_ref[(step * tm + t) * 2 + k]
        return pltpu.make_async_copy(hbuf.at[step % 2, pl.ds(t, 1)], xs_hbm.at[pl.ds(p, 1)], sems.at[step % 2])

    def issue(t, carry):
        copy(i, t, 0).start()
        copy(i, t, 1).start()
        return carry

    lax.fori_loop(0, tm, issue, 0, unroll=8)

    def drain_step(step):
        def drain(t, carry):
            copy(step, t, 0).wait()
            copy(step, t, 1).wait()
            return carry

        lax.fori_loop(0, tm, drain, 0, unroll=8)

    @pl.when(i > 0)
    def _():
        drain_step(i - 1)

    @pl.when(i == last)
    def _():
        drain_step(i)
        hbuf[0] = jnp.zeros(hbuf.shape[1:], F32)
        for e in range(N_EXPERTS):
            def zero_row(r, carry, e=e):
                cp = pltpu.make_async_copy(hbuf.at[0, pl.ds(0, 1)], xs_hbm.at[pl.ds(pad0_ref[e] + r, 1)], sems.at[0])
                cp.start()
                cp.wait()
                return carry

            lax.fori_loop(0, padn_ref[e], zero_row, 0)

        def zero_block(g, carry):
            row = pl.multiple_of(tail_ref[0] + g * sb, sb)
            cp = pltpu.make_async_copy(hbuf.at[0, pl.ds(0, sb)], xs_hbm.at[pl.ds(row, sb)], sems.at[0])
            cp.start()
            cp.wait()
            return carry

        lax.fori_loop(0, (xs_hbm.shape[0] - tail_ref[0]) // sb, zero_block, 0)


def _dispatch_call(x, pos, pad0, padn, tail, gpre, sh2, sc2, rows_out):
    t, d = x.shape
    tm = DISPATCH_TM
    assert t % tm == 0 and tm >= MOE_SB
    kern = functools.partial(_dispatch_kernel, tm=tm, sb=MOE_SB)
    vec = lambda r: pl.BlockSpec((r, d), lambda i, *pf: (0, 0))
    grid_spec = pltpu.PrefetchScalarGridSpec(
        num_scalar_prefetch=4,
        grid=(t // tm,),
        in_specs=[pl.BlockSpec((tm, d), lambda i, *pf: (i, 0)), vec(1), vec(2), vec(2)],
        out_specs=pl.BlockSpec(memory_space=pl.ANY),
        scratch_shapes=[pltpu.VMEM((2, tm, d), F32), pltpu.SemaphoreType.DMA((2,))],
    )
    return pl.pallas_call(
        kern,
        grid_spec=grid_spec,
        out_shape=jax.ShapeDtypeStruct((rows_out, d), F32),
        compiler_params=_params(("arbitrary",)),
        name="dispatch",
    )(pos, pad0, padn, tail, x, gpre.reshape(1, d), sh2, sc2)


def _ffn_moe_kernel(te_ref, ns_ref, r0_ref, xs_hbm, wg_ref, wu_ref, wd_ref, o_hbm,
                    stage, h_ref, acc, wgb, wub, wdb, gsem, osem, *, sb, nsb):
    m = pl.program_id(0)
    f = pl.program_id(1)
    n = ns_ref[m]
    n_tiles = pl.num_programs(0)
    cur = m % 2

    def out_copy(mm, s):
        row = pl.multiple_of(r0_ref[mm] + s * sb, sb)
        return pltpu.make_async_copy(acc.at[pl.ds(s * sb, sb)], o_hbm.at[pl.ds(row, sb)], osem)

    def in_copy(mm, s):
        row = pl.multiple_of(r0_ref[mm] + s * sb, sb)
        return pltpu.make_async_copy(xs_hbm.at[pl.ds(row, sb)], stage, gsem)

    def for_valid(count, body):
        for s in range(nsb):
            @pl.when(s < count)
            def _(s=s):
                body(s)

    @pl.when(f == 0)
    def _():
        @pl.when(m == 0)
        def _():
            def load(s):
                in_copy(m, s).start()
                in_copy(m, s).wait()
                h_ref[0, s * sb:(s + 1) * sb, :] = stage[...].astype(BF16)

            for_valid(n, load)

        @pl.when(m > 0)
        def _():
            for_valid(ns_ref[jnp.maximum(m - 1, 0)], lambda s: out_copy(m - 1, s).wait())

        def clear(s):
            acc[s * sb:(s + 1) * sb, :] = jnp.zeros((sb, acc.shape[1]), F32)

        for_valid(n, clear)

    nxt = jnp.minimum(m + 1, n_tiles - 1)
    n_next = jnp.where(m + 1 < n_tiles, ns_ref[nxt], 0)
    s_next = f // 2

    @pl.when((f % 2 == 0) & (s_next < n_next))
    def _():
        in_copy(nxt, s_next).start()

    @pl.when((f % 2 == 1) & (s_next < n_next))
    def _():
        in_copy(nxt, s_next).wait()
        h_ref[1 - cur, pl.ds(pl.multiple_of(s_next * sb, sb), sb), :] = stage[...].astype(BF16)

    def accumulate(s):
        if s == 0:
            wg = wg_ref[...].astype(BF16)
            wu = wu_ref[...].astype(BF16)
            wd = wd_ref[...].astype(BF16)
            wgb[...] = wg
            wub[...] = wu
            wdb[...] = wd
        else:
            wg, wu, wd = wgb[...], wub[...], wdb[...]
        h = h_ref[cur, s * sb:(s + 1) * sb, :]
        act = (jax.nn.silu(_dot(h, wg)) * _dot(h, wu)).astype(BF16)
        acc[s * sb:(s + 1) * sb, :] += _dot(act, wd)

    for_valid(n, accumulate)

    @pl.when(f == pl.num_programs(1) - 1)
    def _():
        for_valid(n, lambda s: out_copy(m, s).start())

        @pl.when(m == n_tiles - 1)
        def _():
            for_valid(n, lambda s: out_copy(m, s).wait())
            stage[...] = jnp.zeros(stage.shape, F32)

            def fill(g, carry):
                row = pl.multiple_of(r0_ref[n_tiles] + g * sb, sb)
                cp = pltpu.make_async_copy(stage, o_hbm.at[pl.ds(row, sb)], osem)
                cp.start()
                cp.wait()
                return carry

            lax.fori_loop(0, (o_hbm.shape[0] - r0_ref[n_tiles]) // sb, fill, 0)


def _ffn_moe_call(xs, tile_expert, tile_nsb, tile_row0, w_gu, w_down):
    rows, d = xs.shape
    sb, nsb = MOE_SB, MOE_NSB
    n_tiles = tile_expert.shape[0]
    ffn = w_gu.shape[2] // 2
    tf = FFN_TF
    nf = ffn // tf
    assert rows % sb == 0 and tile_row0.shape[0] == n_tiles + 1 and ffn % tf == 0 and nf >= 2 * nsb
    kern = functools.partial(_ffn_moe_kernel, sb=sb, nsb=nsb)
    grid_spec = pltpu.PrefetchScalarGridSpec(
        num_scalar_prefetch=3,
        grid=(n_tiles, nf),
        in_specs=[pl.BlockSpec(memory_space=pl.ANY)]
        + _ffn_weight_specs(d, tf, nf, lambda m, te, *pf: te[m],
                            lambda m, f, te, ns, *pf: jnp.where(ns[m] > 0, f, nf - 1)),
        out_specs=pl.BlockSpec(memory_space=pl.ANY),
        scratch_shapes=[pltpu.VMEM((sb, d), F32), pltpu.VMEM((2, sb * nsb, d), BF16),
                        pltpu.VMEM((sb * nsb, d), F32),
                        pltpu.VMEM((d, tf), BF16), pltpu.VMEM((d, tf), BF16), pltpu.VMEM((tf, d), BF16),
                        pltpu.SemaphoreType.DMA(()), pltpu.SemaphoreType.DMA(())],
    )
    return pl.pallas_call(
        kern,
        grid_spec=grid_spec,
        out_shape=jax.ShapeDtypeStruct((rows, d), F32),
        compiler_params=_params(("arbitrary", "arbitrary")),
        name="expert_ffn",
    )(tile_expert, tile_nsb, tile_row0, xs, w_gu, w_gu, w_down)


ROUTER_TM = 512
R_E1, R_E2, R_W1, R_W2, R_RANK1, R_RANK2 = range(6)


def _router_kernel(x_ref, gpre_ref, sh_ref, sc_ref, wr_ref, route_ref, count_ref, carry_ref, *, tm):
    i = pl.program_id(0)

    @pl.when(i == 0)
    def _():
        carry_ref[...] = jnp.zeros_like(carry_ref)

    h = _rms(x_ref[...]) * gpre_ref[...]
    h = h * (1.0 + sc_ref[0:1, :]) + sh_ref[0:1, :]
    w = wr_ref[...]
    h_hi = h.astype(BF16)
    h_lo = (h - h_hi.astype(F32)).astype(BF16)
    w_hi = w.astype(BF16)
    w_lo = (w - w_hi.astype(F32)).astype(BF16)
    logits = _dot(h_hi, w_hi) + (_dot(h_hi, w_lo) + _dot(h_lo, w_hi))
    lane_i = lax.broadcasted_iota(jnp.int32, logits.shape, 1)
    lane = lane_i.astype(F32)
    logits = jnp.where(lane_i < N_EXPERTS, logits, -jnp.inf)
    v1 = jnp.max(logits, axis=1, keepdims=True)
    e1 = jnp.min(jnp.where(logits == v1, lane, float(LANES)), axis=1, keepdims=True)
    rest = jnp.where(lane == e1, -jnp.inf, logits)
    v2 = jnp.max(rest, axis=1, keepdims=True)
    e2 = jnp.min(jnp.where(rest == v2, lane, float(LANES)), axis=1, keepdims=True)
    ex = jnp.exp(v2 - v1)
    w1 = 1.0 / (1.0 + ex)
    w2 = ex / (1.0 + ex)
    hit1 = lane == e1
    hit2 = lane == e2
    assign = jnp.where(hit1, 1.0, jnp.where(hit2, 1.0, 0.0))
    r = lax.broadcasted_iota(jnp.int32, (tm, tm), 0)
    c = lax.broadcasted_iota(jnp.int32, (tm, tm), 1)
    before = jnp.where(c < r, 1.0, 0.0).astype(BF16)
    prefix = _dot(before, assign.astype(BF16)) + carry_ref[0:1, :]
    rank1 = jnp.sum(jnp.where(hit1, prefix, 0.0), axis=1, keepdims=True)
    rank2 = jnp.sum(jnp.where(hit2, prefix, 0.0), axis=1, keepdims=True)
    total = carry_ref[0:1, :] + jnp.sum(assign, axis=0, keepdims=True)
    carry_ref[...] = jnp.broadcast_to(total, carry_ref.shape)
    count_ref[...] = jnp.broadcast_to(total, count_ref.shape)
    rec = jnp.zeros(logits.shape, F32)
    for k, val in ((R_E1, e1), (R_E2, e2), (R_W1, w1), (R_W2, w2), (R_RANK1, rank1), (R_RANK2, rank2)):
        rec = jnp.where(lane_i == k, val, rec)
    route_ref[...] = rec


def _router_call(x, gpre, sh2, sc2, w_router):
    t, d = x.shape
    tm = ROUTER_TM
    assert t % tm == 0
    wr = jnp.zeros((d, LANES), F32).at[:, :N_EXPERTS].set(w_router)
    kern = functools.partial(_router_kernel, tm=tm)
    return pl.pallas_call(
        kern,
        grid=(t // tm,),
        in_specs=[
            pl.BlockSpec((tm, d), lambda i: (i, 0)),
            pl.BlockSpec((1, d), lambda i: (0, 0)),
            pl.BlockSpec((2, d), lambda i: (0, 0)),
            pl.BlockSpec((2, d), lambda i: (0, 0)),
            pl.BlockSpec((d, LANES), lambda i: (0, 0)),
        ],
        out_specs=[pl.BlockSpec((tm, LANES), lambda i: (i, 0)),
                   pl.BlockSpec((8, LANES), lambda i: (0, 0))],
        out_shape=[jax.ShapeDtypeStruct((t, LANES), F32), jax.ShapeDtypeStruct((8, LANES), F32)],
        scratch_shapes=[pltpu.VMEM((8, LANES), F32)],
        compiler_params=_params(("arbitrary",)),
        name="router_top2",
    )(x, gpre.reshape(1, d), sh2, sc2, wr)


COMBINE_TM = 256


def _combine_kernel(pos_ref, x_ref, route_ref, gpost_ref, gate_ref, ys_hbm, o_ref, buf, sem, *, tm):
    i = pl.program_id(0)
    slot = i % 2

    def copy(step, t, k):
        p = pos_ref[(step * tm + t) * 2 + k]
        return pltpu.make_async_copy(ys_hbm.at[pl.ds(p, 1)], buf.at[step % 2, k, pl.ds(t, 1)], sem.at[step % 2])

    def fetch(step):
        def issue(t, carry):
            copy(step, t, 0).start()
            copy(step, t, 1).start()
            return carry

        lax.fori_loop(0, tm, issue, 0, unroll=8)

    @pl.when(i == 0)
    def _():
        fetch(i)

    @pl.when(i + 1 < pl.num_programs(0))
    def _():
        fetch(i + 1)

    def drain(t, carry):
        copy(i, t, 0).wait()
        copy(i, t, 1).wait()
        return carry

    lax.fori_loop(0, tm, drain, 0, unroll=8)
    rec = route_ref[...]
    y = rec[:, R_W1:R_W1 + 1] * buf[slot, 0] + rec[:, R_W2:R_W2 + 1] * buf[slot, 1]
    o_ref[...] = x_ref[...] + gate_ref[0:1, :] * (_rms(y) * gpost_ref[...])


def _combine_call(x, ys, pos_flat, route, gpost, gate2):
    t, d = x.shape
    tm = COMBINE_TM
    assert t % tm == 0
    kern = functools.partial(_combine_kernel, tm=tm)
    grid_spec = pltpu.PrefetchScalarGridSpec(
        num_scalar_prefetch=1,
        grid=(t // tm,),
        in_specs=[
            pl.BlockSpec((tm, d), lambda i, pos: (i, 0)),
            pl.BlockSpec((tm, LANES), lambda i, pos: (i, 0)),
            pl.BlockSpec((1, d), lambda i, pos: (0, 0)),
            pl.BlockSpec((2, d), lambda i, pos: (0, 0)),
            pl.BlockSpec(memory_space=pl.ANY),
        ],
        out_specs=pl.BlockSpec((tm, d), lambda i, pos: (i, 0)),
        scratch_shapes=[pltpu.VMEM((2, 2, tm, d), F32), pltpu.SemaphoreType.DMA((2,))],
    )
    return pl.pallas_call(
        kern,
        grid_spec=grid_spec,
        out_shape=jax.ShapeDtypeStruct((t, d), F32),
        compiler_params=_params(("arbitrary",)),
        name="combine",
    )(pos_flat, x, route, gpost.reshape(1, d), gate2, ys)


DENSE_TM = 768
LAT_TM = 1024


def _moe_plan(route, counts):
    t = route.shape[0]
    e = N_EXPERTS
    sb, nsb = MOE_SB, MOE_NSB
    i32 = jnp.int32
    cnt = counts[0, :e].astype(i32)
    nsub = (cnt + sb - 1) // sb
    row_end = jnp.cumsum(nsub) * sb
    row_start = row_end - nsub * sb
    ntile = (nsub + nsb - 1) // nsb
    tile_end = jnp.cumsum(ntile)
    tile_start = tile_end - ntile
    ex = route[:, R_E1:R_E2 + 1].astype(i32)
    rank = route[:, R_RANK1:R_RANK2 + 1].astype(i32)
    pos = (row_start[ex] + rank).reshape(-1)
    max_sub = (2 * t + e * (sb - 1)) // sb
    max_tiles = (max_sub + e * (nsb - 1)) // nsb
    m = jnp.arange(max_tiles + 1, dtype=i32)
    used = m < tile_end[-1]
    m_used = jnp.minimum(m, tile_end[-1] - 1)
    te = jnp.minimum(jnp.sum((tile_end[None, :] <= m_used[:, None]).astype(i32), axis=1), e - 1)
    local = m_used - tile_start[te]
    per = nsub[te] // jnp.maximum(ntile[te], 1)
    extra = nsub[te] - per * ntile[te]
    tile_nsb = jnp.where(used, per + (local < extra).astype(i32), 0)
    first_sub = local * per + jnp.minimum(local, extra)
    tile_row0 = jnp.where(used, row_start[te] + first_sub * sb, row_end[-1])
    return dict(pos=pos, tile_expert=te[:-1].astype(i32), tile_nsb=tile_nsb[:-1].astype(i32),
                tile_row0=tile_row0.astype(i32), pad_start=(row_start + cnt).astype(i32),
                pad_len=(nsub * sb - cnt).astype(i32), tail=row_end[-1:].astype(i32), rows=max_sub * sb)


def kernel(x, c, ctx, c_ctx, w_mod, b_mod, g_mix_pre, g_mix_post, g_ffn_pre, g_ffn_post, w_in, sgu_ln_g,
           sgu_w, sgu_b, attn_sink, na_rpb, w_out, ffn_w_gu, ffn_w_down, moe_router, moe_w_gu, moe_w_down):
    assert x.shape[0] == 1 and ctx.shape[0] == 1
    n_lat, d = x.shape[1], x.shape[2]
    n_ctx = ctx.shape[1]
    depth = w_mod.shape[0]
    t_all = n_lat + n_ctx

    c8 = jnp.zeros((8, d), F32).at[0].set(c[0]).at[1].set(c_ctx)
    mod = _mod_call(c8, w_mod, b_mod)
    cos, sin = _rope_tables(n_lat, n_ctx)
    w_in_bf16 = w_in.astype(BF16)
    w_out_bf16 = w_out.astype(BF16)
    nb = B_HEADS * HEAD_DIM

    xa = jnp.concatenate([x[0], ctx[0]], axis=0)
    for l in range(depth):
        last = l == depth - 1
        m2 = mod[l, 0:2]
        sh1, sc1, g1, sh2, sc2, g2 = (m2[:, k * d:(k + 1) * d] for k in range(6))

        p = _inproj_call(xa, g_mix_pre[l], sh1, sc1, w_in_bf16, l, cos, sin, tm=DENSE_TM, n_lat=n_lat)
        y_sgu = _sgu_call(p, sgu_ln_g[l], sgu_w[l], sgu_b[l], chunks=DENSE_TM // CHUNK)
        y_win = _win_call(p, attn_sink[l], n_lat=n_lat, n_ctx=n_ctx)
        y_na = _na_call(p, na_rpb[l], n_lat=n_lat, n_ctx=n_ctx)
        if not last:
            sink12 = jnp.concatenate([attn_sink[l], jnp.full((C_HEADS,), NEG_INF, F32)])
            y_ctx = _ctx_attn_call(p, sink12, n_lat=n_lat, n_ctx=n_ctx)
            y_win = jnp.concatenate([y_win, y_ctx[:, :nb]], axis=0)
            y_na = jnp.concatenate([y_na, y_ctx[:, nb:]], axis=0)
            xa = _outproj_call(y_sgu, y_win, y_na, w_out_bf16, l, xa, g_mix_post[l], g1,
                               rows=t_all, tm=DENSE_TM, n_lat=n_lat)
        else:
            xa = _outproj_call(y_sgu, y_win, y_na, w_out_bf16, l, xa, g_mix_post[l], g1,
                               rows=n_lat, tm=LAT_TM, n_lat=n_lat)

        if l % 2 == 0:
            xa = _ffn_dense_call(xa, g_ffn_pre[l], sh2, sc2, ffn_w_gu, ffn_w_down, l // 2,
                                 g_ffn_post[l], g2, tm=DENSE_TM, n_lat=n_lat)
        else:
            assert last, "expert layers are only supported as the last layer (latent rows only)"
            route, counts = _router_call(xa, g_ffn_pre[l], sh2, sc2, moe_router[l // 2])
            plan = _moe_plan(route, counts)
            xs = _dispatch_call(xa, plan["pos"], plan["pad_start"], plan["pad_len"], plan["tail"],
                                g_ffn_pre[l], sh2, sc2, plan["rows"])
            ys = _ffn_moe_call(xs, plan["tile_expert"], plan["tile_nsb"], plan["tile_row0"],
                               moe_w_gu[l // 2], moe_w_down[l // 2])
            xa = _combine_call(xa, ys, plan["pos"], route, g_ffn_post[l], g2)
    return xa[:n_lat][None]
```

```python
import functools

import numpy as np
import jax
import jax.numpy as jnp
from jax import lax
from jax.experimental import pallas as pl
from jax.experimental.pallas import tpu as pltpu

F32 = jnp.float32
BF16 = jnp.bfloat16

HEAD_DIM = 128
GRID_W = 64
CHUNK = 128
WINDOW_BLOCK = 128
B_HEADS = 6
B_KV_HEADS = 2
C_HEADS = 6
NA_ROWS = 8
NA_COLS = 16
N_EXPERTS = 8
ROPE_BASE = 10000.0
EPS = 1e-6
NEG_INF = -1e30

V7X_VMEM_BYTES = 64 * 1024 * 1024
VMEM_LIMIT = V7X_VMEM_BYTES - 6 * 1024 * 1024
LANES = 128

A_WIDTH = 512
QB_COL = 2 * A_WIDTH // HEAD_DIM
QC_COL = QB_COL + B_HEADS
KB_COL = QC_COL + C_HEADS
VB_COL = KB_COL + B_KV_HEADS
KC_COL = VB_COL + B_KV_HEADS
VC_COL = KC_COL + C_HEADS
IN_COLS = VC_COL + C_HEADS


def _params(sem, vmem=VMEM_LIMIT):
    return pltpu.CompilerParams(dimension_semantics=sem, vmem_limit_bytes=vmem)


def _rms(x):
    return x * lax.rsqrt(jnp.mean(x * x, axis=-1, keepdims=True) + EPS)


def _pick_rows(v2, row0, tm, n_lat):
    rows = row0 + lax.broadcasted_iota(jnp.int32, (tm, 1), 0)
    return jnp.where(rows >= n_lat, v2[1:2, :], v2[0:1, :])


def _norm_mod(x, g, sh2, sc2, row0, tm, n_lat):
    y = _rms(x) * g
    return y * (1.0 + _pick_rows(sc2, row0, tm, n_lat)) + _pick_rows(sh2, row0, tm, n_lat)


def _dot(a, b):
    return jnp.dot(a, b, preferred_element_type=F32)


def _dot_nt(a, b):
    return lax.dot_general(a, b, (((1,), (1,)), ((), ())), preferred_element_type=F32)


def _mod_kernel(c_ref, w_ref, b_ref, o_ref):
    a = jax.nn.silu(c_ref[...]).astype(BF16)
    o_ref[...] = _dot(a, w_ref[...].astype(BF16)) + b_ref[...]


def _mod_call(c8, w_mod, b_mod):
    depth, d, n = w_mod.shape
    tn = 1536
    return pl.pallas_call(
        _mod_kernel,
        grid=(depth, n // tn),
        in_specs=[
            pl.BlockSpec((8, d), lambda l, j: (0, 0)),
            pl.BlockSpec((None, d, tn), lambda l, j: (l, 0, j)),
            pl.BlockSpec((None, 1, tn), lambda l, j: (l, 0, j)),
        ],
        out_specs=pl.BlockSpec((None, 8, tn), lambda l, j: (l, 0, j)),
        out_shape=jax.ShapeDtypeStruct((depth, 8, n), F32),
        compiler_params=_params(("parallel", "parallel")),
        name="modulation",
    )(c8, w_mod, b_mod.reshape(depth, 1, n))


def _rope_head(x, cos, sin):
    lane = lax.broadcasted_iota(jnp.int32, x.shape, 1)
    first = (lane & 63) < 32
    partner = jnp.where(first, pltpu.roll(x, 96, 1), pltpu.roll(x, 32, 1))
    return x * cos + partner * sin


def _inproj_kernel(x_ref, g_ref, sh_ref, sc_ref, w_ref, cos_ref, sin_ref, o_ref, h_ref, *, tm, n_lat):
    i = pl.program_id(0)
    j = pl.program_id(1)

    @pl.when(j == 0)
    def _():
        h = _norm_mod(x_ref[...], g_ref[...], sh_ref[...], sc_ref[...], i * tm, tm, n_lat)
        h_ref[...] = h.astype(BF16)

    acc = _dot(h_ref[...], w_ref[...])

    def store(kinds):
        for hd, kind in enumerate(kinds):
            a = acc[:, hd * HEAD_DIM:(hd + 1) * HEAD_DIM]
            if kind == "gelu":
                a = jax.nn.gelu(a, approximate=True)
            elif kind == "rope":
                a = _rope_head(a, cos_ref[...], sin_ref[...])
            o_ref[:, hd * HEAD_DIM:(hd + 1) * HEAD_DIM] = a.astype(BF16)

    heads_per_tile = INPROJ_TN // HEAD_DIM
    for jt in range(IN_COLS // heads_per_tile):
        @pl.when(j == jt)
        def _(jt=jt):
            store(INPROJ_HEAD_KIND[jt * heads_per_tile:(jt + 1) * heads_per_tile])


INPROJ_HEAD_KIND = (("gelu",) * QB_COL + ("rope",) * B_HEADS + ("none",) * C_HEADS
                    + ("rope",) * B_KV_HEADS + ("none",) * (B_KV_HEADS + 2 * C_HEADS))
INPROJ_TN = 12 * HEAD_DIM


def _inproj_call(xa, g, sh2, sc2, w_all, layer, cos, sin, *, tm, n_lat):
    t, d = xa.shape
    n = w_all.shape[2]
    tn = INPROJ_TN
    assert t % tm == 0 and n == IN_COLS * HEAD_DIM and n % tn == 0 and w_all.dtype == BF16
    kern = functools.partial(_inproj_kernel, tm=tm, n_lat=n_lat)
    return pl.pallas_call(
        kern,
        grid=(t // tm, n // tn),
        in_specs=[
            pl.BlockSpec((tm, d), lambda i, j: (i, 0)),
            pl.BlockSpec((1, d), lambda i, j: (0, 0)),
            pl.BlockSpec((2, d), lambda i, j: (0, 0)),
            pl.BlockSpec((2, d), lambda i, j: (0, 0)),
            pl.BlockSpec((None, d, tn), lambda i, j: (layer, 0, j)),
            pl.BlockSpec((tm, HEAD_DIM), lambda i, j: (i, 0)),
            pl.BlockSpec((tm, HEAD_DIM), lambda i, j: (i, 0)),
        ],
        out_specs=pl.BlockSpec((tm, tn), lambda i, j: (i, j)),
        out_shape=jax.ShapeDtypeStruct((t, n), BF16),
        scratch_shapes=[pltpu.VMEM((tm, d), BF16)],
        compiler_params=_params(("parallel", "arbitrary")),
        name="in_projection",
    )(xa, g.reshape(1, d), sh2, sc2, w_all, cos, sin)


def _rope_tables(n_lat, n_ctx):
    t = np.arange(n_lat)
    n = HEAD_DIM // 4
    inv = (ROPE_BASE ** (-(2.0 / (HEAD_DIM // 2)) * np.arange(n, dtype=np.float32))).astype(np.float32)
    ang_r = (t // GRID_W).astype(np.float32)[:, None] * inv[None, :]
    ang_c = (t % GRID_W).astype(np.float32)[:, None] * inv[None, :]
    cos = np.concatenate([np.cos(ang_r)] * 2 + [np.cos(ang_c)] * 2, axis=1)
    sin = np.concatenate([-np.sin(ang_r), np.sin(ang_r), -np.sin(ang_c), np.sin(ang_c)], axis=1)
    cos = np.concatenate([cos, np.ones((n_ctx, HEAD_DIM), np.float32)], axis=0)
    sin = np.concatenate([sin, np.zeros((n_ctx, HEAD_DIM), np.float32)], axis=0)
    return jnp.asarray(cos, F32), jnp.asarray(sin, F32)


def _sgu_kernel(u_ref, v_ref, lng_ref, w_ref, b_ref, o_ref, *, chunks):
    for c in range(chunks):
        rows = slice(c * CHUNK, (c + 1) * CHUNK)
        for g in range(A_WIDTH // HEAD_DIM):
            cols = slice(g * HEAD_DIM, (g + 1) * HEAD_DIM)
            v = v_ref[rows, cols].astype(F32)
            mu = jnp.mean(v, axis=-1, keepdims=True)
            var = jnp.mean(jnp.square(v - mu), axis=-1, keepdims=True)
            vn = (v - mu) * lax.rsqrt(var + EPS) * lng_ref[:, cols]
            s = _dot(w_ref[g].astype(BF16), vn.astype(BF16)) + b_ref[g]
            o_ref[rows, cols] = (u_ref[rows, cols].astype(F32) * s).astype(BF16)


def _sgu_call(p, ln_g, w_s, b_s, *, chunks):
    t = p.shape[0]
    rows = chunks * CHUNK
    assert t % rows == 0
    groups = A_WIDTH // HEAD_DIM
    b_full = jnp.broadcast_to(b_s[:, :, None], (groups, CHUNK, HEAD_DIM))
    kern = functools.partial(_sgu_kernel, chunks=chunks)
    return pl.pallas_call(
        kern,
        grid=(t // rows,),
        in_specs=[
            pl.BlockSpec((rows, A_WIDTH), lambda i: (i, 0)),
            pl.BlockSpec((rows, A_WIDTH), lambda i: (i, 1)),
            pl.BlockSpec((1, A_WIDTH), lambda i: (0, 0)),
            pl.BlockSpec((groups, CHUNK, CHUNK), lambda i: (0, 0, 0)),
            pl.BlockSpec((groups, CHUNK, HEAD_DIM), lambda i: (0, 0, 0)),
        ],
        out_specs=pl.BlockSpec((rows, A_WIDTH), lambda i: (i, 0)),
        out_shape=jax.ShapeDtypeStruct((t, A_WIDTH), BF16),
        compiler_params=_params(("parallel",)),
        name="spatial_gating",
    )(p, p, ln_g.reshape(1, A_WIDTH), w_s, b_full)


WIN_Q_BLOCKS = 4


def _softmax_pv(s_loc, s_ctx, sink_col, v_loc, v_ctx):
    n_loc = s_loc.shape[1]
    s = jnp.concatenate([s_loc, s_ctx], axis=1)
    m = jnp.max(s, axis=1, keepdims=True)
    if sink_col is not None:
        m = jnp.maximum(m, sink_col)
    p = jnp.exp(s - m)
    den = jnp.sum(p, axis=1, keepdims=True)
    if sink_col is not None:
        den = den + jnp.exp(sink_col - m)
    p = p.astype(BF16)
    o = _dot(p[:, :n_loc], v_loc) + _dot(p[:, n_loc:], v_ctx)
    return o / den


def _win_kernel(sink_ref, q0, q1, q2, kp, km, kn, vp, vm, vn, kc, vc, o_ref, kcat, vcat, *, n_lat):
    kv = pl.program_id(0)
    s = pl.program_id(1)
    wb = WINDOW_BLOCK
    main = WIN_Q_BLOCKS * wb
    kcat[0:wb] = kp[...]
    kcat[wb:wb + main] = km[...]
    kcat[wb + main:2 * wb + main] = kn[...]
    vcat[0:wb] = vp[...]
    vcat[wb:wb + main] = vm[...]
    vcat[wb + main:2 * wb + main] = vn[...]
    scale = HEAD_DIM ** -0.5
    g = B_HEADS // B_KV_HEADS
    qs = (q0, q1, q2)
    row = lax.broadcasted_iota(jnp.int32, (g * wb, 3 * wb), 0) & (wb - 1)
    col = lax.broadcasted_iota(jnp.int32, (g * wb, 3 * wb), 1)
    rel = col - wb - row
    band_bias = jnp.where(rel < -wb, NEG_INF, jnp.where(rel > wb, NEG_INF, 0.0))
    col1 = lax.broadcasted_iota(jnp.int32, (1, 3 * wb), 1)
    sink_col = jnp.concatenate(
        [jnp.full((wb, 1), sink_ref[kv * g + gi], F32) for gi in range(g)], axis=0)
    n_blocks = n_lat // wb
    for b in range(WIN_Q_BLOCKS):
        n = s * WIN_Q_BLOCKS + b
        rows = slice(b * wb, (b + 1) * wb)
        q3 = jnp.concatenate([qs[gi][rows, :] for gi in range(g)], axis=0)
        keys = kcat[b * wb:(b + 3) * wb, :]
        vals = vcat[b * wb:(b + 3) * wb, :]
        off_start = jnp.where(n == 0, NEG_INF, 0.0)
        off_end = jnp.where(n == n_blocks - 1, NEG_INF, 0.0)
        edge = jnp.where(col1 < wb, off_start, jnp.where(col1 >= 2 * wb, off_end, 0.0))
        s_loc = _dot_nt(q3, keys) * scale + band_bias + edge
        s_ctx = _dot_nt(q3, kc[...]) * scale
        o = _softmax_pv(s_loc, s_ctx, sink_col, vals, vc[...])
        for gi in range(g):
            o_ref[rows, gi * HEAD_DIM:(gi + 1) * HEAD_DIM] = o[gi * wb:(gi + 1) * wb, :].astype(BF16)


def _win_call(p, sink, *, n_lat, n_ctx):
    wb = WINDOW_BLOCK
    main = WIN_Q_BLOCKS * wb
    assert n_lat % main == 0 and n_lat % n_ctx == 0
    nsb = n_lat // main
    nb = n_lat // wb
    g = B_HEADS // B_KV_HEADS
    ctx_blk = n_lat // n_ctx

    def qspec(gi):
        return pl.BlockSpec((main, HEAD_DIM), lambda kv, s: (s, QB_COL + kv * g + gi))

    def band_specs(col0):
        return [
            pl.BlockSpec((wb, HEAD_DIM), lambda kv, s: (jnp.maximum(s * WIN_Q_BLOCKS - 1, 0), col0 + kv)),
            pl.BlockSpec((main, HEAD_DIM), lambda kv, s: (s, col0 + kv)),
            pl.BlockSpec((wb, HEAD_DIM), lambda kv, s: (jnp.minimum((s + 1) * WIN_Q_BLOCKS, nb - 1), col0 + kv)),
        ]

    def ctx_spec(col0):
        return pl.BlockSpec((n_ctx, HEAD_DIM), lambda kv, s: (ctx_blk, col0 + kv))

    kern = functools.partial(_win_kernel, n_lat=n_lat)
    return pl.pallas_call(
        kern,
        grid=(B_KV_HEADS, nsb),
        in_specs=[pl.BlockSpec(memory_space=pltpu.SMEM), qspec(0), qspec(1), qspec(2)]
        + band_specs(KB_COL) + band_specs(VB_COL) + [ctx_spec(KB_COL), ctx_spec(VB_COL)],
        out_specs=pl.BlockSpec((main, g * HEAD_DIM), lambda kv, s: (s, kv)),
        out_shape=jax.ShapeDtypeStruct((n_lat, B_HEADS * HEAD_DIM), BF16),
        scratch_shapes=[pltpu.VMEM((main + 2 * wb, HEAD_DIM), BF16)] * 2,
        compiler_params=_params(("parallel", "parallel")),
        name="window_attention",
    )(sink, p, p, p, p, p, p, p, p, p, p, p)


NA_Q_ROWS = 8
NA_K_ROWS = 16


def _na_row_windows(group, n_rows):
    lead = (NA_K_ROWS - NA_Q_ROWS) // 2
    rq = group * NA_Q_ROWS + np.arange(NA_Q_ROWS)
    rk = group * NA_Q_ROWS - lead + np.arange(NA_K_ROWS)
    r0 = np.clip(rq - NA_ROWS // 2, 0, n_rows - NA_ROWS)
    valid = (rk[None, :] >= r0[:, None]) & (rk[None, :] < r0[:, None] + NA_ROWS)
    roff = rk[None, :] - rq[:, None] + (NA_ROWS - 1)
    return valid, roff


def _na_fill_bias(bt_ref, mb_ref, group, n_rows):
    valid, roff = _na_row_windows(group, n_rows)
    w = GRID_W
    left = lax.broadcasted_iota(jnp.int32, (w, 2 * w), 1) < w
    neg = jnp.full((w, 2 * w), NEG_INF, F32)
    for i in range(NA_Q_ROWS):
        for u in range(0, NA_K_ROWS, 2):
            lo = bt_ref[int(roff[i, u])] if valid[i, u] else neg
            hi = bt_ref[int(roff[i, u + 1])] if valid[i, u + 1] else neg
            blk = neg if not (valid[i, u] or valid[i, u + 1]) else jnp.where(left, lo, hi)
            mb_ref[i * w:(i + 1) * w, u * w:(u + 2) * w] = blk


def _na_kernel(q, kp, km, kn, vp, vm, vn, kc, vc, bt, o, kcat, vcat, mb, *, n_rows):
    a = pl.program_id(1)
    ng = n_rows // NA_Q_ROWS
    for group in sorted({0, min(1, ng - 1), ng - 1}):
        @pl.when(a == group)
        def _(group=group):
            _na_fill_bias(bt, mb, group, n_rows)

    half = (NA_K_ROWS - NA_Q_ROWS) // 2 * GRID_W
    main = NA_Q_ROWS * GRID_W
    kcat[0:half] = kp[...]
    kcat[half:half + main] = km[...]
    kcat[half + main:2 * half + main] = kn[...]
    vcat[0:half] = vp[...]
    vcat[half:half + main] = vm[...]
    vcat[half + main:2 * half + main] = vn[...]
    scale = HEAD_DIM ** -0.5
    qc = NA_Q_ROWS // 2
    for c in range(2):
        rows = slice(c * qc * GRID_W, (c + 1) * qc * GRID_W)
        keys = slice(c * qc * GRID_W, (c * qc + qc + NA_ROWS) * GRID_W)
        qv = q[rows, :]
        s_loc = _dot_nt(qv, kcat[keys, :]) * scale + mb[rows, keys]
        s_ctx = _dot_nt(qv, kc[...]) * scale
        o[rows, :] = _softmax_pv(s_loc, s_ctx, None, vcat[keys, :], vc[...]).astype(BF16)


def _na_col_table(rpb):
    w = GRID_W
    cq = np.arange(w)
    c0 = np.clip(cq - NA_COLS // 2, 0, w - NA_COLS)
    cmask = (cq[None, :] >= c0[:, None]) & (cq[None, :] < c0[:, None] + NA_COLS)
    coff = np.clip(cq[None, :] - cq[:, None] + (NA_COLS - 1), 0, 2 * NA_COLS - 2)
    onehot = (coff[None] == np.arange(2 * NA_COLS - 1)[:, None, None]).astype(np.float32)
    picked = jnp.einsum("hrj,jqk->hrqk", rpb.astype(F32), onehot, precision=lax.Precision.HIGHEST)
    by_col = jnp.where(cmask[None, None], picked, NEG_INF)
    return jnp.concatenate([by_col, by_col], axis=-1)


def _na_call(p, rpb, *, n_lat, n_ctx):
    main = NA_Q_ROWS * GRID_W
    half = (NA_K_ROWS - NA_Q_ROWS) // 2 * GRID_W
    assert n_lat % main == 0 and n_lat % n_ctx == 0 and main == 2 * half
    ng = n_lat // main
    assert ng >= 2
    nhalf = n_lat // half
    ctx_blk = n_lat // n_ctx
    bt = _na_col_table(rpb)
    kern = functools.partial(_na_kernel, n_rows=n_lat // GRID_W)

    def band_specs(col0):
        return [
            pl.BlockSpec((half, HEAD_DIM), lambda h, a: (jnp.maximum(2 * a - 1, 0), col0 + h)),
            pl.BlockSpec((main, HEAD_DIM), lambda h, a: (a, col0 + h)),
            pl.BlockSpec((half, HEAD_DIM), lambda h, a: (jnp.minimum(2 * a + 2, nhalf - 1), col0 + h)),
        ]

    return pl.pallas_call(
        kern,
        grid=(C_HEADS, ng),
        in_specs=[pl.BlockSpec((main, HEAD_DIM), lambda h, a: (a, QC_COL + h))]
        + band_specs(KC_COL) + band_specs(VC_COL)
        + [pl.BlockSpec((n_ctx, HEAD_DIM), lambda h, a: (ctx_blk, KC_COL + h)),
           pl.BlockSpec((n_ctx, HEAD_DIM), lambda h, a: (ctx_blk, VC_COL + h)),
           pl.BlockSpec((None,) + bt.shape[1:], lambda h, a: (h, 0, 0, 0))],
        out_specs=pl.BlockSpec((main, HEAD_DIM), lambda h, a: (a, h)),
        out_shape=jax.ShapeDtypeStruct((n_lat, C_HEADS * HEAD_DIM), BF16),
        scratch_shapes=[pltpu.VMEM((main + 2 * half, HEAD_DIM), BF16)] * 2
        + [pltpu.VMEM((main, NA_K_ROWS * GRID_W), F32)],
        compiler_params=_params(("arbitrary", "arbitrary")),
        name="neighbourhood_attention",
    )(p, p, p, p, p, p, p, p, p, bt)


def _ctx_attn_kernel(sink_ref, q, k, v, o):
    hh = pl.program_id(0)
    scale = HEAD_DIM ** -0.5
    s = _dot_nt(q[...], k[...]) * scale
    sink = jnp.full((s.shape[0], 1), sink_ref[hh], F32)
    m = jnp.maximum(jnp.max(s, axis=1, keepdims=True), sink)
    pr = jnp.exp(s - m)
    den = jnp.sum(pr, axis=1, keepdims=True) + jnp.exp(sink - m)
    o[...] = (_dot(pr.astype(BF16), v[...]) / den).astype(BF16)


def _ctx_attn_call(p, sink12, *, n_lat, n_ctx):
    blk = n_lat // n_ctx
    g = B_HEADS // B_KV_HEADS

    def kcol(hh):
        return jnp.where(hh < B_HEADS, KB_COL + hh // g, KC_COL + hh - B_HEADS)

    def vcol(hh):
        return jnp.where(hh < B_HEADS, VB_COL + hh // g, VC_COL + hh - B_HEADS)

    return pl.pallas_call(
        _ctx_attn_kernel,
        grid=(B_HEADS + C_HEADS,),
        in_specs=[
            pl.BlockSpec(memory_space=pltpu.SMEM),
            pl.BlockSpec((n_ctx, HEAD_DIM), lambda hh: (blk, QB_COL + hh)),
            pl.BlockSpec((n_ctx, HEAD_DIM), lambda hh: (blk, kcol(hh))),
            pl.BlockSpec((n_ctx, HEAD_DIM), lambda hh: (blk, vcol(hh))),
        ],
        out_specs=pl.BlockSpec((n_ctx, HEAD_DIM), lambda hh: (0, hh)),
        out_shape=jax.ShapeDtypeStruct((n_ctx, (B_HEADS + C_HEADS) * HEAD_DIM), BF16),
        compiler_params=_params(("parallel",)),
        name="context_attention",
    )(sink12, p, p, p)


def _outproj_kernel(ya_ref, yb_ref, yc_ref, w_ref, x_ref, g_ref, gate_ref, o_ref, *, tm, n_lat):
    i = pl.program_id(0)
    ka = ya_ref.shape[1]
    kb = yb_ref.shape[1]
    acc = _dot(ya_ref[...], w_ref[0:ka, :])
    acc += _dot(yb_ref[...], w_ref[ka:ka + kb, :])
    acc += _dot(yc_ref[...], w_ref[ka + kb:, :])
    r = _rms(acc) * g_ref[...]
    o_ref[...] = x_ref[...] + _pick_rows(gate_ref[...], i * tm, tm, n_lat) * r


def _outproj_call(ya, yb, yc, w_all_bf16, layer, xa, g, gate2, *, rows, tm, n_lat):
    d = xa.shape[1]
    assert rows % tm == 0 and ya.shape[1] + yb.shape[1] + yc.shape[1] == d
    kern = functools.partial(_outproj_kernel, tm=tm, n_lat=n_lat)
    return pl.pallas_call(
        kern,
        grid=(rows // tm,),
        in_specs=[
            pl.BlockSpec((tm, ya.shape[1]), lambda i: (i, 0)),
            pl.BlockSpec((tm, yb.shape[1]), lambda i: (i, 0)),
            pl.BlockSpec((tm, yc.shape[1]), lambda i: (i, 0)),
            pl.BlockSpec((None, d, d), lambda i: (layer, 0, 0)),
            pl.BlockSpec((tm, d), lambda i: (i, 0)),
            pl.BlockSpec((1, d), lambda i: (0, 0)),
            pl.BlockSpec((2, d), lambda i: (0, 0)),
        ],
        out_specs=pl.BlockSpec((tm, d), lambda i: (i, 0)),
        out_shape=jax.ShapeDtypeStruct((rows, d), F32),
        compiler_params=_params(("parallel",)),
        name="out_projection",
    )(ya, yb, yc, w_all_bf16, xa, g.reshape(1, d), gate2)


FFN_TF = 256


def _swiglu_accumulate(h_ref, wg_ref, wu_ref, wd_ref, o_ref):
    h = h_ref[...]
    gt = _dot(h, wg_ref[...].astype(BF16))
    up = _dot(h, wu_ref[...].astype(BF16))
    act = (jax.nn.silu(gt) * up).astype(BF16)
    o_ref[...] += _dot(act, wd_ref[...].astype(BF16))


def _ffn_weight_specs(d, tf, nf, expert_of, chunk_of=lambda m, f, *pf: f):
    return [
        pl.BlockSpec((None, d, tf), lambda m, f, *pf: (expert_of(m, *pf), 0, chunk_of(m, f, *pf))),
        pl.BlockSpec((None, d, tf), lambda m, f, *pf: (expert_of(m, *pf), 0, nf + chunk_of(m, f, *pf))),
        pl.BlockSpec((None, tf, d), lambda m, f, *pf: (expert_of(m, *pf), chunk_of(m, f, *pf), 0)),
    ]


def _ffn_dense_kernel(x_ref, gpre_ref, sh_ref, sc_ref, wg_ref, wu_ref, wd_ref, gpost_ref, gate_ref,
                      o_ref, h_ref, *, tm, n_lat):
    m = pl.program_id(0)
    f = pl.program_id(1)

    @pl.when(f == 0)
    def _():
        h = _norm_mod(x_ref[...], gpre_ref[...], sh_ref[...], sc_ref[...], m * tm, tm, n_lat)
        h_ref[...] = h.astype(BF16)
        o_ref[...] = jnp.zeros_like(o_ref)

    _swiglu_accumulate(h_ref, wg_ref, wu_ref, wd_ref, o_ref)

    @pl.when(f == pl.num_programs(1) - 1)
    def _():
        r = _rms(o_ref[...]) * gpost_ref[...]
        o_ref[...] = x_ref[...] + _pick_rows(gate_ref[...], m * tm, tm, n_lat) * r


def _ffn_dense_call(xa, gpre, sh2, sc2, w_gu, w_down, layer_set, gpost, gate2, *, tm, n_lat):
    t, d = xa.shape
    ffn = w_gu.shape[2] // 2
    tf = FFN_TF
    nf = ffn // tf
    assert t % tm == 0 and ffn % tf == 0
    kern = functools.partial(_ffn_dense_kernel, tm=tm, n_lat=n_lat)
    vec = lambda rows: pl.BlockSpec((rows, d), lambda m, f: (0, 0))
    return pl.pallas_call(
        kern,
        grid=(t // tm, nf),
        in_specs=[pl.BlockSpec((tm, d), lambda m, f: (m, 0), pipeline_mode=pl.Buffered(1)),
                  vec(1), vec(2), vec(2)]
        + _ffn_weight_specs(d, tf, nf, lambda m: layer_set) + [vec(1), vec(2)],
        out_specs=pl.BlockSpec((tm, d), lambda m, f: (m, 0)),
        out_shape=jax.ShapeDtypeStruct((t, d), F32),
        scratch_shapes=[pltpu.VMEM((tm, d), BF16)],
        compiler_params=_params(("parallel", "arbitrary")),
        name="swiglu_ffn",
    )(xa, gpre.reshape(1, d), sh2, sc2, w_gu, w_gu, w_down, gpost.reshape(1, d), gate2)


MOE_SB = 512
MOE_NSB = 4


DISPATCH_TM = 512


def _dispatch_kernel(pos_ref, pad0_ref, padn_ref, tail_ref, x_ref, gpre_ref, sh_ref, sc_ref, xs_hbm, hbuf, sems,
                     *, tm, sb):
    i = pl.program_id(0)
    last = pl.num_programs(0) - 1
    h = _rms(x_ref[...]) * gpre_ref[...]
    hbuf[i % 2] = h * (1.0 + sc_ref[0:1, :]) + sh_ref[0:1, :]

    def copy(step, t, k):
        p = pos_ref[(step * tm + t) * 2 + k]
        return pltpu.make_async_copy(hbuf.at[step % 2, pl.ds(t, 1)], xs_hbm.at[pl.ds(p, 1)], sems.at[step % 2])

    def issue(t, carry):
        copy(i, t, 0).start()
        copy(i, t, 1).start()
        return carry

    lax.fori_loop(0, tm, issue, 0, unroll=8)

    def drain_step(step):
        def drain(t, carry):
            copy(step, t, 0).wait()
            copy(step, t, 1).wait()
            return carry

        lax.fori_loop(0, tm, drain, 0, unroll=8)

    @pl.when(i > 0)
    def _():
        drain_step(i - 1)

    @pl.when(i == last)
    def _():
        drain_step(i)
        hbuf[0] = jnp.zeros(hbuf.shape[1:], F32)
        def zero_row(e, r):
            return pltpu.make_async_copy(hbuf.at[0, pl.ds(0, 1)], xs_hbm.at[pl.ds(pad0_ref[e] + r, 1)], sems.at[0])

        for e in range(N_EXPERTS):
            lax.fori_loop(0, padn_ref[e], lambda r, c, e=e: (zero_row(e, r).start(), c)[1], 0)
        for e in range(N_EXPERTS):
            lax.fori_loop(0, padn_ref[e], lambda r, c, e=e: (zero_row(e, r).wait(), c)[1], 0)

        def zero_block(g, carry):
            row = pl.multiple_of(tail_ref[0] + g * sb, sb)
            cp = pltpu.make_async_copy(hbuf.at[0, pl.ds(0, sb)], xs_hbm.at[pl.ds(row, sb)], sems.at[0])
            cp.start()
            cp.wait()
            return carry

        lax.fori_loop(0, (xs_hbm.shape[0] - tail_ref[0]) // sb, zero_block, 0)


def _dispatch_call(x, pos, pad0, padn, tail, gpre, sh2, sc2, rows_out):
    t, d = x.shape
    tm = DISPATCH_TM
    assert t % tm == 0 and tm >= MOE_SB
    kern = functools.partial(_dispatch_kernel, tm=tm, sb=MOE_SB)
    vec = lambda r: pl.BlockSpec((r, d), lambda i, *pf: (0, 0))
    grid_spec = pltpu.PrefetchScalarGridSpec(
        num_scalar_prefetch=4,
        grid=(t // tm,),
        in_specs=[pl.BlockSpec((tm, d), lambda i, *pf: (i, 0)), vec(1), vec(2), vec(2)],
        out_specs=pl.BlockSpec(memory_space=pl.ANY),
        scratch_shapes=[pltpu.VMEM((2, tm, d), F32), pltpu.SemaphoreType.DMA((2,))],
    )
    return pl.pallas_call(
        kern,
        grid_spec=grid_spec,
        out_shape=jax.ShapeDtypeStruct((rows_out, d), F32),
        compiler_params=_params(("arbitrary",)),
        name="dispatch",
    )(pos, pad0, padn, tail, x, gpre.reshape(1, d), sh2, sc2)


def _ffn_moe_kernel(te_ref, ns_ref, r0_ref, xs_hbm, wg_ref, wu_ref, wd_ref, o_hbm,
                    stage, h_ref, acc, wgb, wub, wdb, gsem, osem, *, sb, nsb):
    m = pl.program_id(0)
    f = pl.program_id(1)
    n = ns_ref[m]
    n_tiles = pl.num_programs(0)
    cur = m % 2

    def out_copy(mm, s):
        row = pl.multiple_of(r0_ref[mm] + s * sb, sb)
        return pltpu.make_async_copy(acc.at[pl.ds(s * sb, sb)], o_hbm.at[pl.ds(row, sb)], osem)

    def in_copy(mm, s):
        row = pl.multiple_of(r0_ref[mm] + s * sb, sb)
        return pltpu.make_async_copy(xs_hbm.at[pl.ds(row, sb)], stage, gsem)

    def for_valid(count, body):
        for s in range(nsb):
            @pl.when(s < count)
            def _(s=s):
                body(s)

    @pl.when(f == 0)
    def _():
        @pl.when(m == 0)
        def _():
            def load(s):
                in_copy(m, s).start()
                in_copy(m, s).wait()
                h_ref[0, s * sb:(s + 1) * sb, :] = stage[...].astype(BF16)

            for_valid(n, load)

        @pl.when(m > 0)
        def _():
            for_valid(ns_ref[jnp.maximum(m - 1, 0)], lambda s: out_copy(m - 1, s).wait())

        def clear(s):
            acc[s * sb:(s + 1) * sb, :] = jnp.zeros((sb, acc.shape[1]), F32)

        for_valid(n, clear)

    nxt = jnp.minimum(m + 1, n_tiles - 1)
    n_next = jnp.where(m + 1 < n_tiles, ns_ref[nxt], 0)
    s_next = f // 2

    @pl.when((f % 2 == 0) & (s_next < n_next))
    def _():
        in_copy(nxt, s_next).start()

    @pl.when((f % 2 == 1) & (s_next < n_next))
    def _():
        in_copy(nxt, s_next).wait()
        h_ref[1 - cur, pl.ds(pl.multiple_of(s_next * sb, sb), sb), :] = stage[...].astype(BF16)

    def accumulate(s):
        if s == 0:
            wg = wg_ref[...].astype(BF16)
            wu = wu_ref[...].astype(BF16)
            wd = wd_ref[...].astype(BF16)
            wgb[...] = wg
            wub[...] = wu
            wdb[...] = wd
        else:
            wg, wu, wd = wgb[...], wub[...], wdb[...]
        h = h_ref[cur, s * sb:(s + 1) * sb, :]
        act = (jax.nn.silu(_dot(h, wg)) * _dot(h, wu)).astype(BF16)
        acc[s * sb:(s + 1) * sb, :] += _dot(act, wd)

    for_valid(n, accumulate)

    @pl.when(f == pl.num_programs(1) - 1)
    def _():
        for_valid(n, lambda s: out_copy(m, s).start())

        @pl.when(m == n_tiles - 1)
        def _():
            for_valid(n, lambda s: out_copy(m, s).wait())
            stage[...] = jnp.zeros(stage.shape, F32)

            def fill(g, carry):
                row = pl.multiple_of(r0_ref[n_tiles] + g * sb, sb)
                cp = pltpu.make_async_copy(stage, o_hbm.at[pl.ds(row, sb)], osem)
                cp.start()
                cp.wait()
                return carry

            lax.fori_loop(0, (o_hbm.shape[0] - r0_ref[n_tiles]) // sb, fill, 0)


def _ffn_moe_call(xs, tile_expert, tile_nsb, tile_row0, w_gu, w_down):
    rows, d = xs.shape
    sb, nsb = MOE_SB, MOE_NSB
    n_tiles = tile_expert.shape[0]
    ffn = w_gu.shape[2] // 2
    tf = FFN_TF
    nf = ffn // tf
    assert rows % sb == 0 and tile_row0.shape[0] == n_tiles + 1 and ffn % tf == 0 and nf >= 2 * nsb
    kern = functools.partial(_ffn_moe_kernel, sb=sb, nsb=nsb)
    grid_spec = pltpu.PrefetchScalarGridSpec(
        num_scalar_prefetch=3,
        grid=(n_tiles, nf),
        in_specs=[pl.BlockSpec(memory_space=pl.ANY)]
        + _ffn_weight_specs(d, tf, nf, lambda m, te, *pf: te[m],
                            lambda m, f, te, ns, *pf: jnp.where(ns[m] > 0, f, nf - 1)),
        out_specs=pl.BlockSpec(memory_space=pl.ANY),
        scratch_shapes=[pltpu.VMEM((sb, d), F32), pltpu.VMEM((2, sb * nsb, d), BF16),
                        pltpu.VMEM((sb * nsb, d), F32),
                        pltpu.VMEM((d, tf), BF16), pltpu.VMEM((d, tf), BF16), pltpu.VMEM((tf, d), BF16),
                        pltpu.SemaphoreType.DMA(()), pltpu.SemaphoreType.DMA(())],
    )
    return pl.pallas_call(
        kern,
        grid_spec=grid_spec,
        out_shape=jax.ShapeDtypeStruct((rows, d), F32),
        compiler_params=_params(("arbitrary", "arbitrary")),
        name="expert_ffn",
    )(tile_expert, tile_nsb, tile_row0, xs, w_gu, w_gu, w_down)


ROUTER_TM = 512
R_E1, R_E2, R_W1, R_W2, R_RANK1, R_RANK2 = range(6)


def _router_kernel(x_ref, gpre_ref, sh_ref, sc_ref, wr_ref, route_ref, count_ref, carry_ref, *, tm):
    i = pl.program_id(0)

    @pl.when(i == 0)
    def _():
        carry_ref[...] = jnp.zeros_like(carry_ref)

    h = _rms(x_ref[...]) * gpre_ref[...]
    h = h * (1.0 + sc_ref[0:1, :]) + sh_ref[0:1, :]
    w = wr_ref[...]
    h_hi = h.astype(BF16)
    h_lo = (h - h_hi.astype(F32)).astype(BF16)
    w_hi = w.astype(BF16)
    w_lo = (w - w_hi.astype(F32)).astype(BF16)
    logits = _dot(h_hi, w_hi) + (_dot(h_hi, w_lo) + _dot(h_lo, w_hi))
    lane_i = lax.broadcasted_iota(jnp.int32, logits.shape, 1)
    lane = lane_i.astype(F32)
    logits = jnp.where(lane_i < N_EXPERTS, logits, -jnp.inf)
    v1 = jnp.max(logits, axis=1, keepdims=True)
    e1 = jnp.min(jnp.where(logits == v1, lane, float(LANES)), axis=1, keepdims=True)
    rest = jnp.where(lane == e1, -jnp.inf, logits)
    v2 = jnp.max(rest, axis=1, keepdims=True)
    e2 = jnp.min(jnp.where(rest == v2, lane, float(LANES)), axis=1, keepdims=True)
    ex = jnp.exp(v2 - v1)
    w1 = 1.0 / (1.0 + ex)
    w2 = ex / (1.0 + ex)
    hit1 = lane == e1
    hit2 = lane == e2
    assign = jnp.where(hit1, 1.0, jnp.where(hit2, 1.0, 0.0))
    r = lax.broadcasted_iota(jnp.int32, (tm, tm), 0)
    c = lax.broadcasted_iota(jnp.int32, (tm, tm), 1)
    before = jnp.where(c < r, 1.0, 0.0).astype(BF16)
    prefix = _dot(before, assign.astype(BF16)) + carry_ref[0:1, :]
    rank1 = jnp.sum(jnp.where(hit1, prefix, 0.0), axis=1, keepdims=True)
    rank2 = jnp.sum(jnp.where(hit2, prefix, 0.0), axis=1, keepdims=True)
    total = carry_ref[0:1, :] + jnp.sum(assign, axis=0, keepdims=True)
    carry_ref[...] = jnp.broadcast_to(total, carry_ref.shape)
    count_ref[...] = jnp.broadcast_to(total, count_ref.shape)
    rec = jnp.zeros(logits.shape, F32)
    for k, val in ((R_E1, e1), (R_E2, e2), (R_W1, w1), (R_W2, w2), (R_RANK1, rank1), (R_RANK2, rank2)):
        rec = jnp.where(lane_i == k, val, rec)
    route_ref[...] = rec


def _router_call(x, gpre, sh2, sc2, w_router):
    t, d = x.shape
    tm = ROUTER_TM
    assert t % tm == 0
    wr = jnp.zeros((d, LANES), F32).at[:, :N_EXPERTS].set(w_router)
    kern = functools.partial(_router_kernel, tm=tm)
    return pl.pallas_call(
        kern,
        grid=(t // tm,),
        in_specs=[
            pl.BlockSpec((tm, d), lambda i: (i, 0)),
            pl.BlockSpec((1, d), lambda i: (0, 0)),
            pl.BlockSpec((2, d), lambda i: (0, 0)),
            pl.BlockSpec((2, d), lambda i: (0, 0)),
            pl.BlockSpec((d, LANES), lambda i: (0, 0)),
        ],
        out_specs=[pl.BlockSpec((tm, LANES), lambda i: (i, 0)),
                   pl.BlockSpec((8, LANES), lambda i: (0, 0))],
        out_shape=[jax.ShapeDtypeStruct((t, LANES), F32), jax.ShapeDtypeStruct((8, LANES), F32)],
        scratch_shapes=[pltpu.VMEM((8, LANES), F32)],
        compiler_params=_params(("arbitrary",)),
        name="router_top2",
    )(x, gpre.reshape(1, d), sh2, sc2, wr)


COMBINE_TM = 256


def _combine_kernel(pos_ref, x_ref, route_ref, gpost_ref, gate_ref, ys_hbm, o_ref, buf, sem, *, tm):
    i = pl.program_id(0)
    slot = i % 2

    def copy(step, t, k):
        p = pos_ref[(step * tm + t) * 2 + k]
        return pltpu.make_async_copy(ys_hbm.at[pl.ds(p, 1)], buf.at[step % 2, k, pl.ds(t, 1)], sem.at[step % 2])

    def fetch(step):
        def issue(t, carry):
            copy(step, t, 0).start()
            copy(step, t, 1).start()
            return carry

        lax.fori_loop(0, tm, issue, 0, unroll=8)

    @pl.when(i == 0)
    def _():
        fetch(i)

    @pl.when(i + 1 < pl.num_programs(0))
    def _():
        fetch(i + 1)

    def drain(t, carry):
        copy(i, t, 0).wait()
        copy(i, t, 1).wait()
        return carry

    lax.fori_loop(0, tm, drain, 0, unroll=8)
    rec = route_ref[...]
    y = rec[:, R_W1:R_W1 + 1] * buf[slot, 0] + rec[:, R_W2:R_W2 + 1] * buf[slot, 1]
    o_ref[...] = x_ref[...] + gate_ref[0:1, :] * (_rms(y) * gpost_ref[...])


def _combine_call(x, ys, pos_flat, route, gpost, gate2):
    t, d = x.shape
    tm = COMBINE_TM
    assert t % tm == 0
    kern = functools.partial(_combine_kernel, tm=tm)
    grid_spec = pltpu.PrefetchScalarGridSpec(
        num_scalar_prefetch=1,
        grid=(t // tm,),
        in_specs=[
            pl.BlockSpec((tm, d), lambda i, pos: (i, 0)),
            pl.BlockSpec((tm, LANES), lambda i, pos: (i, 0)),
            pl.BlockSpec((1, d), lambda i, pos: (0, 0)),
            pl.BlockSpec((2, d), lambda i, pos: (0, 0)),
            pl.BlockSpec(memory_space=pl.ANY),
        ],
        out_specs=pl.BlockSpec((tm, d), lambda i, pos: (i, 0)),
        scratch_shapes=[pltpu.VMEM((2, 2, tm, d), F32), pltpu.SemaphoreType.DMA((2,))],
    )
    return pl.pallas_call(
        kern,
        grid_spec=grid_spec,
        out_shape=jax.ShapeDtypeStruct((t, d), F32),
        compiler_params=_params(("arbitrary",)),
        name="combine",
    )(pos_flat, x, route, gpost.reshape(1, d), gate2, ys)


DENSE_TM = 768
LAT_TM = 1024


def _moe_plan(route, counts):
    t = route.shape[0]
    e = N_EXPERTS
    sb, nsb = MOE_SB, MOE_NSB
    i32 = jnp.int32
    cnt = counts[0, :e].astype(i32)
    nsub = (cnt + sb - 1) // sb
    row_end = jnp.cumsum(nsub) * sb
    row_start = row_end - nsub * sb
    ntile = (nsub + nsb - 1) // nsb
    tile_end = jnp.cumsum(ntile)
    tile_start = tile_end - ntile
    ex = route[:, R_E1:R_E2 + 1].astype(i32)
    rank = route[:, R_RANK1:R_RANK2 + 1].astype(i32)
    pos = (row_start[ex] + rank).reshape(-1)
    max_sub = (2 * t + e * (sb - 1)) // sb
    max_tiles = (max_sub + e * (nsb - 1)) // nsb
    m = jnp.arange(max_tiles + 1, dtype=i32)
    used = m < tile_end[-1]
    m_used = jnp.minimum(m, tile_end[-1] - 1)
    te = jnp.minimum(jnp.sum((tile_end[None, :] <= m_used[:, None]).astype(i32), axis=1), e - 1)
    local = m_used - tile_start[te]
    per = nsub[te] // jnp.maximum(ntile[te], 1)
    extra = nsub[te] - per * ntile[te]
    tile_nsb = jnp.where(used, per + (local < extra).astype(i32), 0)
    first_sub = local * per + jnp.minimum(local, extra)
    tile_row0 = jnp.where(used, row_start[te] + first_sub * sb, row_end[-1])
    return dict(pos=pos, tile_expert=te[:-1].astype(i32), tile_nsb=tile_nsb[:-1].astype(i32),
                tile_row0=tile_row0.astype(i32), pad_start=(row_start + cnt).astype(i32),
                pad_len=(nsub * sb - cnt).astype(i32), tail=row_end[-1:].astype(i32), rows=max_sub * sb)


def kernel(x, c, ctx, c_ctx, w_mod, b_mod, g_mix_pre, g_mix_post, g_ffn_pre, g_ffn_post, w_in, sgu_ln_g,
           sgu_w, sgu_b, attn_sink, na_rpb, w_out, ffn_w_gu, ffn_w_down, moe_router, moe_w_gu, moe_w_down):
    assert x.shape[0] == 1 and ctx.shape[0] == 1
    n_lat, d = x.shape[1], x.shape[2]
    n_ctx = ctx.shape[1]
    depth = w_mod.shape[0]
    t_all = n_lat + n_ctx

    c8 = jnp.zeros((8, d), F32).at[0].set(c[0]).at[1].set(c_ctx)
    mod = _mod_call(c8, w_mod, b_mod)
    cos, sin = _rope_tables(n_lat, n_ctx)
    w_in_bf16 = w_in.astype(BF16)
    w_out_bf16 = w_out.astype(BF16)
    nb = B_HEADS * HEAD_DIM

    xa = jnp.concatenate([x[0], ctx[0]], axis=0)
    for l in range(depth):
        last = l == depth - 1
        m2 = mod[l, 0:2]
        sh1, sc1, g1, sh2, sc2, g2 = (m2[:, k * d:(k + 1) * d] for k in range(6))

        p = _inproj_call(xa, g_mix_pre[l], sh1, sc1, w_in_bf16, l, cos, sin, tm=DENSE_TM, n_lat=n_lat)
        y_sgu = _sgu_call(p, sgu_ln_g[l], sgu_w[l], sgu_b[l], chunks=DENSE_TM // CHUNK)
        y_win = _win_call(p, attn_sink[l], n_lat=n_lat, n_ctx=n_ctx)
        y_na = _na_call(p, na_rpb[l], n_lat=n_lat, n_ctx=n_ctx)
        if not last:
            sink12 = jnp.concatenate([attn_sink[l], jnp.full((C_HEADS,), NEG_INF, F32)])
            y_ctx = _ctx_attn_call(p, sink12, n_lat=n_lat, n_ctx=n_ctx)
            y_win = jnp.concatenate([y_win, y_ctx[:, :nb]], axis=0)
            y_na = jnp.concatenate([y_na, y_ctx[:, nb:]], axis=0)
            xa = _outproj_call(y_sgu, y_win, y_na, w_out_bf16, l, xa, g_mix_post[l], g1,
                               rows=t_all, tm=DENSE_TM, n_lat=n_lat)
        else:
            xa = _outproj_call(y_sgu, y_win, y_na, w_out_bf16, l, xa, g_mix_post[l], g1,
                               rows=n_lat, tm=LAT_TM, n_lat=n_lat)

        if l % 2 == 0:
            xa = _ffn_dense_call(xa, g_ffn_pre[l], sh2, sc2, ffn_w_gu, ffn_w_down, l // 2,
                                 g_ffn_post[l], g2, tm=DENSE_TM, n_lat=n_lat)
        else:
            assert last, "expert layers are only supported as the last layer (latent rows only)"
            route, counts = _router_call(xa, g_ffn_pre[l], sh2, sc2, moe_router[l // 2])
            plan = _moe_plan(route, counts)
            xs = _dispatch_call(xa, plan["pos"], plan["pad_start"], plan["pad_len"], plan["tail"],
                                g_ffn_pre[l], sh2, sc2, plan["rows"])
            ys = _ffn_moe_call(xs, plan["tile_expert"], plan["tile_nsb"], plan["tile_row0"],
                               moe_w_gu[l // 2], moe_w_down[l // 2])
            xa = _combine_call(xa, ys, plan["pos"], route, g_ffn_post[l], g2)
    return xa[:n_lat][None]
```

```python
import functools

import numpy as np
import jax
import jax.numpy as jnp
from jax import lax
from jax.experimental import pallas as pl
from jax.experimental.pallas import tpu as pltpu

F32 = jnp.float32
BF16 = jnp.bfloat16

HEAD_DIM = 128
GRID_W = 64
CHUNK = 128
WINDOW_BLOCK = 128
B_HEADS = 6
B_KV_HEADS = 2
C_HEADS = 6
NA_ROWS = 8
NA_COLS = 16
N_EXPERTS = 8
ROPE_BASE = 10000.0
EPS = 1e-6
NEG_INF = -1e30

V7X_VMEM_BYTES = 64 * 1024 * 1024
VMEM_LIMIT = V7X_VMEM_BYTES - 6 * 1024 * 1024
LANES = 128

A_WIDTH = 512
QB_COL = 2 * A_WIDTH // HEAD_DIM
QC_COL = QB_COL + B_HEADS
KB_COL = QC_COL + C_HEADS
VB_COL = KB_COL + B_KV_HEADS
KC_COL = VB_COL + B_KV_HEADS
VC_COL = KC_COL + C_HEADS
IN_COLS = VC_COL + C_HEADS


def _params(sem, vmem=VMEM_LIMIT):
    return pltpu.CompilerParams(dimension_semantics=sem, vmem_limit_bytes=vmem)


def _rms(x):
    return x * lax.rsqrt(jnp.mean(x * x, axis=-1, keepdims=True) + EPS)


def _pick_rows(v2, row0, tm, n_lat):
    rows = row0 + lax.broadcasted_iota(jnp.int32, (tm, 1), 0)
    return jnp.where(rows >= n_lat, v2[1:2, :], v2[0:1, :])


def _norm_mod(x, g, sh2, sc2, row0, tm, n_lat):
    y = _rms(x) * g
    return y * (1.0 + _pick_rows(sc2, row0, tm, n_lat)) + _pick_rows(sh2, row0, tm, n_lat)


def _dot(a, b):
    return jnp.dot(a, b, preferred_element_type=F32)


def _dot_nt(a, b):
    return lax.dot_general(a, b, (((1,), (1,)), ((), ())), preferred_element_type=F32)


def _mod_kernel(c_ref, w_ref, b_ref, o_ref):
    a = jax.nn.silu(c_ref[...]).astype(BF16)
    o_ref[...] = _dot(a, w_ref[...].astype(BF16)) + b_ref[...]


def _mod_call(c8, w_mod, b_mod):
    depth, d, n = w_mod.shape
    tn = 1536
    return pl.pallas_call(
        _mod_kernel,
        grid=(depth, n // tn),
        in_specs=[
            pl.BlockSpec((8, d), lambda l, j: (0, 0)),
            pl.BlockSpec((None, d, tn), lambda l, j: (l, 0, j)),
            pl.BlockSpec((None, 1, tn), lambda l, j: (l, 0, j)),
        ],
        out_specs=pl.BlockSpec((None, 8, tn), lambda l, j: (l, 0, j)),
        out_shape=jax.ShapeDtypeStruct((depth, 8, n), F32),
        compiler_params=_params(("parallel", "parallel")),
        name="modulation",
    )(c8, w_mod, b_mod.reshape(depth, 1, n))


def _rope_head(x, cos, sin):
    lane = lax.broadcasted_iota(jnp.int32, x.shape, 1)
    first = (lane & 63) < 32
    partner = jnp.where(first, pltpu.roll(x, 96, 1), pltpu.roll(x, 32, 1))
    return x * cos + partner * sin


def _inproj_kernel(x_ref, g_ref, sh_ref, sc_ref, w_ref, cos_ref, sin_ref, o_ref, *, tm, n_lat):
    i = pl.program_id(0)
    rc = INPROJ_ROW_CHUNK
    heads_per_dot = INPROJ_TN // HEAD_DIM
    for r in range(tm // rc):
        rows = slice(r * rc, (r + 1) * rc)
        h = _norm_mod(x_ref[rows, :], g_ref[...], sh_ref[...], sc_ref[...], i * tm + r * rc, rc, n_lat)
        h = h.astype(BF16)
        for jt in range(IN_COLS // heads_per_dot):
            acc = _dot(h, w_ref[:, jt * INPROJ_TN:(jt + 1) * INPROJ_TN])
            for hd in range(heads_per_dot):
                head = jt * heads_per_dot + hd
                a = acc[:, hd * HEAD_DIM:(hd + 1) * HEAD_DIM]
                if INPROJ_HEAD_KIND[head] == "gelu":
                    a = jax.nn.gelu(a, approximate=True)
                elif INPROJ_HEAD_KIND[head] == "rope":
                    a = _rope_head(a, cos_ref[rows, :], sin_ref[rows, :])
                o_ref[rows, head * HEAD_DIM:(head + 1) * HEAD_DIM] = a.astype(BF16)


INPROJ_HEAD_KIND = (("gelu",) * QB_COL + ("rope",) * B_HEADS + ("none",) * C_HEADS
                    + ("rope",) * B_KV_HEADS + ("none",) * (B_KV_HEADS + 2 * C_HEADS))
INPROJ_TN = 12 * HEAD_DIM
INPROJ_ROW_CHUNK = 384


def _inproj_call(xa, g, sh2, sc2, w_all, layer, cos, sin, *, tm, n_lat):
    t, d = xa.shape
    n = w_all.shape[2]
    assert t % tm == 0 and tm % INPROJ_ROW_CHUNK == 0 and w_all.dtype == BF16
    assert n == IN_COLS * HEAD_DIM and n % INPROJ_TN == 0
    kern = functools.partial(_inproj_kernel, tm=tm, n_lat=n_lat)
    return pl.pallas_call(
        kern,
        grid=(t // tm,),
        in_specs=[
            pl.BlockSpec((tm, d), lambda i: (i, 0)),
            pl.BlockSpec((1, d), lambda i: (0, 0)),
            pl.BlockSpec((2, d), lambda i: (0, 0)),
            pl.BlockSpec((2, d), lambda i: (0, 0)),
            pl.BlockSpec((None, d, n), lambda i: (layer, 0, 0), pipeline_mode=pl.Buffered(1)),
            pl.BlockSpec((tm, HEAD_DIM), lambda i: (i, 0)),
            pl.BlockSpec((tm, HEAD_DIM), lambda i: (i, 0)),
        ],
        out_specs=pl.BlockSpec((tm, n), lambda i: (i, 0)),
        out_shape=jax.ShapeDtypeStruct((t, n), BF16),
        compiler_params=_params(("parallel",)),
        name="in_projection",
    )(xa, g.reshape(1, d), sh2, sc2, w_all, cos, sin)


def _rope_tables(n_lat, n_ctx):
    t = np.arange(n_lat)
    n = HEAD_DIM // 4
    inv = (ROPE_BASE ** (-(2.0 / (HEAD_DIM // 2)) * np.arange(n, dtype=np.float32))).astype(np.float32)
    ang_r = (t // GRID_W).astype(np.float32)[:, None] * inv[None, :]
    ang_c = (t % GRID_W).astype(np.float32)[:, None] * inv[None, :]
    cos = np.concatenate([np.cos(ang_r)] * 2 + [np.cos(ang_c)] * 2, axis=1)
    sin = np.concatenate([-np.sin(ang_r), np.sin(ang_r), -np.sin(ang_c), np.sin(ang_c)], axis=1)
    cos = np.concatenate([cos, np.ones((n_ctx, HEAD_DIM), np.float32)], axis=0)
    sin = np.concatenate([sin, np.zeros((n_ctx, HEAD_DIM), np.float32)], axis=0)
    return jnp.asarray(cos, F32), jnp.asarray(sin, F32)


def _sgu_kernel(u_ref, v_ref, lng_ref, w_ref, b_ref, o_ref, *, chunks):
    for c in range(chunks):
        rows = slice(c * CHUNK, (c + 1) * CHUNK)
        for g in range(A_WIDTH // HEAD_DIM):
            cols = slice(g * HEAD_DIM, (g + 1) * HEAD_DIM)
            v = v_ref[rows, cols].astype(F32)
            mu = jnp.mean(v, axis=-1, keepdims=True)
            var = jnp.mean(jnp.square(v - mu), axis=-1, keepdims=True)
            vn = (v - mu) * lax.rsqrt(var + EPS) * lng_ref[:, cols]
            s = _dot(w_ref[g].astype(BF16), vn.astype(BF16)) + b_ref[g]
            o_ref[rows, cols] = (u_ref[rows, cols].astype(F32) * s).astype(BF16)


def _sgu_call(p, ln_g, w_s, b_s, *, chunks):
    t = p.shape[0]
    rows = chunks * CHUNK
    assert t % rows == 0
    groups = A_WIDTH // HEAD_DIM
    b_full = jnp.broadcast_to(b_s[:, :, None], (groups, CHUNK, HEAD_DIM))
    kern = functools.partial(_sgu_kernel, chunks=chunks)
    return pl.pallas_call(
        kern,
        grid=(t // rows,),
        in_specs=[
            pl.BlockSpec((rows, A_WIDTH), lambda i: (i, 0)),
            pl.BlockSpec((rows, A_WIDTH), lambda i: (i, 1)),
            pl.BlockSpec((1, A_WIDTH), lambda i: (0, 0)),
            pl.BlockSpec((groups, CHUNK, CHUNK), lambda i: (0, 0, 0)),
            pl.BlockSpec((groups, CHUNK, HEAD_DIM), lambda i: (0, 0, 0)),
        ],
        out_specs=pl.BlockSpec((rows, A_WIDTH), lambda i: (i, 0)),
        out_shape=jax.ShapeDtypeStruct((t, A_WIDTH), BF16),
        compiler_params=_params(("parallel",)),
        name="spatial_gating",
    )(p, p, ln_g.reshape(1, A_WIDTH), w_s, b_full)


WIN_Q_BLOCKS = 4


def _softmax_pv(s_loc, s_ctx, sink_col, v_loc, v_ctx):
    n_loc = s_loc.shape[1]
    s = jnp.concatenate([s_loc, s_ctx], axis=1)
    m = jnp.max(s, axis=1, keepdims=True)
    if sink_col is not None:
        m = jnp.maximum(m, sink_col)
    p = jnp.exp(s - m)
    den = jnp.sum(p, axis=1, keepdims=True)
    if sink_col is not None:
        den = den + jnp.exp(sink_col - m)
    p = p.astype(BF16)
    o = _dot(p[:, :n_loc], v_loc) + _dot(p[:, n_loc:], v_ctx)
    return o / den


def _win_kernel(sink_ref, q0, q1, q2, kp, km, kn, vp, vm, vn, kc, vc, o_ref, kcat, vcat, *, n_lat):
    kv = pl.program_id(0)
    s = pl.program_id(1)
    wb = WINDOW_BLOCK
    main = WIN_Q_BLOCKS * wb
    kcat[0:wb] = kp[...]
    kcat[wb:wb + main] = km[...]
    kcat[wb + main:2 * wb + main] = kn[...]
    vcat[0:wb] = vp[...]
    vcat[wb:wb + main] = vm[...]
    vcat[wb + main:2 * wb + main] = vn[...]
    scale = HEAD_DIM ** -0.5
    g = B_HEADS // B_KV_HEADS
    qs = (q0, q1, q2)
    row = lax.broadcasted_iota(jnp.int32, (g * wb, 3 * wb), 0) & (wb - 1)
    col = lax.broadcasted_iota(jnp.int32, (g * wb, 3 * wb), 1)
    rel = col - wb - row
    band_bias = jnp.where(rel < -wb, NEG_INF, jnp.where(rel > wb, NEG_INF, 0.0))
    col1 = lax.broadcasted_iota(jnp.int32, (1, 3 * wb), 1)
    sink_col = jnp.concatenate(
        [jnp.full((wb, 1), sink_ref[kv * g + gi], F32) for gi in range(g)], axis=0)
    n_blocks = n_lat // wb
    for b in range(WIN_Q_BLOCKS):
        n = s * WIN_Q_BLOCKS + b
        rows = slice(b * wb, (b + 1) * wb)
        q3 = jnp.concatenate([qs[gi][rows, :] for gi in range(g)], axis=0)
        keys = kcat[b * wb:(b + 3) * wb, :]
        vals = vcat[b * wb:(b + 3) * wb, :]
        off_start = jnp.where(n == 0, NEG_INF, 0.0)
        off_end = jnp.where(n == n_blocks - 1, NEG_INF, 0.0)
        edge = jnp.where(col1 < wb, off_start, jnp.where(col1 >= 2 * wb, off_end, 0.0))
        s_loc = _dot_nt(q3, keys) * scale + band_bias + edge
        s_ctx = _dot_nt(q3, kc[...]) * scale
        o = _softmax_pv(s_loc, s_ctx, sink_col, vals, vc[...])
        for gi in range(g):
            o_ref[rows, gi * HEAD_DIM:(gi + 1) * HEAD_DIM] = o[gi * wb:(gi + 1) * wb, :].astype(BF16)


def _win_call(p, sink, *, n_lat, n_ctx):
    wb = WINDOW_BLOCK
    main = WIN_Q_BLOCKS * wb
    assert n_lat % main == 0 and n_lat % n_ctx == 0
    nsb = n_lat // main
    nb = n_lat // wb
    g = B_HEADS // B_KV_HEADS
    ctx_blk = n_lat // n_ctx

    def qspec(gi):
        return pl.BlockSpec((main, HEAD_DIM), lambda kv, s: (s, QB_COL + kv * g + gi))

    def band_specs(col0):
        return [
            pl.BlockSpec((wb, HEAD_DIM), lambda kv, s: (jnp.maximum(s * WIN_Q_BLOCKS - 1, 0), col0 + kv)),
            pl.BlockSpec((main, HEAD_DIM), lambda kv, s: (s, col0 + kv)),
            pl.BlockSpec((wb, HEAD_DIM), lambda kv, s: (jnp.minimum((s + 1) * WIN_Q_BLOCKS, nb - 1), col0 + kv)),
        ]

    def ctx_spec(col0):
        return pl.BlockSpec((n_ctx, HEAD_DIM), lambda kv, s: (ctx_blk, col0 + kv))

    kern = functools.partial(_win_kernel, n_lat=n_lat)
    return pl.pallas_call(
        kern,
        grid=(B_KV_HEADS, nsb),
        in_specs=[pl.BlockSpec(memory_space=pltpu.SMEM), qspec(0), qspec(1), qspec(2)]
        + band_specs(KB_COL) + band_specs(VB_COL) + [ctx_spec(KB_COL), ctx_spec(VB_COL)],
        out_specs=pl.BlockSpec((main, g * HEAD_DIM), lambda kv, s: (s, kv)),
        out_shape=jax.ShapeDtypeStruct((n_lat, B_HEADS * HEAD_DIM), BF16),
        scratch_shapes=[pltpu.VMEM((main + 2 * wb, HEAD_DIM), BF16)] * 2,
        compiler_params=_params(("parallel", "parallel")),
        name="window_attention",
    )(sink, p, p, p, p, p, p, p, p, p, p, p)


NA_Q_ROWS = 8
NA_K_ROWS = 16


def _na_row_windows(group, n_rows):
    lead = (NA_K_ROWS - NA_Q_ROWS) // 2
    rq = group * NA_Q_ROWS + np.arange(NA_Q_ROWS)
    rk = group * NA_Q_ROWS - lead + np.arange(NA_K_ROWS)
    r0 = np.clip(rq - NA_ROWS // 2, 0, n_rows - NA_ROWS)
    valid = (rk[None, :] >= r0[:, None]) & (rk[None, :] < r0[:, None] + NA_ROWS)
    roff = rk[None, :] - rq[:, None] + (NA_ROWS - 1)
    return valid, roff


def _na_fill_bias(bt_ref, mb_ref, group, n_rows):
    valid, roff = _na_row_windows(group, n_rows)
    w = GRID_W
    left = lax.broadcasted_iota(jnp.int32, (w, 2 * w), 1) < w
    neg = jnp.full((w, 2 * w), NEG_INF, F32)
    for i in range(NA_Q_ROWS):
        for u in range(0, NA_K_ROWS, 2):
            lo = bt_ref[int(roff[i, u])] if valid[i, u] else neg
            hi = bt_ref[int(roff[i, u + 1])] if valid[i, u + 1] else neg
            blk = neg if not (valid[i, u] or valid[i, u + 1]) else jnp.where(left, lo, hi)
            mb_ref[i * w:(i + 1) * w, u * w:(u + 2) * w] = blk


def _na_kernel(q, kp, km, kn, vp, vm, vn, kc, vc, bt, o, kcat, vcat, mb, *, n_rows):
    a = pl.program_id(1)
    ng = n_rows // NA_Q_ROWS
    for group in sorted({0, min(1, ng - 1), ng - 1}):
        @pl.when(a == group)
        def _(group=group):
            _na_fill_bias(bt, mb, group, n_rows)

    half = (NA_K_ROWS - NA_Q_ROWS) // 2 * GRID_W
    main = NA_Q_ROWS * GRID_W
    kcat[0:half] = kp[...]
    kcat[half:half + main] = km[...]
    kcat[half + main:2 * half + main] = kn[...]
    vcat[0:half] = vp[...]
    vcat[half:half + main] = vm[...]
    vcat[half + main:2 * half + main] = vn[...]
    scale = HEAD_DIM ** -0.5
    qc = NA_Q_ROWS // 2
    for c in range(2):
        rows = slice(c * qc * GRID_W, (c + 1) * qc * GRID_W)
        keys = slice(c * qc * GRID_W, (c * qc + qc + NA_ROWS) * GRID_W)
        qv = q[rows, :]
        s_loc = _dot_nt(qv, kcat[keys, :]) * scale + mb[rows, keys]
        s_ctx = _dot_nt(qv, kc[...]) * scale
        o[rows, :] = _softmax_pv(s_loc, s_ctx, None, vcat[keys, :], vc[...]).astype(BF16)


def _na_col_table(rpb):
    w = GRID_W
    cq = np.arange(w)
    c0 = np.clip(cq - NA_COLS // 2, 0, w - NA_COLS)
    cmask = (cq[None, :] >= c0[:, None]) & (cq[None, :] < c0[:, None] + NA_COLS)
    coff = np.clip(cq[None, :] - cq[:, None] + (NA_COLS - 1), 0, 2 * NA_COLS - 2)
    onehot = (coff[None] == np.arange(2 * NA_COLS - 1)[:, None, None]).astype(np.float32)
    picked = jnp.einsum("hrj,jqk->hrqk", rpb.astype(F32), onehot, precision=lax.Precision.HIGHEST)
    by_col = jnp.where(cmask[None, None], picked, NEG_INF)
    return jnp.concatenate([by_col, by_col], axis=-1)


def _na_call(p, rpb, *, n_lat, n_ctx):
    main = NA_Q_ROWS * GRID_W
    half = (NA_K_ROWS - NA_Q_ROWS) // 2 * GRID_W
    assert n_lat % main == 0 and n_lat % n_ctx == 0 and main == 2 * half
    ng = n_lat // main
    assert ng >= 2
    nhalf = n_lat // half
    ctx_blk = n_lat // n_ctx
    bt = _na_col_table(rpb)
    kern = functools.partial(_na_kernel, n_rows=n_lat // GRID_W)

    def band_specs(col0):
        return [
            pl.BlockSpec((half, HEAD_DIM), lambda h, a: (jnp.maximum(2 * a - 1, 0), col0 + h)),
            pl.BlockSpec((main, HEAD_DIM), lambda h, a: (a, col0 + h)),
            pl.BlockSpec((half, HEAD_DIM), lambda h, a: (jnp.minimum(2 * a + 2, nhalf - 1), col0 + h)),
        ]

    return pl.pallas_call(
        kern,
        grid=(C_HEADS, ng),
        in_specs=[pl.BlockSpec((main, HEAD_DIM), lambda h, a: (a, QC_COL + h))]
        + band_specs(KC_COL) + band_specs(VC_COL)
        + [pl.BlockSpec((n_ctx, HEAD_DIM), lambda h, a: (ctx_blk, KC_COL + h)),
           pl.BlockSpec((n_ctx, HEAD_DIM), lambda h, a: (ctx_blk, VC_COL + h)),
           pl.BlockSpec((None,) + bt.shape[1:], lambda h, a: (h, 0, 0, 0))],
        out_specs=pl.BlockSpec((main, HEAD_DIM), lambda h, a: (a, h)),
        out_shape=jax.ShapeDtypeStruct((n_lat, C_HEADS * HEAD_DIM), BF16),
        scratch_shapes=[pltpu.VMEM((main + 2 * half, HEAD_DIM), BF16)] * 2
        + [pltpu.VMEM((main, NA_K_ROWS * GRID_W), F32)],
        compiler_params=_params(("arbitrary", "arbitrary")),
        name="neighbourhood_attention",
    )(p, p, p, p, p, p, p, p, p, bt)


def _ctx_attn_kernel(sink_ref, q, k, v, o):
    hh = pl.program_id(0)
    scale = HEAD_DIM ** -0.5
    s = _dot_nt(q[...], k[...]) * scale
    sink = jnp.full((s.shape[0], 1), sink_ref[hh], F32)
    m = jnp.maximum(jnp.max(s, axis=1, keepdims=True), sink)
    pr = jnp.exp(s - m)
    den = jnp.sum(pr, axis=1, keepdims=True) + jnp.exp(sink - m)
    o[...] = (_dot(pr.astype(BF16), v[...]) / den).astype(BF16)


def _ctx_attn_call(p, sink12, *, n_lat, n_ctx):
    blk = n_lat // n_ctx
    g = B_HEADS // B_KV_HEADS

    def kcol(hh):
        return jnp.where(hh < B_HEADS, KB_COL + hh // g, KC_COL + hh - B_HEADS)

    def vcol(hh):
        return jnp.where(hh < B_HEADS, VB_COL + hh // g, VC_COL + hh - B_HEADS)

    return pl.pallas_call(
        _ctx_attn_kernel,
        grid=(B_HEADS + C_HEADS,),
        in_specs=[
            pl.BlockSpec(memory_space=pltpu.SMEM),
            pl.BlockSpec((n_ctx, HEAD_DIM), lambda hh: (blk, QB_COL + hh)),
            pl.BlockSpec((n_ctx, HEAD_DIM), lambda hh: (blk, kcol(hh))),
            pl.BlockSpec((n_ctx, HEAD_DIM), lambda hh: (blk, vcol(hh))),
        ],
        out_specs=pl.BlockSpec((n_ctx, HEAD_DIM), lambda hh: (0, hh)),
        out_shape=jax.ShapeDtypeStruct((n_ctx, (B_HEADS + C_HEADS) * HEAD_DIM), BF16),
        compiler_params=_params(("parallel",)),
        name="context_attention",
    )(sink12, p, p, p)


def _outproj_kernel(ya_ref, yb_ref, yc_ref, w_ref, x_ref, g_ref, gate_ref, o_ref, *, tm, n_lat):
    i = pl.program_id(0)
    ka = ya_ref.shape[1]
    kb = yb_ref.shape[1]
    acc = _dot(ya_ref[...], w_ref[0:ka, :])
    acc += _dot(yb_ref[...], w_ref[ka:ka + kb, :])
    acc += _dot(yc_ref[...], w_ref[ka + kb:, :])
    r = _rms(acc) * g_ref[...]
    o_ref[...] = x_ref[...] + _pick_rows(gate_ref[...], i * tm, tm, n_lat) * r


def _outproj_call(ya, yb, yc, w_all_bf16, layer, xa, g, gate2, *, rows, tm, n_lat):
    d = xa.shape[1]
    assert rows % tm == 0 and ya.shape[1] + yb.shape[1] + yc.shape[1] == d
    kern = functools.partial(_outproj_kernel, tm=tm, n_lat=n_lat)
    return pl.pallas_call(
        kern,
        grid=(rows // tm,),
        in_specs=[
            pl.BlockSpec((tm, ya.shape[1]), lambda i: (i, 0)),
            pl.BlockSpec((tm, yb.shape[1]), lambda i: (i, 0)),
            pl.BlockSpec((tm, yc.shape[1]), lambda i: (i, 0)),
            pl.BlockSpec((None, d, d), lambda i: (layer, 0, 0)),
            pl.BlockSpec((tm, d), lambda i: (i, 0)),
            pl.BlockSpec((1, d), lambda i: (0, 0)),
            pl.BlockSpec((2, d), lambda i: (0, 0)),
        ],
        out_specs=pl.BlockSpec((tm, d), lambda i: (i, 0)),
        out_shape=jax.ShapeDtypeStruct((rows, d), F32),
        compiler_params=_params(("parallel",)),
        name="out_projection",
    )(ya, yb, yc, w_all_bf16, xa, g.reshape(1, d), gate2)


FFN_TF = 256


def _ffn_weight_specs(d, tf, nf, expert_of, chunk_of=lambda m, f, *pf: f, down_chunk_of=None):
    down_chunk_of = down_chunk_of or chunk_of
    return [
        pl.BlockSpec((None, d, tf), lambda m, f, *pf: (expert_of(m, *pf), 0, chunk_of(m, f, *pf))),
        pl.BlockSpec((None, d, tf), lambda m, f, *pf: (expert_of(m, *pf), 0, nf + chunk_of(m, f, *pf))),
        pl.BlockSpec((None, tf, d), lambda m, f, *pf: (expert_of(m, *pf), down_chunk_of(m, f, *pf), 0)),
    ]


def _ffn_dense_kernel(x_ref, gpre_ref, sh_ref, sc_ref, wg_ref, wu_ref, wd_ref, gpost_ref, gate_ref,
                      o_ref, h_ref, act_ref, *, tm, n_lat):
    m = pl.program_id(0)
    f = pl.program_id(1)
    nf = pl.num_programs(1) - 1

    def gate_up():
        h = h_ref[...]
        gt = _dot(h, wg_ref[...].astype(BF16))
        up = _dot(h, wu_ref[...].astype(BF16))
        return (jax.nn.silu(gt) * up).astype(BF16)

    def down(act):
        o_ref[...] += _dot(act, wd_ref[...].astype(BF16))

    @pl.when(f == 0)
    def _():
        h = _norm_mod(x_ref[...], gpre_ref[...], sh_ref[...], sc_ref[...], m * tm, tm, n_lat)
        h_ref[...] = h.astype(BF16)
        o_ref[...] = jnp.zeros_like(o_ref)
        act_ref[0] = gate_up()

    @pl.when((f > 0) & (f < nf))
    def _():
        prev = act_ref[(f - 1) % 2]
        act_ref[f % 2] = gate_up()
        down(prev)

    @pl.when(f == nf)
    def _():
        down(act_ref[(f - 1) % 2])
        r = _rms(o_ref[...]) * gpost_ref[...]
        o_ref[...] = x_ref[...] + _pick_rows(gate_ref[...], m * tm, tm, n_lat) * r


def _ffn_dense_call(xa, gpre, sh2, sc2, w_gu, w_down, layer_set, gpost, gate2, *, tm, n_lat):
    t, d = xa.shape
    ffn = w_gu.shape[2] // 2
    tf = FFN_TF
    nf = ffn // tf
    assert t % tm == 0 and ffn % tf == 0
    kern = functools.partial(_ffn_dense_kernel, tm=tm, n_lat=n_lat)
    vec = lambda rows: pl.BlockSpec((rows, d), lambda m, f: (0, 0))
    return pl.pallas_call(
        kern,
        grid=(t // tm, nf + 1),
        in_specs=[pl.BlockSpec((tm, d), lambda m, f: (m, 0), pipeline_mode=pl.Buffered(1)),
                  vec(1), vec(2), vec(2)]
        + _ffn_weight_specs(d, tf, nf, lambda m: layer_set,
                            chunk_of=lambda m, f: jnp.minimum(f, nf - 1),
                            down_chunk_of=lambda m, f: jnp.maximum(f - 1, 0)) + [vec(1), vec(2)],
        out_specs=pl.BlockSpec((tm, d), lambda m, f: (m, 0)),
        out_shape=jax.ShapeDtypeStruct((t, d), F32),
        scratch_shapes=[pltpu.VMEM((tm, d), BF16), pltpu.VMEM((2, tm, tf), BF16)],
        compiler_params=_params(("parallel", "arbitrary")),
        name="swiglu_ffn",
    )(xa, gpre.reshape(1, d), sh2, sc2, w_gu, w_gu, w_down, gpost.reshape(1, d), gate2)


MOE_SB = 512
MOE_NSB = 4


DISPATCH_TM = 512


def _dispatch_kernel(pos_ref, pad0_ref, padn_ref, tail_ref, x_ref, gpre_ref, sh_ref, sc_ref, xs_hbm, hbuf, sems,
                     *, tm, sb):
    i = pl.program_id(0)
    last = pl.num_programs(0) - 1
    h = _rms(x_ref[...]) * gpre_ref[...]
    hbuf[i % 2] = h * (1.0 + sc_ref[0:1, :]) + sh_ref[0:1, :]

    def copy(step, t, k):
        p = pos_ref[(step * tm + t) * 2 + k]
        return pltpu.make_async_copy(hbuf.at[step % 2, pl.ds(t, 1)], xs_hbm.at[pl.ds(p, 1)], sems.at[step % 2])

    def issue(t, carry):
        copy(i, t, 0).start()
        copy(i, t, 1).start()
        return carry

    lax.fori_loop(0, tm, issue, 0, unroll=8)

    def drain_step(step):
        def drain(t, carry):
            copy(step, t, 0).wait()
            copy(step, t, 1).wait()
            return carry

        lax.fori_loop(0, tm, drain, 0, unroll=8)

    @pl.when(i > 0)
    def _():
        drain_step(i - 1)

    @pl.when(i == last)
    def _():
        drain_step(i)
        hbuf[0] = jnp.zeros(hbuf.shape[1:], F32)
        def zero_row(e, r):
            return pltpu.make_async_copy(hbuf.at[0, pl.ds(0, 1)], xs_hbm.at[pl.ds(pad0_ref[e] + r, 1)], sems.at[0])

        for e in range(N_EXPERTS):
            lax.fori_loop(0, padn_ref[e], lambda r, c, e=e: (zero_row(e, r).start(), c)[1], 0)
        for e in range(N_EXPERTS):
            lax.fori_loop(0, padn_ref[e], lambda r, c, e=e: (zero_row(e, r).wait(), c)[1], 0)

        def zero_block(g, carry):
            row = pl.multiple_of(tail_ref[0] + g * sb, sb)
            cp = pltpu.make_async_copy(hbuf.at[0, pl.ds(0, sb)], xs_hbm.at[pl.ds(row, sb)], sems.at[0])
            cp.start()
            cp.wait()
            return carry

        lax.fori_loop(0, (xs_hbm.shape[0] - tail_ref[0]) // sb, zero_block, 0)


def _dispatch_call(x, pos, pad0, padn, tail, gpre, sh2, sc2, rows_out):
    t, d = x.shape
    tm = DISPATCH_TM
    assert t % tm == 0 and tm >= MOE_SB
    kern = functools.partial(_dispatch_kernel, tm=tm, sb=MOE_SB)
    vec = lambda r: pl.BlockSpec((r, d), lambda i, *pf: (0, 0))
    grid_spec = pltpu.PrefetchScalarGridSpec(
        num_scalar_prefetch=4,
        grid=(t // tm,),
        in_specs=[pl.BlockSpec((tm, d), lambda i, *pf: (i, 0)), vec(1), vec(2), vec(2)],
        out_specs=pl.BlockSpec(memory_space=pl.ANY),
        scratch_shapes=[pltpu.VMEM((2, tm, d), F32), pltpu.SemaphoreType.DMA((2,))],
    )
    return pl.pallas_call(
        kern,
        grid_spec=grid_spec,
        out_shape=jax.ShapeDtypeStruct((rows_out, d), F32),
        compiler_params=_params(("arbitrary",)),
        name="dispatch",
    )(pos, pad0, padn, tail, x, gpre.reshape(1, d), sh2, sc2)


def _ffn_moe_kernel(te_ref, ns_ref, r0_ref, xs_hbm, wg_ref, wu_ref, wd_ref, o_hbm,
                    stage, h_ref, acc, wgb, wub, wdb, gsem, osem, *, sb, nsb):
    m = pl.program_id(0)
    f = pl.program_id(1)
    n = ns_ref[m]
    n_tiles = pl.num_programs(0)
    cur = m % 2

    def out_copy(mm, s):
        row = pl.multiple_of(r0_ref[mm] + s * sb, sb)
        return pltpu.make_async_copy(acc.at[pl.ds(s * sb, sb)], o_hbm.at[pl.ds(row, sb)], osem)

    def in_copy(mm, s):
        row = pl.multiple_of(r0_ref[mm] + s * sb, sb)
        return pltpu.make_async_copy(xs_hbm.at[pl.ds(row, sb)], stage, gsem)

    def for_valid(count, body):
        for s in range(nsb):
            @pl.when(s < count)
            def _(s=s):
                body(s)

    @pl.when(f == 0)
    def _():
        @pl.when(m == 0)
        def _():
            def load(s):
                in_copy(m, s).start()
                in_copy(m, s).wait()
                h_ref[0, s * sb:(s + 1) * sb, :] = stage[...].astype(BF16)

            for_valid(n, load)

        @pl.when(m > 0)
        def _():
            for_valid(ns_ref[jnp.maximum(m - 1, 0)], lambda s: out_copy(m - 1, s).wait())

        def clear(s):
            acc[s * sb:(s + 1) * sb, :] = jnp.zeros((sb, acc.shape[1]), F32)

        for_valid(n, clear)

    nxt = jnp.minimum(m + 1, n_tiles - 1)
    n_next = jnp.where(m + 1 < n_tiles, ns_ref[nxt], 0)
    s_next = f // 2

    @pl.when((f % 2 == 0) & (s_next < n_next))
    def _():
        in_copy(nxt, s_next).start()

    @pl.when((f % 2 == 1) & (s_next < n_next))
    def _():
        in_copy(nxt, s_next).wait()
        h_ref[1 - cur, pl.ds(pl.multiple_of(s_next * sb, sb), sb), :] = stage[...].astype(BF16)

    def accumulate(subs):
        if 0 in subs:
            wg = wg_ref[...].astype(BF16)
            wu = wu_ref[...].astype(BF16)
            wd = wd_ref[...].astype(BF16)
            wgb[...] = wg
            wub[...] = wu
            wdb[...] = wd
        else:
            wg, wu, wd = wgb[...], wub[...], wdb[...]
        for s in subs:
            h = h_ref[cur, s * sb:(s + 1) * sb, :]
            act = (jax.nn.silu(_dot(h, wg)) * _dot(h, wu)).astype(BF16)
            acc[s * sb:(s + 1) * sb, :] += _dot(act, wd)

    for s0 in range(0, nsb, 2):
        if s0 + 1 < nsb:
            @pl.when(s0 + 1 < n)
            def _(s0=s0):
                accumulate((s0, s0 + 1))

        @pl.when(s0 + 1 == n)
        def _(s0=s0):
            accumulate((s0,))

    @pl.when(f == pl.num_programs(1) - 1)
    def _():
        for_valid(n, lambda s: out_copy(m, s).start())

        @pl.when(m == n_tiles - 1)
        def _():
            for_valid(n, lambda s: out_copy(m, s).wait())
            stage[...] = jnp.zeros(stage.shape, F32)

            def fill(g, carry):
                row = pl.multiple_of(r0_ref[n_tiles] + g * sb, sb)
                cp = pltpu.make_async_copy(stage, o_hbm.at[pl.ds(row, sb)], osem)
                cp.start()
                cp.wait()
                return carry

            lax.fori_loop(0, (o_hbm.shape[0] - r0_ref[n_tiles]) // sb, fill, 0)


def _ffn_moe_call(xs, tile_expert, tile_nsb, tile_row0, w_gu, w_down):
    rows, d = xs.shape
    sb, nsb = MOE_SB, MOE_NSB
    n_tiles = tile_expert.shape[0]
    ffn = w_gu.shape[2] // 2
    tf = FFN_TF
    nf = ffn // tf
    assert rows % sb == 0 and tile_row0.shape[0] == n_tiles + 1 and ffn % tf == 0 and nf >= 2 * nsb
    kern = functools.partial(_ffn_moe_kernel, sb=sb, nsb=nsb)
    grid_spec = pltpu.PrefetchScalarGridSpec(
        num_scalar_prefetch=3,
        grid=(n_tiles, nf),
        in_specs=[pl.BlockSpec(memory_space=pl.ANY)]
        + _ffn_weight_specs(d, tf, nf, lambda m, te, *pf: te[m],
                            lambda m, f, te, ns, *pf: jnp.where(ns[m] > 0, f, nf - 1)),
        out_specs=pl.BlockSpec(memory_space=pl.ANY),
        scratch_shapes=[pltpu.VMEM((sb, d), F32), pltpu.VMEM((2, sb * nsb, d), BF16),
                        pltpu.VMEM((sb * nsb, d), F32),
                        pltpu.VMEM((d, tf), BF16), pltpu.VMEM((d, tf), BF16), pltpu.VMEM((tf, d), BF16),
                        pltpu.SemaphoreType.DMA(()), pltpu.SemaphoreType.DMA(())],
    )
    return pl.pallas_call(
        kern,
        grid_spec=grid_spec,
        out_shape=jax.ShapeDtypeStruct((rows, d), F32),
        compiler_params=_params(("arbitrary", "arbitrary")),
        name="expert_ffn",
    )(tile_expert, tile_nsb, tile_row0, xs, w_gu, w_gu, w_down)


ROUTER_TM = 512
R_E1, R_E2, R_W1, R_W2, R_RANK1, R_RANK2 = range(6)


def _router_kernel(x_ref, gpre_ref, sh_ref, sc_ref, wr_ref, route_ref, count_ref, carry_ref, *, tm):
    i = pl.program_id(0)

    @pl.when(i == 0)
    def _():
        carry_ref[...] = jnp.zeros_like(carry_ref)

    h = _rms(x_ref[...]) * gpre_ref[...]
    h = h * (1.0 + sc_ref[0:1, :]) + sh_ref[0:1, :]
    w = wr_ref[...]
    h_hi = h.astype(BF16)
    h_lo = (h - h_hi.astype(F32)).astype(BF16)
    w_hi = w.astype(BF16)
    w_lo = (w - w_hi.astype(F32)).astype(BF16)
    logits = _dot(h_hi, w_hi) + (_dot(h_hi, w_lo) + _dot(h_lo, w_hi))
    lane_i = lax.broadcasted_iota(jnp.int32, logits.shape, 1)
    lane = lane_i.astype(F32)
    logits = jnp.where(lane_i < N_EXPERTS, logits, -jnp.inf)
    v1 = jnp.max(logits, axis=1, keepdims=True)
    e1 = jnp.min(jnp.where(logits == v1, lane, float(LANES)), axis=1, keepdims=True)
    rest = jnp.where(lane == e1, -jnp.inf, logits)
    v2 = jnp.max(rest, axis=1, keepdims=True)
    e2 = jnp.min(jnp.where(rest == v2, lane, float(LANES)), axis=1, keepdims=True)
    ex = jnp.exp(v2 - v1)
    w1 = 1.0 / (1.0 + ex)
    w2 = ex / (1.0 + ex)
    hit1 = lane == e1
    hit2 = lane == e2
    assign = jnp.where(hit1, 1.0, jnp.where(hit2, 1.0, 0.0))
    r = lax.broadcasted_iota(jnp.int32, (tm, tm), 0)
    c = lax.broadcasted_iota(jnp.int32, (tm, tm), 1)
    before = jnp.where(c < r, 1.0, 0.0).astype(BF16)
    prefix = _dot(before, assign.astype(BF16)) + carry_ref[0:1, :]
    rank1 = jnp.sum(jnp.where(hit1, prefix, 0.0), axis=1, keepdims=True)
    rank2 = jnp.sum(jnp.where(hit2, prefix, 0.0), axis=1, keepdims=True)
    total = carry_ref[0:1, :] + jnp.sum(assign, axis=0, keepdims=True)
    carry_ref[...] = jnp.broadcast_to(total, carry_ref.shape)
    count_ref[...] = jnp.broadcast_to(total, count_ref.shape)
    rec = jnp.zeros(logits.shape, F32)
    for k, val in ((R_E1, e1), (R_E2, e2), (R_W1, w1), (R_W2, w2), (R_RANK1, rank1), (R_RANK2, rank2)):
        rec = jnp.where(lane_i == k, val, rec)
    route_ref[...] = rec


def _router_call(x, gpre, sh2, sc2, w_router):
    t, d = x.shape
    tm = ROUTER_TM
    assert t % tm == 0
    wr = jnp.zeros((d, LANES), F32).at[:, :N_EXPERTS].set(w_router)
    kern = functools.partial(_router_kernel, tm=tm)
    return pl.pallas_call(
        kern,
        grid=(t // tm,),
        in_specs=[
            pl.BlockSpec((tm, d), lambda i: (i, 0)),
            pl.BlockSpec((1, d), lambda i: (0, 0)),
            pl.BlockSpec((2, d), lambda i: (0, 0)),
            pl.BlockSpec((2, d), lambda i: (0, 0)),
            pl.BlockSpec((d, LANES), lambda i: (0, 0)),
        ],
        out_specs=[pl.BlockSpec((tm, LANES), lambda i: (i, 0)),
                   pl.BlockSpec((8, LANES), lambda i: (0, 0))],
        out_shape=[jax.ShapeDtypeStruct((t, LANES), F32), jax.ShapeDtypeStruct((8, LANES), F32)],
        scratch_shapes=[pltpu.VMEM((8, LANES), F32)],
        compiler_params=_params(("arbitrary",)),
        name="router_top2",
    )(x, gpre.reshape(1, d), sh2, sc2, wr)


COMBINE_TM = 256


def _combine_kernel(pos_ref, x_ref, route_ref, gpost_ref, gate_ref, ys_hbm, o_ref, buf, sem, *, tm):
    i = pl.program_id(0)
    slot = i % 2

    def copy(step, t, k):
        p = pos_ref[(step * tm + t) * 2 + k]
        return pltpu.make_async_copy(ys_hbm.at[pl.ds(p, 1)], buf.at[step % 2, k, pl.ds(t, 1)], sem.at[step % 2])

    def fetch(step):
        def issue(t, carry):
            copy(step, t, 0).start()
            copy(step, t, 1).start()
            return carry

        lax.fori_loop(0, tm, issue, 0, unroll=8)

    @pl.when(i == 0)
    def _():
        fetch(i)

    @pl.when(i + 1 < pl.num_programs(0))
    def _():
        fetch(i + 1)

    def drain(t, carry):
        copy(i, t, 0).wait()
        copy(i, t, 1).wait()
        return carry

    lax.fori_loop(0, tm, drain, 0, unroll=8)
    rec = route_ref[...]
    y = rec[:, R_W1:R_W1 + 1] * buf[slot, 0] + rec[:, R_W2:R_W2 + 1] * buf[slot, 1]
    o_ref[...] = x_ref[...] + gate_ref[0:1, :] * (_rms(y) * gpost_ref[...])


def _combine_call(x, ys, pos_flat, route, gpost, gate2):
    t, d = x.shape
    tm = COMBINE_TM
    assert t % tm == 0
    kern = functools.partial(_combine_kernel, tm=tm)
    grid_spec = pltpu.PrefetchScalarGridSpec(
        num_scalar_prefetch=1,
        grid=(t // tm,),
        in_specs=[
            pl.BlockSpec((tm, d), lambda i, pos: (i, 0)),
            pl.BlockSpec((tm, LANES), lambda i, pos: (i, 0)),
            pl.BlockSpec((1, d), lambda i, pos: (0, 0)),
            pl.BlockSpec((2, d), lambda i, pos: (0, 0)),
            pl.BlockSpec(memory_space=pl.ANY),
        ],
        out_specs=pl.BlockSpec((tm, d), lambda i, pos: (i, 0)),
        scratch_shapes=[pltpu.VMEM((2, 2, tm, d), F32), pltpu.SemaphoreType.DMA((2,))],
    )
    return pl.pallas_call(
        kern,
        grid_spec=grid_spec,
        out_shape=jax.ShapeDtypeStruct((t, d), F32),
        compiler_params=_params(("arbitrary",)),
        name="combine",
    )(pos_flat, x, route, gpost.reshape(1, d), gate2, ys)


DENSE_TM = 768
LAT_TM = 1024


def _moe_plan(route, counts):
    t = route.shape[0]
    e = N_EXPERTS
    sb, nsb = MOE_SB, MOE_NSB
    i32 = jnp.int32
    cnt = counts[0, :e].astype(i32)
    nsub = (cnt + sb - 1) // sb
    row_end = jnp.cumsum(nsub) * sb
    row_start = row_end - nsub * sb
    ntile = (nsub + nsb - 1) // nsb
    tile_end = jnp.cumsum(ntile)
    tile_start = tile_end - ntile
    ex = route[:, R_E1:R_E2 + 1].astype(i32)
    rank = route[:, R_RANK1:R_RANK2 + 1].astype(i32)
    pos = (row_start[ex] + rank).reshape(-1)
    max_sub = (2 * t + e * (sb - 1)) // sb
    max_tiles = (max_sub + e * (nsb - 1)) // nsb
    m = jnp.arange(max_tiles + 1, dtype=i32)
    used = m < tile_end[-1]
    m_used = jnp.minimum(m, tile_end[-1] - 1)
    te = jnp.minimum(jnp.sum((tile_end[None, :] <= m_used[:, None]).astype(i32), axis=1), e - 1)
    local = m_used - tile_start[te]
    per = nsub[te] // jnp.maximum(ntile[te], 1)
    extra = nsub[te] - per * ntile[te]
    tile_nsb = jnp.where(used, per + (local < extra).astype(i32), 0)
    first_sub = local * per + jnp.minimum(local, extra)
    tile_row0 = jnp.where(used, row_start[te] + first_sub * sb, row_end[-1])
    return dict(pos=pos, tile_expert=te[:-1].astype(i32), tile_nsb=tile_nsb[:-1].astype(i32),
                tile_row0=tile_row0.astype(i32), pad_start=(row_start + cnt).astype(i32),
                pad_len=(nsub * sb - cnt).astype(i32), tail=row_end[-1:].astype(i32), rows=max_sub * sb)


def kernel(x, c, ctx, c_ctx, w_mod, b_mod, g_mix_pre, g_mix_post, g_ffn_pre, g_ffn_post, w_in, sgu_ln_g,
           sgu_w, sgu_b, attn_sink, na_rpb, w_out, ffn_w_gu, ffn_w_down, moe_router, moe_w_gu, moe_w_down):
    assert x.shape[0] == 1 and ctx.shape[0] == 1
    n_lat, d = x.shape[1], x.shape[2]
    n_ctx = ctx.shape[1]
    depth = w_mod.shape[0]
    t_all = n_lat + n_ctx

    c8 = jnp.zeros((8, d), F32).at[0].set(c[0]).at[1].set(c_ctx)
    mod = _mod_call(c8, w_mod, b_mod)
    cos, sin = _rope_tables(n_lat, n_ctx)
    w_in_bf16 = w_in.astype(BF16)
    w_out_bf16 = w_out.astype(BF16)
    nb = B_HEADS * HEAD_DIM

    xa = jnp.concatenate([x[0], ctx[0]], axis=0)
    for l in range(depth):
        last = l == depth - 1
        m2 = mod[l, 0:2]
        sh1, sc1, g1, sh2, sc2, g2 = (m2[:, k * d:(k + 1) * d] for k in range(6))

        p = _inproj_call(xa, g_mix_pre[l], sh1, sc1, w_in_bf16, l, cos, sin, tm=DENSE_TM, n_lat=n_lat)
        y_sgu = _sgu_call(p, sgu_ln_g[l], sgu_w[l], sgu_b[l], chunks=DENSE_TM // CHUNK)
        y_win = _win_call(p, attn_sink[l], n_lat=n_lat, n_ctx=n_ctx)
        y_na = _na_call(p, na_rpb[l], n_lat=n_lat, n_ctx=n_ctx)
        if not last:
            sink12 = jnp.concatenate([attn_sink[l], jnp.full((C_HEADS,), NEG_INF, F32)])
            y_ctx = _ctx_attn_call(p, sink12, n_lat=n_lat, n_ctx=n_ctx)
            y_win = jnp.concatenate([y_win, y_ctx[:, :nb]], axis=0)
            y_na = jnp.concatenate([y_na, y_ctx[:, nb:]], axis=0)
            xa = _outproj_call(y_sgu, y_win, y_na, w_out_bf16, l, xa, g_mix_post[l], g1,
                               rows=t_all, tm=DENSE_TM, n_lat=n_lat)
        else:
            xa = _outproj_call(y_sgu, y_win, y_na, w_out_bf16, l, xa, g_mix_post[l], g1,
                               rows=n_lat, tm=LAT_TM, n_lat=n_lat)

        if l % 2 == 0:
            xa = _ffn_dense_call(xa, g_ffn_pre[l], sh2, sc2, ffn_w_gu, ffn_w_down, l // 2,
                                 g_ffn_post[l], g2, tm=DENSE_TM, n_lat=n_lat)
        else:
            assert last, "expert layers are only supported as the last layer (latent rows only)"
            route, counts = _router_call(xa, g_ffn_pre[l], sh2, sc2, moe_router[l // 2])
            plan = _moe_plan(route, counts)
            xs = _dispatch_call(xa, plan["pos"], plan["pad_start"], plan["pad_len"], plan["tail"],
                                g_ffn_pre[l], sh2, sc2, plan["rows"])
            ys = _ffn_moe_call(xs, plan["tile_expert"], plan["tile_nsb"], plan["tile_row0"],
                               moe_w_gu[l // 2], moe_w_down[l // 2])
            xa = _combine_call(xa, ys, plan["pos"], route, g_ffn_post[l], g2)
    return xa[:n_lat][None]
```

```python
import functools

import numpy as np
import jax
import jax.numpy as jnp
from jax import lax
from jax.experimental import pallas as pl
from jax.experimental.pallas import tpu as pltpu

F32 = jnp.float32
BF16 = jnp.bfloat16

HEAD_DIM = 128
GRID_W = 64
CHUNK = 128
WINDOW_BLOCK = 128
B_HEADS = 6
B_KV_HEADS = 2
C_HEADS = 6
NA_ROWS = 8
NA_COLS = 16
N_EXPERTS = 8
ROPE_BASE = 10000.0
EPS = 1e-6
NEG_INF = -1e30

V7X_VMEM_BYTES = 64 * 1024 * 1024
VMEM_LIMIT = V7X_VMEM_BYTES - 6 * 1024 * 1024
LANES = 128

A_WIDTH = 512
QB_COL = 2 * A_WIDTH // HEAD_DIM
QC_COL = QB_COL + B_HEADS
KB_COL = QC_COL + C_HEADS
VB_COL = KB_COL + B_KV_HEADS
KC_COL = VB_COL + B_KV_HEADS
VC_COL = KC_COL + C_HEADS
IN_COLS = VC_COL + C_HEADS


def _params(sem, vmem=VMEM_LIMIT):
    return pltpu.CompilerParams(dimension_semantics=sem, vmem_limit_bytes=vmem)


def _rms(x):
    return x * lax.rsqrt(jnp.mean(x * x, axis=-1, keepdims=True) + EPS)


def _pick_rows(v2, row0, tm, n_lat):
    rows = row0 + lax.broadcasted_iota(jnp.int32, (tm, 1), 0)
    return jnp.where(rows >= n_lat, v2[1:2, :], v2[0:1, :])


def _norm_mod(x, g, sh2, sc2, row0, tm, n_lat):
    y = _rms(x) * g
    return y * (1.0 + _pick_rows(sc2, row0, tm, n_lat)) + _pick_rows(sh2, row0, tm, n_lat)


def _dot(a, b):
    return jnp.dot(a, b, preferred_element_type=F32)


def _dot_nt(a, b):
    return lax.dot_general(a, b, (((1,), (1,)), ((), ())), preferred_element_type=F32)


def _mod_kernel(c_ref, w_ref, b_ref, o_ref):
    a = jax.nn.silu(c_ref[...]).astype(BF16)
    o_ref[...] = _dot(a, w_ref[...].astype(BF16)) + b_ref[...]


def _mod_call(c8, w_mod, b_mod):
    depth, d, n = w_mod.shape
    tn = 1536
    return pl.pallas_call(
        _mod_kernel,
        grid=(depth, n // tn),
        in_specs=[
            pl.BlockSpec((8, d), lambda l, j: (0, 0)),
            pl.BlockSpec((None, d, tn), lambda l, j: (l, 0, j)),
            pl.BlockSpec((None, 1, tn), lambda l, j: (l, 0, j)),
        ],
        out_specs=pl.BlockSpec((None, 8, tn), lambda l, j: (l, 0, j)),
        out_shape=jax.ShapeDtypeStruct((depth, 8, n), F32),
        compiler_params=_params(("parallel", "parallel")),
        name="modulation",
    )(c8, w_mod, b_mod.reshape(depth, 1, n))


def _rope_head(x, cos, sin):
    lane = lax.broadcasted_iota(jnp.int32, x.shape, 1)
    first = (lane & 63) < 32
    partner = jnp.where(first, pltpu.roll(x, 96, 1), pltpu.roll(x, 32, 1))
    return x * cos + partner * sin


def _inproj_kernel(x_ref, g_ref, sh_ref, sc_ref, w_ref, cos_ref, sin_ref, o_ref, *, tm, n_lat):
    i = pl.program_id(0)
    rc = INPROJ_ROW_CHUNK
    heads_per_dot = INPROJ_TN // HEAD_DIM
    for r in range(tm // rc):
        rows = slice(r * rc, (r + 1) * rc)
        h = _norm_mod(x_ref[rows, :], g_ref[...], sh_ref[...], sc_ref[...], i * tm + r * rc, rc, n_lat)
        h = h.astype(BF16)
        for jt in range(IN_COLS // heads_per_dot):
            acc = _dot(h, w_ref[:, jt * INPROJ_TN:(jt + 1) * INPROJ_TN])
            for hd in range(heads_per_dot):
                head = jt * heads_per_dot + hd
                a = acc[:, hd * HEAD_DIM:(hd + 1) * HEAD_DIM]
                if INPROJ_HEAD_KIND[head] == "gelu":
                    a = jax.nn.gelu(a, approximate=True)
                elif INPROJ_HEAD_KIND[head] == "rope":
                    a = _rope_head(a, cos_ref[rows, :], sin_ref[rows, :])
                o_ref[rows, head * HEAD_DIM:(head + 1) * HEAD_DIM] = a.astype(BF16)


INPROJ_HEAD_KIND = (("gelu",) * QB_COL + ("rope",) * B_HEADS + ("none",) * C_HEADS
                    + ("rope",) * B_KV_HEADS + ("none",) * (B_KV_HEADS + 2 * C_HEADS))
INPROJ_TN = 12 * HEAD_DIM
INPROJ_ROW_CHUNK = 384


def _inproj_call(xa, g, sh2, sc2, w_all, layer, cos, sin, *, tm, n_lat):
    t, d = xa.shape
    n = w_all.shape[2]
    assert t % tm == 0 and tm % INPROJ_ROW_CHUNK == 0 and w_all.dtype == BF16
    assert n == IN_COLS * HEAD_DIM and n % INPROJ_TN == 0
    kern = functools.partial(_inproj_kernel, tm=tm, n_lat=n_lat)
    return pl.pallas_call(
        kern,
        grid=(t // tm,),
        in_specs=[
            pl.BlockSpec((tm, d), lambda i: (i, 0)),
            pl.BlockSpec((1, d), lambda i: (0, 0)),
            pl.BlockSpec((2, d), lambda i: (0, 0)),
            pl.BlockSpec((2, d), lambda i: (0, 0)),
            pl.BlockSpec((None, d, n), lambda i: (layer, 0, 0), pipeline_mode=pl.Buffered(1)),
            pl.BlockSpec((tm, HEAD_DIM), lambda i: (i, 0)),
            pl.BlockSpec((tm, HEAD_DIM), lambda i: (i, 0)),
        ],
        out_specs=pl.BlockSpec((tm, n), lambda i: (i, 0)),
        out_shape=jax.ShapeDtypeStruct((t, n), BF16),
        compiler_params=_params(("parallel",)),
        name="in_projection",
    )(xa, g.reshape(1, d), sh2, sc2, w_all, cos, sin)


def _rope_tables(n_lat, n_ctx):
    t = np.arange(n_lat)
    n = HEAD_DIM // 4
    inv = (ROPE_BASE ** (-(2.0 / (HEAD_DIM // 2)) * np.arange(n, dtype=np.float32))).astype(np.float32)
    ang_r = (t // GRID_W).astype(np.float32)[:, None] * inv[None, :]
    ang_c = (t % GRID_W).astype(np.float32)[:, None] * inv[None, :]
    cos = np.concatenate([np.cos(ang_r)] * 2 + [np.cos(ang_c)] * 2, axis=1)
    sin = np.concatenate([-np.sin(ang_r), np.sin(ang_r), -np.sin(ang_c), np.sin(ang_c)], axis=1)
    cos = np.concatenate([cos, np.ones((n_ctx, HEAD_DIM), np.float32)], axis=0)
    sin = np.concatenate([sin, np.zeros((n_ctx, HEAD_DIM), np.float32)], axis=0)
    return jnp.asarray(cos, F32), jnp.asarray(sin, F32)


def _sgu_kernel(u_ref, v_ref, lng_ref, w_ref, b_ref, o_ref, *, chunks):
    for c in range(chunks):
        rows = slice(c * CHUNK, (c + 1) * CHUNK)
        for g in range(A_WIDTH // HEAD_DIM):
            cols = slice(g * HEAD_DIM, (g + 1) * HEAD_DIM)
            v = v_ref[rows, cols].astype(F32)
            mu = jnp.mean(v, axis=-1, keepdims=True)
            var = jnp.mean(jnp.square(v - mu), axis=-1, keepdims=True)
            vn = (v - mu) * lax.rsqrt(var + EPS) * lng_ref[:, cols]
            s = _dot(w_ref[g].astype(BF16), vn.astype(BF16)) + b_ref[g]
            o_ref[rows, cols] = (u_ref[rows, cols].astype(F32) * s).astype(BF16)


def _sgu_call(p, ln_g, w_s, b_s, *, chunks):
    t = p.shape[0]
    rows = chunks * CHUNK
    assert t % rows == 0
    groups = A_WIDTH // HEAD_DIM
    b_full = jnp.broadcast_to(b_s[:, :, None], (groups, CHUNK, HEAD_DIM))
    kern = functools.partial(_sgu_kernel, chunks=chunks)
    return pl.pallas_call(
        kern,
        grid=(t // rows,),
        in_specs=[
            pl.BlockSpec((rows, A_WIDTH), lambda i: (i, 0)),
            pl.BlockSpec((rows, A_WIDTH), lambda i: (i, 1)),
            pl.BlockSpec((1, A_WIDTH), lambda i: (0, 0)),
            pl.BlockSpec((groups, CHUNK, CHUNK), lambda i: (0, 0, 0)),
            pl.BlockSpec((groups, CHUNK, HEAD_DIM), lambda i: (0, 0, 0)),
        ],
        out_specs=pl.BlockSpec((rows, A_WIDTH), lambda i: (i, 0)),
        out_shape=jax.ShapeDtypeStruct((t, A_WIDTH), BF16),
        compiler_params=_params(("parallel",)),
        name="spatial_gating",
    )(p, p, ln_g.reshape(1, A_WIDTH), w_s, b_full)


WIN_Q_BLOCKS = 4


def _softmax_pv(s_loc, s_ctx, sink_col, v_loc, v_ctx):
    n_loc = s_loc.shape[1]
    s = jnp.concatenate([s_loc, s_ctx], axis=1)
    m = jnp.max(s, axis=1, keepdims=True)
    if sink_col is not None:
        m = jnp.maximum(m, sink_col)
    p = jnp.exp(s - m)
    den = jnp.sum(p, axis=1, keepdims=True)
    if sink_col is not None:
        den = den + jnp.exp(sink_col - m)
    p = p.astype(BF16)
    o = _dot(p[:, :n_loc], v_loc) + _dot(p[:, n_loc:], v_ctx)
    return o / den


def _win_kernel(sink_ref, q0, q1, q2, kp, km, kn, vp, vm, vn, kc, vc, o_ref, kcat, vcat, *, n_lat):
    kv = pl.program_id(0)
    s = pl.program_id(1)
    wb = WINDOW_BLOCK
    main = WIN_Q_BLOCKS * wb
    kcat[0:wb] = kp[...]
    kcat[wb:wb + main] = km[...]
    kcat[wb + main:2 * wb + main] = kn[...]
    vcat[0:wb] = vp[...]
    vcat[wb:wb + main] = vm[...]
    vcat[wb + main:2 * wb + main] = vn[...]
    scale = HEAD_DIM ** -0.5
    g = B_HEADS // B_KV_HEADS
    qs = (q0, q1, q2)
    row = lax.broadcasted_iota(jnp.int32, (g * wb, 3 * wb), 0) & (wb - 1)
    col = lax.broadcasted_iota(jnp.int32, (g * wb, 3 * wb), 1)
    rel = col - wb - row
    band_bias = jnp.where(rel < -wb, NEG_INF, jnp.where(rel > wb, NEG_INF, 0.0))
    col1 = lax.broadcasted_iota(jnp.int32, (1, 3 * wb), 1)
    sink_col = jnp.concatenate(
        [jnp.full((wb, 1), sink_ref[kv * g + gi], F32) for gi in range(g)], axis=0)
    n_blocks = n_lat // wb
    for b in range(WIN_Q_BLOCKS):
        n = s * WIN_Q_BLOCKS + b
        rows = slice(b * wb, (b + 1) * wb)
        q3 = jnp.concatenate([qs[gi][rows, :] for gi in range(g)], axis=0)
        keys = kcat[b * wb:(b + 3) * wb, :]
        vals = vcat[b * wb:(b + 3) * wb, :]
        off_start = jnp.where(n == 0, NEG_INF, 0.0)
        off_end = jnp.where(n == n_blocks - 1, NEG_INF, 0.0)
        edge = jnp.where(col1 < wb, off_start, jnp.where(col1 >= 2 * wb, off_end, 0.0))
        s_loc = _dot_nt(q3, keys) * scale + band_bias + edge
        s_ctx = _dot_nt(q3, kc[...]) * scale
        o = _softmax_pv(s_loc, s_ctx, sink_col, vals, vc[...])
        for gi in range(g):
            o_ref[rows, gi * HEAD_DIM:(gi + 1) * HEAD_DIM] = o[gi * wb:(gi + 1) * wb, :].astype(BF16)


def _win_call(p, sink, *, n_lat, n_ctx):
    wb = WINDOW_BLOCK
    main = WIN_Q_BLOCKS * wb
    assert n_lat % main == 0 and n_lat % n_ctx == 0
    nsb = n_lat // main
    nb = n_lat // wb
    g = B_HEADS // B_KV_HEADS
    ctx_blk = n_lat // n_ctx

    def qspec(gi):
        return pl.BlockSpec((main, HEAD_DIM), lambda kv, s: (s, QB_COL + kv * g + gi))

    def band_specs(col0):
        return [
            pl.BlockSpec((wb, HEAD_DIM), lambda kv, s: (jnp.maximum(s * WIN_Q_BLOCKS - 1, 0), col0 + kv)),
            pl.BlockSpec((main, HEAD_DIM), lambda kv, s: (s, col0 + kv)),
            pl.BlockSpec((wb, HEAD_DIM), lambda kv, s: (jnp.minimum((s + 1) * WIN_Q_BLOCKS, nb - 1), col0 + kv)),
        ]

    def ctx_spec(col0):
        return pl.BlockSpec((n_ctx, HEAD_DIM), lambda kv, s: (ctx_blk, col0 + kv))

    kern = functools.partial(_win_kernel, n_lat=n_lat)
    return pl.pallas_call(
        kern,
        grid=(B_KV_HEADS, nsb),
        in_specs=[pl.BlockSpec(memory_space=pltpu.SMEM), qspec(0), qspec(1), qspec(2)]
        + band_specs(KB_COL) + band_specs(VB_COL) + [ctx_spec(KB_COL), ctx_spec(VB_COL)],
        out_specs=pl.BlockSpec((main, g * HEAD_DIM), lambda kv, s: (s, kv)),
        out_shape=jax.ShapeDtypeStruct((n_lat, B_HEADS * HEAD_DIM), BF16),
        scratch_shapes=[pltpu.VMEM((main + 2 * wb, HEAD_DIM), BF16)] * 2,
        compiler_params=_params(("parallel", "parallel")),
        name="window_attention",
    )(sink, p, p, p, p, p, p, p, p, p, p, p)


NA_Q_ROWS = 8
NA_K_ROWS = 16


def _na_row_windows(group, n_rows):
    lead = (NA_K_ROWS - NA_Q_ROWS) // 2
    rq = group * NA_Q_ROWS + np.arange(NA_Q_ROWS)
    rk = group * NA_Q_ROWS - lead + np.arange(NA_K_ROWS)
    r0 = np.clip(rq - NA_ROWS // 2, 0, n_rows - NA_ROWS)
    valid = (rk[None, :] >= r0[:, None]) & (rk[None, :] < r0[:, None] + NA_ROWS)
    roff = rk[None, :] - rq[:, None] + (NA_ROWS - 1)
    return valid, roff


def _na_fill_bias(bt_ref, mb_ref, group, n_rows):
    valid, roff = _na_row_windows(group, n_rows)
    w = GRID_W
    left = lax.broadcasted_iota(jnp.int32, (w, 2 * w), 1) < w
    neg = jnp.full((w, 2 * w), NEG_INF, F32)
    for i in range(NA_Q_ROWS):
        for u in range(0, NA_K_ROWS, 2):
            lo = bt_ref[int(roff[i, u])] if valid[i, u] else neg
            hi = bt_ref[int(roff[i, u + 1])] if valid[i, u + 1] else neg
            blk = neg if not (valid[i, u] or valid[i, u + 1]) else jnp.where(left, lo, hi)
            mb_ref[i * w:(i + 1) * w, u * w:(u + 2) * w] = blk


def _na_kernel(q, kp, km, kn, vp, vm, vn, kc, vc, bt, o, kcat, vcat, mb, *, n_rows):
    a = pl.program_id(1)
    ng = n_rows // NA_Q_ROWS
    for group in sorted({0, min(1, ng - 1), ng - 1}):
        @pl.when(a == group)
        def _(group=group):
            _na_fill_bias(bt, mb, group, n_rows)

    half = (NA_K_ROWS - NA_Q_ROWS) // 2 * GRID_W
    main = NA_Q_ROWS * GRID_W
    kcat[0:half] = kp[...]
    kcat[half:half + main] = km[...]
    kcat[half + main:2 * half + main] = kn[...]
    vcat[0:half] = vp[...]
    vcat[half:half + main] = vm[...]
    vcat[half + main:2 * half + main] = vn[...]
    scale = HEAD_DIM ** -0.5
    qc = NA_Q_ROWS // 2
    for c in range(2):
        rows = slice(c * qc * GRID_W, (c + 1) * qc * GRID_W)
        keys = slice(c * qc * GRID_W, (c * qc + qc + NA_ROWS) * GRID_W)
        qv = q[rows, :]
        s_loc = _dot_nt(qv, kcat[keys, :]) * scale + mb[rows, keys]
        s_ctx = _dot_nt(qv, kc[...]) * scale
        o[rows, :] = _softmax_pv(s_loc, s_ctx, None, vcat[keys, :], vc[...]).astype(BF16)


def _na_col_table(rpb):
    w = GRID_W
    cq = np.arange(w)
    c0 = np.clip(cq - NA_COLS // 2, 0, w - NA_COLS)
    cmask = (cq[None, :] >= c0[:, None]) & (cq[None, :] < c0[:, None] + NA_COLS)
    coff = np.clip(cq[None, :] - cq[:, None] + (NA_COLS - 1), 0, 2 * NA_COLS - 2)
    onehot = (coff[None] == np.arange(2 * NA_COLS - 1)[:, None, None]).astype(np.float32)
    picked = jnp.einsum("hrj,jqk->hrqk", rpb.astype(F32), onehot, precision=lax.Precision.HIGHEST)
    by_col = jnp.where(cmask[None, None], picked, NEG_INF)
    return jnp.concatenate([by_col, by_col], axis=-1)


def _na_call(p, rpb, *, n_lat, n_ctx):
    main = NA_Q_ROWS * GRID_W
    half = (NA_K_ROWS - NA_Q_ROWS) // 2 * GRID_W
    assert n_lat % main == 0 and n_lat % n_ctx == 0 and main == 2 * half
    ng = n_lat // main
    assert ng >= 2
    nhalf = n_lat // half
    ctx_blk = n_lat // n_ctx
    bt = _na_col_table(rpb)
    kern = functools.partial(_na_kernel, n_rows=n_lat // GRID_W)

    def band_specs(col0):
        return [
            pl.BlockSpec((half, HEAD_DIM), lambda h, a: (jnp.maximum(2 * a - 1, 0), col0 + h)),
            pl.BlockSpec((main, HEAD_DIM), lambda h, a: (a, col0 + h)),
            pl.BlockSpec((half, HEAD_DIM), lambda h, a: (jnp.minimum(2 * a + 2, nhalf - 1), col0 + h)),
        ]

    return pl.pallas_call(
        kern,
        grid=(C_HEADS, ng),
        in_specs=[pl.BlockSpec((main, HEAD_DIM), lambda h, a: (a, QC_COL + h))]
        + band_specs(KC_COL) + band_specs(VC_COL)
        + [pl.BlockSpec((n_ctx, HEAD_DIM), lambda h, a: (ctx_blk, KC_COL + h)),
           pl.BlockSpec((n_ctx, HEAD_DIM), lambda h, a: (ctx_blk, VC_COL + h)),
           pl.BlockSpec((None,) + bt.shape[1:], lambda h, a: (h, 0, 0, 0))],
        out_specs=pl.BlockSpec((main, HEAD_DIM), lambda h, a: (a, h)),
        out_shape=jax.ShapeDtypeStruct((n_lat, C_HEADS * HEAD_DIM), BF16),
        scratch_shapes=[pltpu.VMEM((main + 2 * half, HEAD_DIM), BF16)] * 2
        + [pltpu.VMEM((main, NA_K_ROWS * GRID_W), F32)],
        compiler_params=_params(("arbitrary", "arbitrary")),
        name="neighbourhood_attention",
    )(p, p, p, p, p, p, p, p, p, bt)


def _ctx_attn_kernel(sink_ref, q, k, v, o):
    hh = pl.program_id(0)
    scale = HEAD_DIM ** -0.5
    s = _dot_nt(q[...], k[...]) * scale
    sink = jnp.full((s.shape[0], 1), sink_ref[hh], F32)
    m = jnp.maximum(jnp.max(s, axis=1, keepdims=True), sink)
    pr = jnp.exp(s - m)
    den = jnp.sum(pr, axis=1, keepdims=True) + jnp.exp(sink - m)
    o[...] = (_dot(pr.astype(BF16), v[...]) / den).astype(BF16)


def _ctx_attn_call(p, sink12, *, n_lat, n_ctx):
    blk = n_lat // n_ctx
    g = B_HEADS // B_KV_HEADS

    def kcol(hh):
        return jnp.where(hh < B_HEADS, KB_COL + hh // g, KC_COL + hh - B_HEADS)

    def vcol(hh):
        return jnp.where(hh < B_HEADS, VB_COL + hh // g, VC_COL + hh - B_HEADS)

    return pl.pallas_call(
        _ctx_attn_kernel,
        grid=(B_HEADS + C_HEADS,),
        in_specs=[
            pl.BlockSpec(memory_space=pltpu.SMEM),
            pl.BlockSpec((n_ctx, HEAD_DIM), lambda hh: (blk, QB_COL + hh)),
            pl.BlockSpec((n_ctx, HEAD_DIM), lambda hh: (blk, kcol(hh))),
            pl.BlockSpec((n_ctx, HEAD_DIM), lambda hh: (blk, vcol(hh))),
        ],
        out_specs=pl.BlockSpec((n_ctx, HEAD_DIM), lambda hh: (0, hh)),
        out_shape=jax.ShapeDtypeStruct((n_ctx, (B_HEADS + C_HEADS) * HEAD_DIM), BF16),
        compiler_params=_params(("parallel",)),
        name="context_attention",
    )(sink12, p, p, p)


def _outproj_kernel(ya_ref, yb_ref, yc_ref, w_ref, x_ref, g_ref, gate_ref, o_ref, *, tm, n_lat):
    i = pl.program_id(0)
    ka = ya_ref.shape[1]
    kb = yb_ref.shape[1]
    rc = tm // 2
    for c in range(2):
        rows = slice(c * rc, (c + 1) * rc)
        acc = _dot(ya_ref[rows, :], w_ref[0:ka, :])
        acc += _dot(yb_ref[rows, :], w_ref[ka:ka + kb, :])
        acc += _dot(yc_ref[rows, :], w_ref[ka + kb:, :])
        r = _rms(acc) * g_ref[...]
        o_ref[rows, :] = x_ref[rows, :] + _pick_rows(gate_ref[...], i * tm + c * rc, rc, n_lat) * r


def _outproj_call(ya, yb, yc, w_all_bf16, layer, xa, g, gate2, *, rows, tm, n_lat):
    d = xa.shape[1]
    assert rows % tm == 0 and ya.shape[1] + yb.shape[1] + yc.shape[1] == d
    kern = functools.partial(_outproj_kernel, tm=tm, n_lat=n_lat)
    return pl.pallas_call(
        kern,
        grid=(rows // tm,),
        in_specs=[
            pl.BlockSpec((tm, ya.shape[1]), lambda i: (i, 0)),
            pl.BlockSpec((tm, yb.shape[1]), lambda i: (i, 0)),
            pl.BlockSpec((tm, yc.shape[1]), lambda i: (i, 0)),
            pl.BlockSpec((None, d, d), lambda i: (layer, 0, 0)),
            pl.BlockSpec((tm, d), lambda i: (i, 0)),
            pl.BlockSpec((1, d), lambda i: (0, 0)),
            pl.BlockSpec((2, d), lambda i: (0, 0)),
        ],
        out_specs=pl.BlockSpec((tm, d), lambda i: (i, 0)),
        out_shape=jax.ShapeDtypeStruct((rows, d), F32),
        compiler_params=_params(("parallel",)),
        name="out_projection",
    )(ya, yb, yc, w_all_bf16, xa, g.reshape(1, d), gate2)


FFN_TF = 256


def _ffn_weight_specs(d, tf, nf, expert_of, chunk_of=lambda m, f, *pf: f, down_chunk_of=None):
    down_chunk_of = down_chunk_of or chunk_of
    return [
        pl.BlockSpec((None, d, tf), lambda m, f, *pf: (expert_of(m, *pf), 0, chunk_of(m, f, *pf))),
        pl.BlockSpec((None, d, tf), lambda m, f, *pf: (expert_of(m, *pf), 0, nf + chunk_of(m, f, *pf))),
        pl.BlockSpec((None, tf, d), lambda m, f, *pf: (expert_of(m, *pf), down_chunk_of(m, f, *pf), 0)),
    ]


def _ffn_dense_kernel(x_ref, gpre_ref, sh_ref, sc_ref, wg_ref, wu_ref, wd_ref, gpost_ref, gate_ref,
                      o_ref, h_ref, act_ref, *, tm, n_lat):
    m = pl.program_id(0)
    f = pl.program_id(1)
    nf = pl.num_programs(1) - 1

    def gate_up():
        h = h_ref[...]
        gt = _dot(h, wg_ref[...].astype(BF16))
        up = _dot(h, wu_ref[...].astype(BF16))
        return (jax.nn.silu(gt) * up).astype(BF16)

    def down(act):
        o_ref[...] += _dot(act, wd_ref[...].astype(BF16))

    @pl.when(f == 0)
    def _():
        h = _norm_mod(x_ref[...], gpre_ref[...], sh_ref[...], sc_ref[...], m * tm, tm, n_lat)
        h_ref[...] = h.astype(BF16)
        o_ref[...] = jnp.zeros_like(o_ref)
        act_ref[0] = gate_up()

    @pl.when((f > 0) & (f < nf))
    def _():
        prev = act_ref[(f - 1) % 2]
        act_ref[f % 2] = gate_up()
        down(prev)

    @pl.when(f == nf)
    def _():
        down(act_ref[(f - 1) % 2])
        r = _rms(o_ref[...]) * gpost_ref[...]
        o_ref[...] = x_ref[...] + _pick_rows(gate_ref[...], m * tm, tm, n_lat) * r


def _ffn_dense_call(xa, gpre, sh2, sc2, w_gu, w_down, layer_set, gpost, gate2, *, tm, n_lat):
    t, d = xa.shape
    ffn = w_gu.shape[2] // 2
    tf = FFN_TF
    nf = ffn // tf
    assert t % tm == 0 and ffn % tf == 0
    kern = functools.partial(_ffn_dense_kernel, tm=tm, n_lat=n_lat)
    vec = lambda rows: pl.BlockSpec((rows, d), lambda m, f: (0, 0))
    return pl.pallas_call(
        kern,
        grid=(t // tm, nf + 1),
        in_specs=[pl.BlockSpec((tm, d), lambda m, f: (m, 0), pipeline_mode=pl.Buffered(1)),
                  vec(1), vec(2), vec(2)]
        + _ffn_weight_specs(d, tf, nf, lambda m: layer_set,
                            chunk_of=lambda m, f: jnp.minimum(f, nf - 1),
                            down_chunk_of=lambda m, f: jnp.maximum(f - 1, 0)) + [vec(1), vec(2)],
        out_specs=pl.BlockSpec((tm, d), lambda m, f: (m, 0)),
        out_shape=jax.ShapeDtypeStruct((t, d), F32),
        scratch_shapes=[pltpu.VMEM((tm, d), BF16), pltpu.VMEM((2, tm, tf), BF16)],
        compiler_params=_params(("parallel", "arbitrary")),
        name="swiglu_ffn",
    )(xa, gpre.reshape(1, d), sh2, sc2, w_gu, w_gu, w_down, gpost.reshape(1, d), gate2)


MOE_SB = 512
MOE_NSB = 4


DISPATCH_TM = 512


def _dispatch_kernel(pos_ref, pad0_ref, padn_ref, tail_ref, x_ref, gpre_ref, sh_ref, sc_ref, xs_hbm, hbuf, sems,
                     *, tm, sb):
    i = pl.program_id(0)
    last = pl.num_programs(0) - 1
    h = _rms(x_ref[...]) * gpre_ref[...]
    hbuf[i % 2] = h * (1.0 + sc_ref[0:1, :]) + sh_ref[0:1, :]

    def copy(step, t, k):
        p = pos_ref[(step * tm + t) * 2 + k]
        return pltpu.make_async_copy(hbuf.at[step % 2, pl.ds(t, 1)], xs_hbm.at[pl.ds(p, 1)], sems.at[step % 2])

    def issue(t, carry):
        copy(i, t, 0).start()
        copy(i, t, 1).start()
        return carry

    lax.fori_loop(0, tm, issue, 0, unroll=8)

    def drain_step(step):
        def drain(t, carry):
            copy(step, t, 0).wait()
            copy(step, t, 1).wait()
            return carry

        lax.fori_loop(0, tm, drain, 0, unroll=8)

    @pl.when(i > 0)
    def _():
        drain_step(i - 1)

    @pl.when(i == last)
    def _():
        drain_step(i)
        hbuf[0] = jnp.zeros(hbuf.shape[1:], F32)
        def zero_row(e, r):
            return pltpu.make_async_copy(hbuf.at[0, pl.ds(0, 1)], xs_hbm.at[pl.ds(pad0_ref[e] + r, 1)], sems.at[0])

        for e in range(N_EXPERTS):
            lax.fori_loop(0, padn_ref[e], lambda r, c, e=e: (zero_row(e, r).start(), c)[1], 0)
        for e in range(N_EXPERTS):
            lax.fori_loop(0, padn_ref[e], lambda r, c, e=e: (zero_row(e, r).wait(), c)[1], 0)

        def zero_block(g, carry):
            row = pl.multiple_of(tail_ref[0] + g * sb, sb)
            cp = pltpu.make_async_copy(hbuf.at[0, pl.ds(0, sb)], xs_hbm.at[pl.ds(row, sb)], sems.at[0])
            cp.start()
            cp.wait()
            return carry

        lax.fori_loop(0, (xs_hbm.shape[0] - tail_ref[0]) // sb, zero_block, 0)


def _dispatch_call(x, pos, pad0, padn, tail, gpre, sh2, sc2, rows_out):
    t, d = x.shape
    tm = DISPATCH_TM
    assert t % tm == 0 and tm >= MOE_SB
    kern = functools.partial(_dispatch_kernel, tm=tm, sb=MOE_SB)
    vec = lambda r: pl.BlockSpec((r, d), lambda i, *pf: (0, 0))
    grid_spec = pltpu.PrefetchScalarGridSpec(
        num_scalar_prefetch=4,
        grid=(t // tm,),
        in_specs=[pl.BlockSpec((tm, d), lambda i, *pf: (i, 0)), vec(1), vec(2), vec(2)],
        out_specs=pl.BlockSpec(memory_space=pl.ANY),
        scratch_shapes=[pltpu.VMEM((2, tm, d), F32), pltpu.SemaphoreType.DMA((2,))],
    )
    return pl.pallas_call(
        kern,
        grid_spec=grid_spec,
        out_shape=jax.ShapeDtypeStruct((rows_out, d), F32),
        compiler_params=_params(("arbitrary",)),
        name="dispatch",
    )(pos, pad0, padn, tail, x, gpre.reshape(1, d), sh2, sc2)


def _ffn_moe_kernel(te_ref, ns_ref, r0_ref, xs_hbm, wg_ref, wu_ref, wd_ref, o_hbm,
                    stage, h_ref, acc, wgb, wub, wdb, gsem, osem, *, sb, nsb):
    m = pl.program_id(0)
    f = pl.program_id(1)
    n = ns_ref[m]
    n_tiles = pl.num_programs(0)
    cur = m % 2

    def out_copy(mm, s):
        row = pl.multiple_of(r0_ref[mm] + s * sb, sb)
        return pltpu.make_async_copy(acc.at[pl.ds(s * sb, sb)], o_hbm.at[pl.ds(row, sb)], osem)

    def in_copy(mm, s):
        row = pl.multiple_of(r0_ref[mm] + s * sb, sb)
        return pltpu.make_async_copy(xs_hbm.at[pl.ds(row, sb)], stage, gsem)

    def for_valid(count, body):
        for s in range(nsb):
            @pl.when(s < count)
            def _(s=s):
                body(s)

    @pl.when(f == 0)
    def _():
        @pl.when(m == 0)
        def _():
            def load(s):
                in_copy(m, s).start()
                in_copy(m, s).wait()
                h_ref[0, s * sb:(s + 1) * sb, :] = stage[...].astype(BF16)

            for_valid(n, load)

        @pl.when(m > 0)
        def _():
            for_valid(ns_ref[jnp.maximum(m - 1, 0)], lambda s: out_copy(m - 1, s).wait())

        def clear(s):
            acc[s * sb:(s + 1) * sb, :] = jnp.zeros((sb, acc.shape[1]), F32)

        for_valid(n, clear)

    nxt = jnp.minimum(m + 1, n_tiles - 1)
    n_next = jnp.where(m + 1 < n_tiles, ns_ref[nxt], 0)
    s_next = f // 2

    @pl.when((f % 2 == 0) & (s_next < n_next))
    def _():
        in_copy(nxt, s_next).start()

    @pl.when((f % 2 == 1) & (s_next < n_next))
    def _():
        in_copy(nxt, s_next).wait()
        h_ref[1 - cur, pl.ds(pl.multiple_of(s_next * sb, sb), sb), :] = stage[...].astype(BF16)

    def accumulate(subs):
        if 0 in subs:
            wg = wg_ref[...].astype(BF16)
            wu = wu_ref[...].astype(BF16)
            wd = wd_ref[...].astype(BF16)
            wgb[...] = wg
            wub[...] = wu
            wdb[...] = wd
        else:
            wg, wu, wd = wgb[...], wub[...], wdb[...]
        for s in subs:
            h = h_ref[cur, s * sb:(s + 1) * sb, :]
            act = (jax.nn.silu(_dot(h, wg)) * _dot(h, wu)).astype(BF16)
            acc[s * sb:(s + 1) * sb, :] += _dot(act, wd)

    for s0 in range(0, nsb, 2):
        if s0 + 1 < nsb:
            @pl.when(s0 + 1 < n)
            def _(s0=s0):
                accumulate((s0, s0 + 1))

        @pl.when(s0 + 1 == n)
        def _(s0=s0):
            accumulate((s0,))

    @pl.when(f == pl.num_programs(1) - 1)
    def _():
        for_valid(n, lambda s: out_copy(m, s).start())

        @pl.when(m == n_tiles - 1)
        def _():
            for_valid(n, lambda s: out_copy(m, s).wait())
            stage[...] = jnp.zeros(stage.shape, F32)

            def fill(g, carry):
                row = pl.multiple_of(r0_ref[n_tiles] + g * sb, sb)
                cp = pltpu.make_async_copy(stage, o_hbm.at[pl.ds(row, sb)], osem)
                cp.start()
                cp.wait()
                return carry

            lax.fori_loop(0, (o_hbm.shape[0] - r0_ref[n_tiles]) // sb, fill, 0)


def _ffn_moe_call(xs, tile_expert, tile_nsb, tile_row0, w_gu, w_down):
    rows, d = xs.shape
    sb, nsb = MOE_SB, MOE_NSB
    n_tiles = tile_expert.shape[0]
    ffn = w_gu.shape[2] // 2
    tf = FFN_TF
    nf = ffn // tf
    assert rows % sb == 0 and tile_row0.shape[0] == n_tiles + 1 and ffn % tf == 0 and nf >= 2 * nsb
    kern = functools.partial(_ffn_moe_kernel, sb=sb, nsb=nsb)
    grid_spec = pltpu.PrefetchScalarGridSpec(
        num_scalar_prefetch=3,
        grid=(n_tiles, nf),
        in_specs=[pl.BlockSpec(memory_space=pl.ANY)]
        + _ffn_weight_specs(d, tf, nf, lambda m, te, *pf: te[m],
                            lambda m, f, te, ns, *pf: jnp.where(ns[m] > 0, f, nf - 1)),
        out_specs=pl.BlockSpec(memory_space=pl.ANY),
        scratch_shapes=[pltpu.VMEM((sb, d), F32), pltpu.VMEM((2, sb * nsb, d), BF16),
                        pltpu.VMEM((sb * nsb, d), F32),
                        pltpu.VMEM((d, tf), BF16), pltpu.VMEM((d, tf), BF16), pltpu.VMEM((tf, d), BF16),
                        pltpu.SemaphoreType.DMA(()), pltpu.SemaphoreType.DMA(())],
    )
    return pl.pallas_call(
        kern,
        grid_spec=grid_spec,
        out_shape=jax.ShapeDtypeStruct((rows, d), F32),
        compiler_params=_params(("arbitrary", "arbitrary")),
        name="expert_ffn",
    )(tile_expert, tile_nsb, tile_row0, xs, w_gu, w_gu, w_down)


ROUTER_TM = 512
R_E1, R_E2, R_W1, R_W2, R_RANK1, R_RANK2 = range(6)


def _router_kernel(x_ref, gpre_ref, sh_ref, sc_ref, wr_ref, route_ref, count_ref, carry_ref, *, tm):
    i = pl.program_id(0)

    @pl.when(i == 0)
    def _():
        carry_ref[...] = jnp.zeros_like(carry_ref)

    h = _rms(x_ref[...]) * gpre_ref[...]
    h = h * (1.0 + sc_ref[0:1, :]) + sh_ref[0:1, :]
    w = wr_ref[...]
    h_hi = h.astype(BF16)
    h_lo = (h - h_hi.astype(F32)).astype(BF16)
    w_hi = w.astype(BF16)
    w_lo = (w - w_hi.astype(F32)).astype(BF16)
    logits = _dot(h_hi, w_hi) + (_dot(h_hi, w_lo) + _dot(h_lo, w_hi))
    lane_i = lax.broadcasted_iota(jnp.int32, logits.shape, 1)
    lane = lane_i.astype(F32)
    logits = jnp.where(lane_i < N_EXPERTS, logits, -jnp.inf)
    v1 = jnp.max(logits, axis=1, keepdims=True)
    e1 = jnp.min(jnp.where(logits == v1, lane, float(LANES)), axis=1, keepdims=True)
    rest = jnp.where(lane == e1, -jnp.inf, logits)
    v2 = jnp.max(rest, axis=1, keepdims=True)
    e2 = jnp.min(jnp.where(rest == v2, lane, float(LANES)), axis=1, keepdims=True)
    ex = jnp.exp(v2 - v1)
    w1 = 1.0 / (1.0 + ex)
    w2 = ex / (1.0 + ex)
    hit1 = lane == e1
    hit2 = lane == e2
    assign = jnp.where(hit1, 1.0, jnp.where(hit2, 1.0, 0.0))
    r = lax.broadcasted_iota(jnp.int32, (tm, tm), 0)
    c = lax.broadcasted_iota(jnp.int32, (tm, tm), 1)
    before = jnp.where(c < r, 1.0, 0.0).astype(BF16)
    prefix = _dot(before, assign.astype(BF16)) + carry_ref[0:1, :]
    rank1 = jnp.sum(jnp.where(hit1, prefix, 0.0), axis=1, keepdims=True)
    rank2 = jnp.sum(jnp.where(hit2, prefix, 0.0), axis=1, keepdims=True)
    total = carry_ref[0:1, :] + jnp.sum(assign, axis=0, keepdims=True)
    carry_ref[...] = jnp.broadcast_to(total, carry_ref.shape)
    count_ref[...] = jnp.broadcast_to(total, count_ref.shape)
    rec = jnp.zeros(logits.shape, F32)
    for k, val in ((R_E1, e1), (R_E2, e2), (R_W1, w1), (R_W2, w2), (R_RANK1, rank1), (R_RANK2, rank2)):
        rec = jnp.where(lane_i == k, val, rec)
    route_ref[...] = rec


def _router_call(x, gpre, sh2, sc2, w_router):
    t, d = x.shape
    tm = ROUTER_TM
    assert t % tm == 0
    wr = jnp.zeros((d, LANES), F32).at[:, :N_EXPERTS].set(w_router)
    kern = functools.partial(_router_kernel, tm=tm)
    return pl.pallas_call(
        kern,
        grid=(t // tm,),
        in_specs=[
            pl.BlockSpec((tm, d), lambda i: (i, 0)),
            pl.BlockSpec((1, d), lambda i: (0, 0)),
            pl.BlockSpec((2, d), lambda i: (0, 0)),
            pl.BlockSpec((2, d), lambda i: (0, 0)),
            pl.BlockSpec((d, LANES), lambda i: (0, 0)),
        ],
        out_specs=[pl.BlockSpec((tm, LANES), lambda i: (i, 0)),
                   pl.BlockSpec((8, LANES), lambda i: (0, 0))],
        out_shape=[jax.ShapeDtypeStruct((t, LANES), F32), jax.ShapeDtypeStruct((8, LANES), F32)],
        scratch_shapes=[pltpu.VMEM((8, LANES), F32)],
        compiler_params=_params(("arbitrary",)),
        name="router_top2",
    )(x, gpre.reshape(1, d), sh2, sc2, wr)


COMBINE_TM = 256


def _combine_kernel(pos_ref, x_ref, route_ref, gpost_ref, gate_ref, ys_hbm, o_ref, buf, sem, *, tm):
    i = pl.program_id(0)
    slot = i % 2

    def copy(step, t, k):
        p = pos_ref[(step * tm + t) * 2 + k]
        return pltpu.make_async_copy(ys_hbm.at[pl.ds(p, 1)], buf.at[step % 2, k, pl.ds(t, 1)], sem.at[step % 2])

    def fetch(step):
        def issue(t, carry):
            copy(step, t, 0).start()
            copy(step, t, 1).start()
            return carry

        lax.fori_loop(0, tm, issue, 0, unroll=8)

    @pl.when(i == 0)
    def _():
        fetch(i)

    @pl.when(i + 1 < pl.num_programs(0))
    def _():
        fetch(i + 1)

    def drain(t, carry):
        copy(i, t, 0).wait()
        copy(i, t, 1).wait()
        return carry

    lax.fori_loop(0, tm, drain, 0, unroll=8)
    rec = route_ref[...]
    y = rec[:, R_W1:R_W1 + 1] * buf[slot, 0] + rec[:, R_W2:R_W2 + 1] * buf[slot, 1]
    o_ref[...] = x_ref[...] + gate_ref[0:1, :] * (_rms(y) * gpost_ref[...])


def _combine_call(x, ys, pos_flat, route, gpost, gate2):
    t, d = x.shape
    tm = COMBINE_TM
    assert t % tm == 0
    kern = functools.partial(_combine_kernel, tm=tm)
    grid_spec = pltpu.PrefetchScalarGridSpec(
        num_scalar_prefetch=1,
        grid=(t // tm,),
        in_specs=[
            pl.BlockSpec((tm, d), lambda i, pos: (i, 0)),
            pl.BlockSpec((tm, LANES), lambda i, pos: (i, 0)),
            pl.BlockSpec((1, d), lambda i, pos: (0, 0)),
            pl.BlockSpec((2, d), lambda i, pos: (0, 0)),
            pl.BlockSpec(memory_space=pl.ANY),
        ],
        out_specs=pl.BlockSpec((tm, d), lambda i, pos: (i, 0)),
        scratch_shapes=[pltpu.VMEM((2, 2, tm, d), F32), pltpu.SemaphoreType.DMA((2,))],
    )
    return pl.pallas_call(
        kern,
        grid_spec=grid_spec,
        out_shape=jax.ShapeDtypeStruct((t, d), F32),
        compiler_params=_params(("arbitrary",)),
        name="combine",
    )(pos_flat, x, route, gpost.reshape(1, d), gate2, ys)


DENSE_TM = 768
LAT_TM = 1024
DENSE_FFN_TM = 1056


def _moe_plan(route, counts):
    t = route.shape[0]
    e = N_EXPERTS
    sb, nsb = MOE_SB, MOE_NSB
    i32 = jnp.int32
    cnt = counts[0, :e].astype(i32)
    nsub = (cnt + sb - 1) // sb
    row_end = jnp.cumsum(nsub) * sb
    row_start = row_end - nsub * sb
    ntile = (nsub + nsb - 1) // nsb
    tile_end = jnp.cumsum(ntile)
    tile_start = tile_end - ntile
    ex = route[:, R_E1:R_E2 + 1].astype(i32)
    rank = route[:, R_RANK1:R_RANK2 + 1].astype(i32)
    pos = (row_start[ex] + rank).reshape(-1)
    max_sub = (2 * t + e * (sb - 1)) // sb
    max_tiles = (max_sub + e * (nsb - 1)) // nsb
    m = jnp.arange(max_tiles + 1, dtype=i32)
    used = m < tile_end[-1]
    m_used = jnp.minimum(m, tile_end[-1] - 1)
    te = jnp.minimum(jnp.sum((tile_end[None, :] <= m_used[:, None]).astype(i32), axis=1), e - 1)
    local = m_used - tile_start[te]
    per = nsub[te] // jnp.maximum(ntile[te], 1)
    extra = nsub[te] - per * ntile[te]
    tile_nsb = jnp.where(used, per + (local < extra).astype(i32), 0)
    first_sub = local * per + jnp.minimum(local, extra)
    tile_row0 = jnp.where(used, row_start[te] + first_sub * sb, row_end[-1])
    return dict(pos=pos, tile_expert=te[:-1].astype(i32), tile_nsb=tile_nsb[:-1].astype(i32),
                tile_row0=tile_row0.astype(i32), pad_start=(row_start + cnt).astype(i32),
                pad_len=(nsub * sb - cnt).astype(i32), tail=row_end[-1:].astype(i32), rows=max_sub * sb)


def kernel(x, c, ctx, c_ctx, w_mod, b_mod, g_mix_pre, g_mix_post, g_ffn_pre, g_ffn_post, w_in, sgu_ln_g,
           sgu_w, sgu_b, attn_sink, na_rpb, w_out, ffn_w_gu, ffn_w_down, moe_router, moe_w_gu, moe_w_down):
    assert x.shape[0] == 1 and ctx.shape[0] == 1
    n_lat, d = x.shape[1], x.shape[2]
    n_ctx = ctx.shape[1]
    depth = w_mod.shape[0]
    t_all = n_lat + n_ctx

    c8 = jnp.zeros((8, d), F32).at[0].set(c[0]).at[1].set(c_ctx)
    mod = _mod_call(c8, w_mod, b_mod)
    cos, sin = _rope_tables(n_lat, n_ctx)
    w_in_bf16 = w_in.astype(BF16)
    w_out_bf16 = w_out.astype(BF16)
    nb = B_HEADS * HEAD_DIM

    xa = jnp.concatenate([x[0], ctx[0]], axis=0)
    for l in range(depth):
        last = l == depth - 1
        m2 = mod[l, 0:2]
        sh1, sc1, g1, sh2, sc2, g2 = (m2[:, k * d:(k + 1) * d] for k in range(6))

        p = _inproj_call(xa, g_mix_pre[l], sh1, sc1, w_in_bf16, l, cos, sin, tm=DENSE_TM, n_lat=n_lat)
        y_sgu = _sgu_call(p, sgu_ln_g[l], sgu_w[l], sgu_b[l], chunks=DENSE_TM // CHUNK)
        y_win = _win_call(p, attn_sink[l], n_lat=n_lat, n_ctx=n_ctx)
        y_na = _na_call(p, na_rpb[l], n_lat=n_lat, n_ctx=n_ctx)
        if not last:
            sink12 = jnp.concatenate([attn_sink[l], jnp.full((C_HEADS,), NEG_INF, F32)])
            y_ctx = _ctx_attn_call(p, sink12, n_lat=n_lat, n_ctx=n_ctx)
            y_win = jnp.concatenate([y_win, y_ctx[:, :nb]], axis=0)
            y_na = jnp.concatenate([y_na, y_ctx[:, nb:]], axis=0)
            xa = _outproj_call(y_sgu, y_win, y_na, w_out_bf16, l, xa, g_mix_post[l], g1,
                               rows=t_all, tm=DENSE_TM, n_lat=n_lat)
        else:
            xa = _outproj_call(y_sgu, y_win, y_na, w_out_bf16, l, xa, g_mix_post[l], g1,
                               rows=n_lat, tm=LAT_TM, n_lat=n_lat)

        if l % 2 == 0:
            xa = _ffn_dense_call(xa, g_ffn_pre[l], sh2, sc2, ffn_w_gu, ffn_w_down, l // 2,
                                 g_ffn_post[l], g2, tm=DENSE_FFN_TM, n_lat=n_lat)
        else:
            assert last, "expert layers are only supported as the last layer (latent rows only)"
            route, counts = _router_call(xa, g_ffn_pre[l], sh2, sc2, moe_router[l // 2])
            plan = _moe_plan(route, counts)
            xs = _dispatch_call(xa, plan["pos"], plan["pad_start"], plan["pad_len"], plan["tail"],
                                g_ffn_pre[l], sh2, sc2, plan["rows"])
            ys = _ffn_moe_call(xs, plan["tile_expert"], plan["tile_nsb"], plan["tile_row0"],
                               moe_w_gu[l // 2], moe_w_down[l // 2])
            xa = _combine_call(xa, ys, plan["pos"], route, g_ffn_post[l], g2)
    return xa[:n_lat][None]
```

```python
import functools

import numpy as np
import jax
import jax.numpy as jnp
from jax import lax
from jax.experimental import pallas as pl
from jax.experimental.pallas import tpu as pltpu

F32 = jnp.float32
BF16 = jnp.bfloat16

HEAD_DIM = 128
GRID_W = 64
CHUNK = 128
WINDOW_BLOCK = 128
B_HEADS = 6
B_KV_HEADS = 2
C_HEADS = 6
NA_ROWS = 8
NA_COLS = 16
N_EXPERTS = 8
ROPE_BASE = 10000.0
EPS = 1e-6
NEG_INF = -1e30

V7X_VMEM_BYTES = 64 * 1024 * 1024
VMEM_LIMIT = V7X_VMEM_BYTES - 6 * 1024 * 1024
LANES = 128

A_WIDTH = 512
QB_COL = 2 * A_WIDTH // HEAD_DIM
QC_COL = QB_COL + B_HEADS
KB_COL = QC_COL + C_HEADS
VB_COL = KB_COL + B_KV_HEADS
KC_COL = VB_COL + B_KV_HEADS
VC_COL = KC_COL + C_HEADS
IN_COLS = VC_COL + C_HEADS


def _params(sem, vmem=VMEM_LIMIT):
    return pltpu.CompilerParams(dimension_semantics=sem, vmem_limit_bytes=vmem)


def _rms(x):
    return x * lax.rsqrt(jnp.mean(x * x, axis=-1, keepdims=True) + EPS)


def _pick_rows(v2, row0, tm, n_lat):
    rows = row0 + lax.broadcasted_iota(jnp.int32, (tm, 1), 0)
    return jnp.where(rows >= n_lat, v2[1:2, :], v2[0:1, :])


def _norm_mod(x, g, sh2, sc2, row0, tm, n_lat):
    y = _rms(x) * g
    return y * (1.0 + _pick_rows(sc2, row0, tm, n_lat)) + _pick_rows(sh2, row0, tm, n_lat)


def _dot(a, b):
    return jnp.dot(a, b, preferred_element_type=F32)


def _dot_nt(a, b):
    return lax.dot_general(a, b, (((1,), (1,)), ((), ())), preferred_element_type=F32)


def _mod_kernel(c_ref, w_ref, b_ref, o_ref):
    a = jax.nn.silu(c_ref[...]).astype(BF16)
    o_ref[...] = _dot(a, w_ref[...].astype(BF16)) + b_ref[...]


def _mod_call(c8, w_mod, b_mod):
    depth, d, n = w_mod.shape
    tn = 1536
    return pl.pallas_call(
        _mod_kernel,
        grid=(depth, n // tn),
        in_specs=[
            pl.BlockSpec((8, d), lambda l, j: (0, 0)),
            pl.BlockSpec((None, d, tn), lambda l, j: (l, 0, j)),
            pl.BlockSpec((None, 1, tn), lambda l, j: (l, 0, j)),
        ],
        out_specs=pl.BlockSpec((None, 8, tn), lambda l, j: (l, 0, j)),
        out_shape=jax.ShapeDtypeStruct((depth, 8, n), F32),
        compiler_params=_params(("parallel", "parallel")),
        name="modulation",
    )(c8, w_mod, b_mod.reshape(depth, 1, n))


def _rope_head(x, cos, sin):
    lane = lax.broadcasted_iota(jnp.int32, x.shape, 1)
    first = (lane & 63) < 32
    partner = jnp.where(first, pltpu.roll(x, 96, 1), pltpu.roll(x, 32, 1))
    return x * cos + partner * sin


def _inproj_kernel(x_ref, g_ref, sh_ref, sc_ref, w_ref, cos_ref, sin_ref, o_ref, *, tm, n_lat):
    i = pl.program_id(0)
    rc = INPROJ_ROW_CHUNK
    heads_per_dot = INPROJ_TN // HEAD_DIM
    for r in range(tm // rc):
        rows = slice(r * rc, (r + 1) * rc)
        h = _norm_mod(x_ref[rows, :], g_ref[...], sh_ref[...], sc_ref[...], i * tm + r * rc, rc, n_lat)
        h = h.astype(BF16)
        for jt in range(IN_COLS // heads_per_dot):
            acc = _dot(h, w_ref[:, jt * INPROJ_TN:(jt + 1) * INPROJ_TN])
            for hd in range(heads_per_dot):
                head = jt * heads_per_dot + hd
                a = acc[:, hd * HEAD_DIM:(hd + 1) * HEAD_DIM]
                if INPROJ_HEAD_KIND[head] == "gelu":
                    a = jax.nn.gelu(a, approximate=True)
                elif INPROJ_HEAD_KIND[head] == "rope":
                    a = _rope_head(a, cos_ref[rows, :], sin_ref[rows, :])
                o_ref[rows, head * HEAD_DIM:(head + 1) * HEAD_DIM] = a.astype(BF16)


INPROJ_HEAD_KIND = (("gelu",) * QB_COL + ("rope",) * B_HEADS + ("none",) * C_HEADS
                    + ("rope",) * B_KV_HEADS + ("none",) * (B_KV_HEADS + 2 * C_HEADS))
INPROJ_TN = 12 * HEAD_DIM
INPROJ_ROW_CHUNK = 384


def _inproj_call(xa, g, sh2, sc2, w_all, layer, cos, sin, *, tm, n_lat):
    t, d = xa.shape
    n = w_all.shape[2]
    assert t % tm == 0 and tm % INPROJ_ROW_CHUNK == 0 and w_all.dtype == BF16
    assert n == IN_COLS * HEAD_DIM and n % INPROJ_TN == 0
    kern = functools.partial(_inproj_kernel, tm=tm, n_lat=n_lat)
    return pl.pallas_call(
        kern,
        grid=(t // tm,),
        in_specs=[
            pl.BlockSpec((tm, d), lambda i: (i, 0)),
            pl.BlockSpec((1, d), lambda i: (0, 0)),
            pl.BlockSpec((2, d), lambda i: (0, 0)),
            pl.BlockSpec((2, d), lambda i: (0, 0)),
            pl.BlockSpec((None, d, n), lambda i: (layer, 0, 0), pipeline_mode=pl.Buffered(1)),
            pl.BlockSpec((tm, HEAD_DIM), lambda i: (i, 0)),
            pl.BlockSpec((tm, HEAD_DIM), lambda i: (i, 0)),
        ],
        out_specs=pl.BlockSpec((tm, n), lambda i: (i, 0)),
        out_shape=jax.ShapeDtypeStruct((t, n), BF16),
        compiler_params=_params(("parallel",)),
        name="in_projection",
    )(xa, g.reshape(1, d), sh2, sc2, w_all, cos, sin)


def _rope_tables(n_lat, n_ctx):
    t = np.arange(n_lat)
    n = HEAD_DIM // 4
    inv = (ROPE_BASE ** (-(2.0 / (HEAD_DIM // 2)) * np.arange(n, dtype=np.float32))).astype(np.float32)
    ang_r = (t // GRID_W).astype(np.float32)[:, None] * inv[None, :]
    ang_c = (t % GRID_W).astype(np.float32)[:, None] * inv[None, :]
    cos = np.concatenate([np.cos(ang_r)] * 2 + [np.cos(ang_c)] * 2, axis=1)
    sin = np.concatenate([-np.sin(ang_r), np.sin(ang_r), -np.sin(ang_c), np.sin(ang_c)], axis=1)
    cos = np.concatenate([cos, np.ones((n_ctx, HEAD_DIM), np.float32)], axis=0)
    sin = np.concatenate([sin, np.zeros((n_ctx, HEAD_DIM), np.float32)], axis=0)
    return jnp.asarray(cos, F32), jnp.asarray(sin, F32)


def _sgu_kernel(u_ref, v_ref, lng_ref, w_ref, b_ref, o_ref, *, chunks):
    for c in range(chunks):
        rows = slice(c * CHUNK, (c + 1) * CHUNK)
        for g in range(A_WIDTH // HEAD_DIM):
            cols = slice(g * HEAD_DIM, (g + 1) * HEAD_DIM)
            v = v_ref[rows, cols].astype(F32)
            mu = jnp.mean(v, axis=-1, keepdims=True)
            var = jnp.mean(jnp.square(v - mu), axis=-1, keepdims=True)
            vn = (v - mu) * lax.rsqrt(var + EPS) * lng_ref[:, cols]
            s = _dot(w_ref[g].astype(BF16), vn.astype(BF16)) + b_ref[g]
            o_ref[rows, cols] = (u_ref[rows, cols].astype(F32) * s).astype(BF16)


def _sgu_call(p, ln_g, w_s, b_s, *, chunks):
    t = p.shape[0]
    rows = chunks * CHUNK
    assert t % rows == 0
    groups = A_WIDTH // HEAD_DIM
    b_full = jnp.broadcast_to(b_s[:, :, None], (groups, CHUNK, HEAD_DIM))
    kern = functools.partial(_sgu_kernel, chunks=chunks)
    return pl.pallas_call(
        kern,
        grid=(t // rows,),
        in_specs=[
            pl.BlockSpec((rows, A_WIDTH), lambda i: (i, 0)),
            pl.BlockSpec((rows, A_WIDTH), lambda i: (i, 1)),
            pl.BlockSpec((1, A_WIDTH), lambda i: (0, 0)),
            pl.BlockSpec((groups, CHUNK, CHUNK), lambda i: (0, 0, 0)),
            pl.BlockSpec((groups, CHUNK, HEAD_DIM), lambda i: (0, 0, 0)),
        ],
        out_specs=pl.BlockSpec((rows, A_WIDTH), lambda i: (i, 0)),
        out_shape=jax.ShapeDtypeStruct((t, A_WIDTH), BF16),
        compiler_params=_params(("parallel",)),
        name="spatial_gating",
    )(p, p, ln_g.reshape(1, A_WIDTH), w_s, b_full)


WIN_Q_BLOCKS = 4


def _softmax_pv(s_loc, s_ctx, sink_col, v_loc, v_ctx):
    n_loc = s_loc.shape[1]
    s = jnp.concatenate([s_loc, s_ctx], axis=1)
    m = jnp.max(s, axis=1, keepdims=True)
    if sink_col is not None:
        m = jnp.maximum(m, sink_col)
    p = jnp.exp(s - m)
    den = jnp.sum(p, axis=1, keepdims=True)
    if sink_col is not None:
        den = den + jnp.exp(sink_col - m)
    p = p.astype(BF16)
    o = _dot(p[:, :n_loc], v_loc) + _dot(p[:, n_loc:], v_ctx)
    return o / den


def _win_kernel(sink_ref, q0, q1, q2, kp, km, kn, vp, vm, vn, kc, vc, o_ref, kcat, vcat, *, n_lat):
    kv = pl.program_id(0)
    s = pl.program_id(1)
    wb = WINDOW_BLOCK
    main = WIN_Q_BLOCKS * wb
    kcat[0:wb] = kp[...]
    kcat[wb:wb + main] = km[...]
    kcat[wb + main:2 * wb + main] = kn[...]
    vcat[0:wb] = vp[...]
    vcat[wb:wb + main] = vm[...]
    vcat[wb + main:2 * wb + main] = vn[...]
    scale = HEAD_DIM ** -0.5
    g = B_HEADS // B_KV_HEADS
    qs = (q0, q1, q2)
    row = lax.broadcasted_iota(jnp.int32, (g * wb, 3 * wb), 0) & (wb - 1)
    col = lax.broadcasted_iota(jnp.int32, (g * wb, 3 * wb), 1)
    rel = col - wb - row
    band_bias = jnp.where(rel < -wb, NEG_INF, jnp.where(rel > wb, NEG_INF, 0.0))
    col1 = lax.broadcasted_iota(jnp.int32, (1, 3 * wb), 1)
    sink_col = jnp.concatenate(
        [jnp.full((wb, 1), sink_ref[kv * g + gi], F32) for gi in range(g)], axis=0)
    n_blocks = n_lat // wb
    for b in range(WIN_Q_BLOCKS):
        n = s * WIN_Q_BLOCKS + b
        rows = slice(b * wb, (b + 1) * wb)
        q3 = jnp.concatenate([qs[gi][rows, :] for gi in range(g)], axis=0)
        keys = kcat[b * wb:(b + 3) * wb, :]
        vals = vcat[b * wb:(b + 3) * wb, :]
        off_start = jnp.where(n == 0, NEG_INF, 0.0)
        off_end = jnp.where(n == n_blocks - 1, NEG_INF, 0.0)
        edge = jnp.where(col1 < wb, off_start, jnp.where(col1 >= 2 * wb, off_end, 0.0))
        s_loc = _dot_nt(q3, keys) * scale + band_bias + edge
        s_ctx = _dot_nt(q3, kc[...]) * scale
        o = _softmax_pv(s_loc, s_ctx, sink_col, vals, vc[...])
        for gi in range(g):
            o_ref[rows, gi * HEAD_DIM:(gi + 1) * HEAD_DIM] = o[gi * wb:(gi + 1) * wb, :].astype(BF16)


def _win_call(p, sink, *, n_lat, n_ctx):
    wb = WINDOW_BLOCK
    main = WIN_Q_BLOCKS * wb
    assert n_lat % main == 0 and n_lat % n_ctx == 0
    nsb = n_lat // main
    nb = n_lat // wb
    g = B_HEADS // B_KV_HEADS
    ctx_blk = n_lat // n_ctx

    def qspec(gi):
        return pl.BlockSpec((main, HEAD_DIM), lambda kv, s: (s, QB_COL + kv * g + gi))

    def band_specs(col0):
        return [
            pl.BlockSpec((wb, HEAD_DIM), lambda kv, s: (jnp.maximum(s * WIN_Q_BLOCKS - 1, 0), col0 + kv)),
            pl.BlockSpec((main, HEAD_DIM), lambda kv, s: (s, col0 + kv)),
            pl.BlockSpec((wb, HEAD_DIM), lambda kv, s: (jnp.minimum((s + 1) * WIN_Q_BLOCKS, nb - 1), col0 + kv)),
        ]

    def ctx_spec(col0):
        return pl.BlockSpec((n_ctx, HEAD_DIM), lambda kv, s: (ctx_blk, col0 + kv))

    kern = functools.partial(_win_kernel, n_lat=n_lat)
    return pl.pallas_call(
        kern,
        grid=(B_KV_HEADS, nsb),
        in_specs=[pl.BlockSpec(memory_space=pltpu.SMEM), qspec(0), qspec(1), qspec(2)]
        + band_specs(KB_COL) + band_specs(VB_COL) + [ctx_spec(KB_COL), ctx_spec(VB_COL)],
        out_specs=pl.BlockSpec((main, g * HEAD_DIM), lambda kv, s: (s, kv)),
        out_shape=jax.ShapeDtypeStruct((n_lat, B_HEADS * HEAD_DIM), BF16),
        scratch_shapes=[pltpu.VMEM((main + 2 * wb, HEAD_DIM), BF16)] * 2,
        compiler_params=_params(("parallel", "parallel")),
        name="window_attention",
    )(sink, p, p, p, p, p, p, p, p, p, p, p)


NA_Q_ROWS = 8
NA_K_ROWS = 16


def _na_row_windows(group, n_rows):
    lead = (NA_K_ROWS - NA_Q_ROWS) // 2
    rq = group * NA_Q_ROWS + np.arange(NA_Q_ROWS)
    rk = group * NA_Q_ROWS - lead + np.arange(NA_K_ROWS)
    r0 = np.clip(rq - NA_ROWS // 2, 0, n_rows - NA_ROWS)
    valid = (rk[None, :] >= r0[:, None]) & (rk[None, :] < r0[:, None] + NA_ROWS)
    roff = rk[None, :] - rq[:, None] + (NA_ROWS - 1)
    return valid, roff


def _na_fill_bias(bt_ref, mb_ref, group, n_rows):
    valid, roff = _na_row_windows(group, n_rows)
    w = GRID_W
    left = lax.broadcasted_iota(jnp.int32, (w, 2 * w), 1) < w
    neg = jnp.full((w, 2 * w), NEG_INF, F32)
    for i in range(NA_Q_ROWS):
        for u in range(0, NA_K_ROWS, 2):
            lo = bt_ref[int(roff[i, u])] if valid[i, u] else neg
            hi = bt_ref[int(roff[i, u + 1])] if valid[i, u + 1] else neg
            blk = neg if not (valid[i, u] or valid[i, u + 1]) else jnp.where(left, lo, hi)
            mb_ref[i * w:(i + 1) * w, u * w:(u + 2) * w] = blk


def _na_kernel(q, kp, km, kn, vp, vm, vn, kc, vc, bt, o, kcat, vcat, mb, *, n_rows):
    a = pl.program_id(1)
    ng = n_rows // NA_Q_ROWS
    for group in sorted({0, min(1, ng - 1), ng - 1}):
        @pl.when(a == group)
        def _(group=group):
            _na_fill_bias(bt, mb, group, n_rows)

    half = (NA_K_ROWS - NA_Q_ROWS) // 2 * GRID_W
    main = NA_Q_ROWS * GRID_W
    kcat[0:half] = kp[...]
    kcat[half:half + main] = km[...]
    kcat[half + main:2 * half + main] = kn[...]
    vcat[0:half] = vp[...]
    vcat[half:half + main] = vm[...]
    vcat[half + main:2 * half + main] = vn[...]
    scale = HEAD_DIM ** -0.5
    qc = NA_Q_ROWS // 2
    for c in range(2):
        rows = slice(c * qc * GRID_W, (c + 1) * qc * GRID_W)
        keys = slice(c * qc * GRID_W, (c * qc + qc + NA_ROWS) * GRID_W)
        qv = q[rows, :]
        s_loc = _dot_nt(qv, kcat[keys, :]) * scale + mb[rows, keys]
        s_ctx = _dot_nt(qv, kc[...]) * scale
        o[rows, :] = _softmax_pv(s_loc, s_ctx, None, vcat[keys, :], vc[...]).astype(BF16)


def _na_col_table(rpb):
    w = GRID_W
    cq = np.arange(w)
    c0 = np.clip(cq - NA_COLS // 2, 0, w - NA_COLS)
    cmask = (cq[None, :] >= c0[:, None]) & (cq[None, :] < c0[:, None] + NA_COLS)
    coff = np.clip(cq[None, :] - cq[:, None] + (NA_COLS - 1), 0, 2 * NA_COLS - 2)
    onehot = (coff[None] == np.arange(2 * NA_COLS - 1)[:, None, None]).astype(np.float32)
    picked = jnp.einsum("hrj,jqk->hrqk", rpb.astype(F32), onehot, precision=lax.Precision.HIGHEST)
    by_col = jnp.where(cmask[None, None], picked, NEG_INF)
    return jnp.concatenate([by_col, by_col], axis=-1)


def _na_call(p, rpb, *, n_lat, n_ctx):
    main = NA_Q_ROWS * GRID_W
    half = (NA_K_ROWS - NA_Q_ROWS) // 2 * GRID_W
    assert n_lat % main == 0 and n_lat % n_ctx == 0 and main == 2 * half
    ng = n_lat // main
    assert ng >= 2
    nhalf = n_lat // half
    ctx_blk = n_lat // n_ctx
    bt = _na_col_table(rpb)
    kern = functools.partial(_na_kernel, n_rows=n_lat // GRID_W)

    def band_specs(col0):
        return [
            pl.BlockSpec((half, HEAD_DIM), lambda h, a: (jnp.maximum(2 * a - 1, 0), col0 + h)),
            pl.BlockSpec((main, HEAD_DIM), lambda h, a: (a, col0 + h)),
            pl.BlockSpec((half, HEAD_DIM), lambda h, a: (jnp.minimum(2 * a + 2, nhalf - 1), col0 + h)),
        ]

    return pl.pallas_call(
        kern,
        grid=(C_HEADS, ng),
        in_specs=[pl.BlockSpec((main, HEAD_DIM), lambda h, a: (a, QC_COL + h))]
        + band_specs(KC_COL) + band_specs(VC_COL)
        + [pl.BlockSpec((n_ctx, HEAD_DIM), lambda h, a: (ctx_blk, KC_COL + h)),
           pl.BlockSpec((n_ctx, HEAD_DIM), lambda h, a: (ctx_blk, VC_COL + h)),
           pl.BlockSpec((None,) + bt.shape[1:], lambda h, a: (h, 0, 0, 0))],
        out_specs=pl.BlockSpec((main, HEAD_DIM), lambda h, a: (a, h)),
        out_shape=jax.ShapeDtypeStruct((n_lat, C_HEADS * HEAD_DIM), BF16),
        scratch_shapes=[pltpu.VMEM((main + 2 * half, HEAD_DIM), BF16)] * 2
        + [pltpu.VMEM((main, NA_K_ROWS * GRID_W), F32)],
        compiler_params=_params(("arbitrary", "arbitrary")),
        name="neighbourhood_attention",
    )(p, p, p, p, p, p, p, p, p, bt)


def _ctx_attn_kernel(sink_ref, q, k, v, o):
    hh = pl.program_id(0)
    scale = HEAD_DIM ** -0.5
    s = _dot_nt(q[...], k[...]) * scale
    sink = jnp.full((s.shape[0], 1), sink_ref[hh], F32)
    m = jnp.maximum(jnp.max(s, axis=1, keepdims=True), sink)
    pr = jnp.exp(s - m)
    den = jnp.sum(pr, axis=1, keepdims=True) + jnp.exp(sink - m)
    o[...] = (_dot(pr.astype(BF16), v[...]) / den).astype(BF16)


def _ctx_attn_call(p, sink12, *, n_lat, n_ctx):
    blk = n_lat // n_ctx
    g = B_HEADS // B_KV_HEADS

    def kcol(hh):
        return jnp.where(hh < B_HEADS, KB_COL + hh // g, KC_COL + hh - B_HEADS)

    def vcol(hh):
        return jnp.where(hh < B_HEADS, VB_COL + hh // g, VC_COL + hh - B_HEADS)

    return pl.pallas_call(
        _ctx_attn_kernel,
        grid=(B_HEADS + C_HEADS,),
        in_specs=[
            pl.BlockSpec(memory_space=pltpu.SMEM),
            pl.BlockSpec((n_ctx, HEAD_DIM), lambda hh: (blk, QB_COL + hh)),
            pl.BlockSpec((n_ctx, HEAD_DIM), lambda hh: (blk, kcol(hh))),
            pl.BlockSpec((n_ctx, HEAD_DIM), lambda hh: (blk, vcol(hh))),
        ],
        out_specs=pl.BlockSpec((n_ctx, HEAD_DIM), lambda hh: (0, hh)),
        out_shape=jax.ShapeDtypeStruct((n_ctx, (B_HEADS + C_HEADS) * HEAD_DIM), BF16),
        compiler_params=_params(("parallel",)),
        name="context_attention",
    )(sink12, p, p, p)


def _outproj_kernel(ya_ref, yb_ref, yc_ref, w_ref, x_ref, g_ref, gate_ref, o_ref, *, tm, n_lat):
    i = pl.program_id(0)
    ka = ya_ref.shape[1]
    kb = yb_ref.shape[1]
    rc = tm // 2
    for c in range(2):
        rows = slice(c * rc, (c + 1) * rc)
        acc = _dot(ya_ref[rows, :], w_ref[0:ka, :])
        acc += _dot(yb_ref[rows, :], w_ref[ka:ka + kb, :])
        acc += _dot(yc_ref[rows, :], w_ref[ka + kb:, :])
        r = _rms(acc) * g_ref[...]
        o_ref[rows, :] = x_ref[rows, :] + _pick_rows(gate_ref[...], i * tm + c * rc, rc, n_lat) * r


def _outproj_call(ya, yb, yc, w_all_bf16, layer, xa, g, gate2, *, rows, tm, n_lat):
    d = xa.shape[1]
    assert rows % tm == 0 and ya.shape[1] + yb.shape[1] + yc.shape[1] == d
    kern = functools.partial(_outproj_kernel, tm=tm, n_lat=n_lat)
    return pl.pallas_call(
        kern,
        grid=(rows // tm,),
        in_specs=[
            pl.BlockSpec((tm, ya.shape[1]), lambda i: (i, 0)),
            pl.BlockSpec((tm, yb.shape[1]), lambda i: (i, 0)),
            pl.BlockSpec((tm, yc.shape[1]), lambda i: (i, 0)),
            pl.BlockSpec((None, d, d), lambda i: (layer, 0, 0)),
            pl.BlockSpec((tm, d), lambda i: (i, 0)),
            pl.BlockSpec((1, d), lambda i: (0, 0)),
            pl.BlockSpec((2, d), lambda i: (0, 0)),
        ],
        out_specs=pl.BlockSpec((tm, d), lambda i: (i, 0)),
        out_shape=jax.ShapeDtypeStruct((rows, d), F32),
        compiler_params=_params(("parallel",)),
        name="out_projection",
    )(ya, yb, yc, w_all_bf16, xa, g.reshape(1, d), gate2)


FFN_TF = 256


def _ffn_weight_specs(d, tf, nf, expert_of, chunk_of=lambda m, f, *pf: f, down_chunk_of=None):
    down_chunk_of = down_chunk_of or chunk_of
    return [
        pl.BlockSpec((None, d, tf), lambda m, f, *pf: (expert_of(m, *pf), 0, chunk_of(m, f, *pf))),
        pl.BlockSpec((None, d, tf), lambda m, f, *pf: (expert_of(m, *pf), 0, nf + chunk_of(m, f, *pf))),
        pl.BlockSpec((None, tf, d), lambda m, f, *pf: (expert_of(m, *pf), down_chunk_of(m, f, *pf), 0)),
    ]


def _ffn_dense_kernel(x_ref, gpre_ref, sh_ref, sc_ref, wg_ref, wu_ref, wd_ref, gpost_ref, gate_ref,
                      o_ref, h_ref, act_ref, *, tm, n_lat):
    m = pl.program_id(0)
    f = pl.program_id(1)
    nf = pl.num_programs(1) - 1

    def gate_up():
        h = h_ref[...]
        gt = _dot(h, wg_ref[...].astype(BF16))
        up = _dot(h, wu_ref[...].astype(BF16))
        return (jax.nn.silu(gt) * up).astype(BF16)

    def down(act):
        o_ref[...] += _dot(act, wd_ref[...].astype(BF16))

    @pl.when(f == 0)
    def _():
        h = _norm_mod(x_ref[...], gpre_ref[...], sh_ref[...], sc_ref[...], m * tm, tm, n_lat)
        h_ref[...] = h.astype(BF16)
        o_ref[...] = jnp.zeros_like(o_ref)
        act_ref[0] = gate_up()

    @pl.when((f > 0) & (f < nf))
    def _():
        prev = act_ref[(f - 1) % 2]
        act_ref[f % 2] = gate_up()
        down(prev)

    @pl.when(f == nf)
    def _():
        down(act_ref[(f - 1) % 2])
        r = _rms(o_ref[...]) * gpost_ref[...]
        o_ref[...] = x_ref[...] + _pick_rows(gate_ref[...], m * tm, tm, n_lat) * r


def _ffn_dense_call(xa, gpre, sh2, sc2, w_gu, w_down, layer_set, gpost, gate2, *, tm, n_lat):
    t, d = xa.shape
    ffn = w_gu.shape[2] // 2
    tf = FFN_TF
    nf = ffn // tf
    assert t % tm == 0 and ffn % tf == 0
    kern = functools.partial(_ffn_dense_kernel, tm=tm, n_lat=n_lat)
    vec = lambda rows: pl.BlockSpec((rows, d), lambda m, f: (0, 0))
    return pl.pallas_call(
        kern,
        grid=(t // tm, nf + 1),
        in_specs=[pl.BlockSpec((tm, d), lambda m, f: (m, 0), pipeline_mode=pl.Buffered(1)),
                  vec(1), vec(2), vec(2)]
        + _ffn_weight_specs(d, tf, nf, lambda m: layer_set,
                            chunk_of=lambda m, f: jnp.minimum(f, nf - 1),
                            down_chunk_of=lambda m, f: jnp.maximum(f - 1, 0)) + [vec(1), vec(2)],
        out_specs=pl.BlockSpec((tm, d), lambda m, f: (m, 0)),
        out_shape=jax.ShapeDtypeStruct((t, d), F32),
        scratch_shapes=[pltpu.VMEM((tm, d), BF16), pltpu.VMEM((2, tm, tf), BF16)],
        compiler_params=_params(("parallel", "arbitrary")),
        name="swiglu_ffn",
    )(xa, gpre.reshape(1, d), sh2, sc2, w_gu, w_gu, w_down, gpost.reshape(1, d), gate2)


MOE_SB = 512
MOE_NSB = 4


DISPATCH_TM = 512


def _dispatch_kernel(pos_ref, pad0_ref, padn_ref, tail_ref, x_ref, gpre_ref, sh_ref, sc_ref, xs_hbm, hbuf, sems,
                     *, tm, sb):
    i = pl.program_id(0)
    last = pl.num_programs(0) - 1
    h = _rms(x_ref[...]) * gpre_ref[...]
    hbuf[i % 2] = h * (1.0 + sc_ref[0:1, :]) + sh_ref[0:1, :]

    def copy(step, t, k):
        p = pos_ref[(step * tm + t) * 2 + k]
        return pltpu.make_async_copy(hbuf.at[step % 2, pl.ds(t, 1)], xs_hbm.at[pl.ds(p, 1)], sems.at[step % 2])

    def issue(t, carry):
        copy(i, t, 0).start()
        copy(i, t, 1).start()
        return carry

    lax.fori_loop(0, tm, issue, 0, unroll=8)

    def drain_step(step):
        def drain(t, carry):
            copy(step, t, 0).wait()
            copy(step, t, 1).wait()
            return carry

        lax.fori_loop(0, tm, drain, 0, unroll=8)

    @pl.when(i > 0)
    def _():
        drain_step(i - 1)

    @pl.when(i == last)
    def _():
        drain_step(i)
        hbuf[0] = jnp.zeros(hbuf.shape[1:], F32)
        def zero_row(e, r):
            return pltpu.make_async_copy(hbuf.at[0, pl.ds(0, 1)], xs_hbm.at[pl.ds(pad0_ref[e] + r, 1)], sems.at[0])

        for e in range(N_EXPERTS):
            lax.fori_loop(0, padn_ref[e], lambda r, c, e=e: (zero_row(e, r).start(), c)[1], 0)
        for e in range(N_EXPERTS):
            lax.fori_loop(0, padn_ref[e], lambda r, c, e=e: (zero_row(e, r).wait(), c)[1], 0)

        def zero_block(g, carry):
            row = pl.multiple_of(tail_ref[0] + g * sb, sb)
            cp = pltpu.make_async_copy(hbuf.at[0, pl.ds(0, sb)], xs_hbm.at[pl.ds(row, sb)], sems.at[0])
            cp.start()
            cp.wait()
            return carry

        lax.fori_loop(0, (xs_hbm.shape[0] - tail_ref[0]) // sb, zero_block, 0)


def _dispatch_call(x, pos, pad0, padn, tail, gpre, sh2, sc2, rows_out):
    t, d = x.shape
    tm = DISPATCH_TM
    assert t % tm == 0 and tm >= MOE_SB
    kern = functools.partial(_dispatch_kernel, tm=tm, sb=MOE_SB)
    vec = lambda r: pl.BlockSpec((r, d), lambda i, *pf: (0, 0))
    grid_spec = pltpu.PrefetchScalarGridSpec(
        num_scalar_prefetch=4,
        grid=(t // tm,),
        in_specs=[pl.BlockSpec((tm, d), lambda i, *pf: (i, 0)), vec(1), vec(2), vec(2)],
        out_specs=pl.BlockSpec(memory_space=pl.ANY),
        scratch_shapes=[pltpu.VMEM((2, tm, d), F32), pltpu.SemaphoreType.DMA((2,))],
    )
    return pl.pallas_call(
        kern,
        grid_spec=grid_spec,
        out_shape=jax.ShapeDtypeStruct((rows_out, d), F32),
        compiler_params=_params(("arbitrary",)),
        name="dispatch",
    )(pos, pad0, padn, tail, x, gpre.reshape(1, d), sh2, sc2)


def _ffn_moe_kernel(te_ref, ns_ref, r0_ref, xs_hbm, *refs, sb, nsb, n_lat=None):
    fused = n_lat is not None
    if fused:
        gpre_ref, sh_ref, sc_ref, gpost_ref, gate_ref = refs[:5]
        refs = refs[5:]
    wg_ref, wu_ref, wd_ref, o_hbm, stage, h_ref, acc, wgb, wub, wdb, gsem, osem = refs
    m = pl.program_id(0)
    f = pl.program_id(1)
    n = ns_ref[m]
    n_tiles = pl.num_programs(0)
    cur = m % 2

    def staged(mm, s):
        if not fused:
            return stage[...].astype(BF16)
        row0 = r0_ref[mm] + s * sb
        return _norm_mod(stage[...], gpre_ref[...], sh_ref[...], sc_ref[...], row0, sb, n_lat).astype(BF16)

    def out_copy(mm, s):
        row = pl.multiple_of(r0_ref[mm] + s * sb, sb)
        return pltpu.make_async_copy(acc.at[pl.ds(s * sb, sb)], o_hbm.at[pl.ds(row, sb)], osem)

    def in_copy(mm, s):
        row = pl.multiple_of(r0_ref[mm] + s * sb, sb)
        return pltpu.make_async_copy(xs_hbm.at[pl.ds(row, sb)], stage, gsem)

    def for_valid(count, body):
        for s in range(nsb):
            @pl.when(s < count)
            def _(s=s):
                body(s)

    @pl.when(f == 0)
    def _():
        @pl.when(m == 0)
        def _():
            def load(s):
                in_copy(m, s).start()
                in_copy(m, s).wait()
                h_ref[0, s * sb:(s + 1) * sb, :] = staged(m, s)

            for_valid(n, load)

        @pl.when(m > 0)
        def _():
            for_valid(ns_ref[jnp.maximum(m - 1, 0)], lambda s: out_copy(m - 1, s).wait())

        def clear(s):
            acc[s * sb:(s + 1) * sb, :] = jnp.zeros((sb, acc.shape[1]), F32)

        for_valid(n, clear)

    nxt = jnp.minimum(m + 1, n_tiles - 1)
    n_next = jnp.where(m + 1 < n_tiles, ns_ref[nxt], 0)
    s_next = f // 2

    @pl.when((f % 2 == 0) & (s_next < n_next))
    def _():
        in_copy(nxt, s_next).start()

    @pl.when((f % 2 == 1) & (s_next < n_next))
    def _():
        in_copy(nxt, s_next).wait()
        h_ref[1 - cur, pl.ds(pl.multiple_of(s_next * sb, sb), sb), :] = staged(nxt, s_next)

    def accumulate(subs):
        if 0 in subs:
            wg = wg_ref[...].astype(BF16)
            wu = wu_ref[...].astype(BF16)
            wd = wd_ref[...].astype(BF16)
            wgb[...] = wg
            wub[...] = wu
            wdb[...] = wd
        else:
            wg, wu, wd = wgb[...], wub[...], wdb[...]
        for s in subs:
            h = h_ref[cur, s * sb:(s + 1) * sb, :]
            act = (jax.nn.silu(_dot(h, wg)) * _dot(h, wu)).astype(BF16)
            acc[s * sb:(s + 1) * sb, :] += _dot(act, wd)

    for s0 in range(0, nsb, 2):
        if s0 + 1 < nsb:
            @pl.when(s0 + 1 < n)
            def _(s0=s0):
                accumulate((s0, s0 + 1))

        @pl.when(s0 + 1 == n)
        def _(s0=s0):
            accumulate((s0,))

    @pl.when(f == pl.num_programs(1) - 1)
    def _():
        def finish(s):
            if fused:
                in_copy(m, s).start()
                in_copy(m, s).wait()
                rows = slice(s * sb, (s + 1) * sb)
                r = _rms(acc[rows, :]) * gpost_ref[...]
                acc[rows, :] = stage[...] + _pick_rows(gate_ref[...], r0_ref[m] + s * sb, sb, n_lat) * r
            out_copy(m, s).start()

        for_valid(n, finish)

        @pl.when(m == n_tiles - 1)
        def _():
            for_valid(n, lambda s: out_copy(m, s).wait())
            stage[...] = jnp.zeros(stage.shape, F32)

            def fill(g, carry):
                row = pl.multiple_of(r0_ref[n_tiles] + g * sb, sb)
                cp = pltpu.make_async_copy(stage, o_hbm.at[pl.ds(row, sb)], osem)
                cp.start()
                cp.wait()
                return carry

            lax.fori_loop(0, (o_hbm.shape[0] - r0_ref[n_tiles]) // sb, fill, 0)


def _ffn_moe_call(xs, tile_expert, tile_nsb, tile_row0, w_gu, w_down, *, sb, nsb, fused=None):
    rows, d = xs.shape
    n_tiles = tile_expert.shape[0]
    ffn = w_gu.shape[2] // 2
    tf = FFN_TF
    nf = ffn // tf
    assert rows % sb == 0 and tile_row0.shape[0] == n_tiles + 1 and ffn % tf == 0 and nf >= 2 * nsb
    vec = lambda r: pl.BlockSpec((r, d), lambda m, f, *pf: (0, 0))
    extra_specs, extra_args, n_lat = [], [], None
    if fused is not None:
        gpre, sh2, sc2, gpost, gate2, n_lat = fused
        extra_specs = [vec(1), vec(2), vec(2), vec(1), vec(2)]
        extra_args = [gpre.reshape(1, d), sh2, sc2, gpost.reshape(1, d), gate2]
    kern = functools.partial(_ffn_moe_kernel, sb=sb, nsb=nsb, n_lat=n_lat)
    grid_spec = pltpu.PrefetchScalarGridSpec(
        num_scalar_prefetch=3,
        grid=(n_tiles, nf),
        in_specs=[pl.BlockSpec(memory_space=pl.ANY)] + extra_specs
        + _ffn_weight_specs(d, tf, nf, lambda m, te, *pf: te[m],
                            lambda m, f, te, ns, *pf: jnp.where(ns[m] > 0, f, nf - 1)),
        out_specs=pl.BlockSpec(memory_space=pl.ANY),
        scratch_shapes=[pltpu.VMEM((sb, d), F32), pltpu.VMEM((2, sb * nsb, d), BF16),
                        pltpu.VMEM((sb * nsb, d), F32),
                        pltpu.VMEM((d, tf), BF16), pltpu.VMEM((d, tf), BF16), pltpu.VMEM((tf, d), BF16),
                        pltpu.SemaphoreType.DMA(()), pltpu.SemaphoreType.DMA(())],
    )
    return pl.pallas_call(
        kern,
        grid_spec=grid_spec,
        out_shape=jax.ShapeDtypeStruct((rows, d), F32),
        compiler_params=_params(("arbitrary", "arbitrary")),
        name="expert_ffn" if fused is None else "tiled_ffn",
    )(tile_expert, tile_nsb, tile_row0, xs, *extra_args, w_gu, w_gu, w_down)


ROUTER_TM = 512
R_E1, R_E2, R_W1, R_W2, R_RANK1, R_RANK2 = range(6)


def _router_kernel(x_ref, gpre_ref, sh_ref, sc_ref, wr_ref, route_ref, count_ref, carry_ref, *, tm):
    i = pl.program_id(0)

    @pl.when(i == 0)
    def _():
        carry_ref[...] = jnp.zeros_like(carry_ref)

    h = _rms(x_ref[...]) * gpre_ref[...]
    h = h * (1.0 + sc_ref[0:1, :]) + sh_ref[0:1, :]
    w = wr_ref[...]
    h_hi = h.astype(BF16)
    h_lo = (h - h_hi.astype(F32)).astype(BF16)
    w_hi = w.astype(BF16)
    w_lo = (w - w_hi.astype(F32)).astype(BF16)
    logits = _dot(h_hi, w_hi) + (_dot(h_hi, w_lo) + _dot(h_lo, w_hi))
    lane_i = lax.broadcasted_iota(jnp.int32, logits.shape, 1)
    lane = lane_i.astype(F32)
    logits = jnp.where(lane_i < N_EXPERTS, logits, -jnp.inf)
    v1 = jnp.max(logits, axis=1, keepdims=True)
    e1 = jnp.min(jnp.where(logits == v1, lane, float(LANES)), axis=1, keepdims=True)
    rest = jnp.where(lane == e1, -jnp.inf, logits)
    v2 = jnp.max(rest, axis=1, keepdims=True)
    e2 = jnp.min(jnp.where(rest == v2, lane, float(LANES)), axis=1, keepdims=True)
    ex = jnp.exp(v2 - v1)
    w1 = 1.0 / (1.0 + ex)
    w2 = ex / (1.0 + ex)
    hit1 = lane == e1
    hit2 = lane == e2
    assign = jnp.where(hit1, 1.0, jnp.where(hit2, 1.0, 0.0))
    r = lax.broadcasted_iota(jnp.int32, (tm, tm), 0)
    c = lax.broadcasted_iota(jnp.int32, (tm, tm), 1)
    before = jnp.where(c < r, 1.0, 0.0).astype(BF16)
    prefix = _dot(before, assign.astype(BF16)) + carry_ref[0:1, :]
    rank1 = jnp.sum(jnp.where(hit1, prefix, 0.0), axis=1, keepdims=True)
    rank2 = jnp.sum(jnp.where(hit2, prefix, 0.0), axis=1, keepdims=True)
    total = carry_ref[0:1, :] + jnp.sum(assign, axis=0, keepdims=True)
    carry_ref[...] = jnp.broadcast_to(total, carry_ref.shape)
    count_ref[...] = jnp.broadcast_to(total, count_ref.shape)
    rec = jnp.zeros(logits.shape, F32)
    for k, val in ((R_E1, e1), (R_E2, e2), (R_W1, w1), (R_W2, w2), (R_RANK1, rank1), (R_RANK2, rank2)):
        rec = jnp.where(lane_i == k, val, rec)
    route_ref[...] = rec


def _router_call(x, gpre, sh2, sc2, w_router):
    t, d = x.shape
    tm = ROUTER_TM
    assert t % tm == 0
    wr = jnp.zeros((d, LANES), F32).at[:, :N_EXPERTS].set(w_router)
    kern = functools.partial(_router_kernel, tm=tm)
    return pl.pallas_call(
        kern,
        grid=(t // tm,),
        in_specs=[
            pl.BlockSpec((tm, d), lambda i: (i, 0)),
            pl.BlockSpec((1, d), lambda i: (0, 0)),
            pl.BlockSpec((2, d), lambda i: (0, 0)),
            pl.BlockSpec((2, d), lambda i: (0, 0)),
            pl.BlockSpec((d, LANES), lambda i: (0, 0)),
        ],
        out_specs=[pl.BlockSpec((tm, LANES), lambda i: (i, 0)),
                   pl.BlockSpec((8, LANES), lambda i: (0, 0))],
        out_shape=[jax.ShapeDtypeStruct((t, LANES), F32), jax.ShapeDtypeStruct((8, LANES), F32)],
        scratch_shapes=[pltpu.VMEM((8, LANES), F32)],
        compiler_params=_params(("arbitrary",)),
        name="router_top2",
    )(x, gpre.reshape(1, d), sh2, sc2, wr)


COMBINE_TM = 256


def _combine_kernel(pos_ref, x_ref, route_ref, gpost_ref, gate_ref, ys_hbm, o_ref, buf, sem, *, tm):
    i = pl.program_id(0)
    slot = i % 2

    def copy(step, t, k):
        p = pos_ref[(step * tm + t) * 2 + k]
        return pltpu.make_async_copy(ys_hbm.at[pl.ds(p, 1)], buf.at[step % 2, k, pl.ds(t, 1)], sem.at[step % 2])

    def fetch(step):
        def issue(t, carry):
            copy(step, t, 0).start()
            copy(step, t, 1).start()
            return carry

        lax.fori_loop(0, tm, issue, 0, unroll=8)

    @pl.when(i == 0)
    def _():
        fetch(i)

    @pl.when(i + 1 < pl.num_programs(0))
    def _():
        fetch(i + 1)

    def drain(t, carry):
        copy(i, t, 0).wait()
        copy(i, t, 1).wait()
        return carry

    lax.fori_loop(0, tm, drain, 0, unroll=8)
    rec = route_ref[...]
    y = rec[:, R_W1:R_W1 + 1] * buf[slot, 0] + rec[:, R_W2:R_W2 + 1] * buf[slot, 1]
    o_ref[...] = x_ref[...] + gate_ref[0:1, :] * (_rms(y) * gpost_ref[...])


def _combine_call(x, ys, pos_flat, route, gpost, gate2):
    t, d = x.shape
    tm = COMBINE_TM
    assert t % tm == 0
    kern = functools.partial(_combine_kernel, tm=tm)
    grid_spec = pltpu.PrefetchScalarGridSpec(
        num_scalar_prefetch=1,
        grid=(t // tm,),
        in_specs=[
            pl.BlockSpec((tm, d), lambda i, pos: (i, 0)),
            pl.BlockSpec((tm, LANES), lambda i, pos: (i, 0)),
            pl.BlockSpec((1, d), lambda i, pos: (0, 0)),
            pl.BlockSpec((2, d), lambda i, pos: (0, 0)),
            pl.BlockSpec(memory_space=pl.ANY),
        ],
        out_specs=pl.BlockSpec((tm, d), lambda i, pos: (i, 0)),
        scratch_shapes=[pltpu.VMEM((2, 2, tm, d), F32), pltpu.SemaphoreType.DMA((2,))],
    )
    return pl.pallas_call(
        kern,
        grid_spec=grid_spec,
        out_shape=jax.ShapeDtypeStruct((t, d), F32),
        compiler_params=_params(("arbitrary",)),
        name="combine",
    )(pos_flat, x, route, gpost.reshape(1, d), gate2, ys)


DENSE_TM = 768
LAT_TM = 1024
DENSE_FFN_SB = 704
DENSE_FFN_NSB = 2


def _moe_plan(route, counts):
    t = route.shape[0]
    e = N_EXPERTS
    sb, nsb = MOE_SB, MOE_NSB
    i32 = jnp.int32
    cnt = counts[0, :e].astype(i32)
    nsub = (cnt + sb - 1) // sb
    row_end = jnp.cumsum(nsub) * sb
    row_start = row_end - nsub * sb
    ntile = (nsub + nsb - 1) // nsb
    tile_end = jnp.cumsum(ntile)
    tile_start = tile_end - ntile
    ex = route[:, R_E1:R_E2 + 1].astype(i32)
    rank = route[:, R_RANK1:R_RANK2 + 1].astype(i32)
    pos = (row_start[ex] + rank).reshape(-1)
    max_sub = (2 * t + e * (sb - 1)) // sb
    max_tiles = (max_sub + e * (nsb - 1)) // nsb
    m = jnp.arange(max_tiles + 1, dtype=i32)
    used = m < tile_end[-1]
    m_used = jnp.minimum(m, tile_end[-1] - 1)
    te = jnp.minimum(jnp.sum((tile_end[None, :] <= m_used[:, None]).astype(i32), axis=1), e - 1)
    local = m_used - tile_start[te]
    per = nsub[te] // jnp.maximum(ntile[te], 1)
    extra = nsub[te] - per * ntile[te]
    tile_nsb = jnp.where(used, per + (local < extra).astype(i32), 0)
    first_sub = local * per + jnp.minimum(local, extra)
    tile_row0 = jnp.where(used, row_start[te] + first_sub * sb, row_end[-1])
    return dict(pos=pos, tile_expert=te[:-1].astype(i32), tile_nsb=tile_nsb[:-1].astype(i32),
                tile_row0=tile_row0.astype(i32), pad_start=(row_start + cnt).astype(i32),
                pad_len=(nsub * sb - cnt).astype(i32), tail=row_end[-1:].astype(i32), rows=max_sub * sb)


def kernel(x, c, ctx, c_ctx, w_mod, b_mod, g_mix_pre, g_mix_post, g_ffn_pre, g_ffn_post, w_in, sgu_ln_g,
           sgu_w, sgu_b, attn_sink, na_rpb, w_out, ffn_w_gu, ffn_w_down, moe_router, moe_w_gu, moe_w_down):
    assert x.shape[0] == 1 and ctx.shape[0] == 1
    n_lat, d = x.shape[1], x.shape[2]
    n_ctx = ctx.shape[1]
    depth = w_mod.shape[0]
    t_all = n_lat + n_ctx

    c8 = jnp.zeros((8, d), F32).at[0].set(c[0]).at[1].set(c_ctx)
    mod = _mod_call(c8, w_mod, b_mod)
    cos, sin = _rope_tables(n_lat, n_ctx)
    w_in_bf16 = w_in.astype(BF16)
    w_out_bf16 = w_out.astype(BF16)
    nb = B_HEADS * HEAD_DIM

    xa = jnp.concatenate([x[0], ctx[0]], axis=0)
    for l in range(depth):
        last = l == depth - 1
        m2 = mod[l, 0:2]
        sh1, sc1, g1, sh2, sc2, g2 = (m2[:, k * d:(k + 1) * d] for k in range(6))

        p = _inproj_call(xa, g_mix_pre[l], sh1, sc1, w_in_bf16, l, cos, sin, tm=DENSE_TM, n_lat=n_lat)
        y_sgu = _sgu_call(p, sgu_ln_g[l], sgu_w[l], sgu_b[l], chunks=DENSE_TM // CHUNK)
        y_win = _win_call(p, attn_sink[l], n_lat=n_lat, n_ctx=n_ctx)
        y_na = _na_call(p, na_rpb[l], n_lat=n_lat, n_ctx=n_ctx)
        if not last:
            sink12 = jnp.concatenate([attn_sink[l], jnp.full((C_HEADS,), NEG_INF, F32)])
            y_ctx = _ctx_attn_call(p, sink12, n_lat=n_lat, n_ctx=n_ctx)
            y_win = jnp.concatenate([y_win, y_ctx[:, :nb]], axis=0)
            y_na = jnp.concatenate([y_na, y_ctx[:, nb:]], axis=0)
            xa = _outproj_call(y_sgu, y_win, y_na, w_out_bf16, l, xa, g_mix_post[l], g1,
                               rows=t_all, tm=DENSE_TM, n_lat=n_lat)
        else:
            xa = _outproj_call(y_sgu, y_win, y_na, w_out_bf16, l, xa, g_mix_post[l], g1,
                               rows=n_lat, tm=LAT_TM, n_lat=n_lat)

        if l % 2 == 0:
            rows = xa.shape[0]
            tile_rows = DENSE_FFN_SB * DENSE_FFN_NSB
            assert rows % tile_rows == 0
            n_tiles = rows // tile_rows
            xa = _ffn_moe_call(xa, jnp.full((n_tiles,), l // 2, jnp.int32),
                               jnp.full((n_tiles,), DENSE_FFN_NSB, jnp.int32),
                               jnp.arange(n_tiles + 1, dtype=jnp.int32) * tile_rows,
                               ffn_w_gu, ffn_w_down, sb=DENSE_FFN_SB, nsb=DENSE_FFN_NSB,
                               fused=(g_ffn_pre[l], sh2, sc2, g_ffn_post[l], g2, n_lat))
        else:
            assert last, "expert layers are only supported as the last layer (latent rows only)"
            route, counts = _router_call(xa, g_ffn_pre[l], sh2, sc2, moe_router[l // 2])
            plan = _moe_plan(route, counts)
            xs = _dispatch_call(xa, plan["pos"], plan["pad_start"], plan["pad_len"], plan["tail"],
                                g_ffn_pre[l], sh2, sc2, plan["rows"])
            ys = _ffn_moe_call(xs, plan["tile_expert"], plan["tile_nsb"], plan["tile_row0"],
                               moe_w_gu[l // 2], moe_w_down[l // 2], sb=MOE_SB, nsb=MOE_NSB)
            xa = _combine_call(xa, ys, plan["pos"], route, g_ffn_post[l], g2)
    return xa[:n_lat][None]
```

```python
import functools

import numpy as np
import jax
import jax.numpy as jnp
from jax import lax
from jax.experimental import pallas as pl
from jax.experimental.pallas import tpu as pltpu

F32 = jnp.float32
BF16 = jnp.bfloat16

HEAD_DIM = 128
GRID_W = 64
CHUNK = 128
WINDOW_BLOCK = 128
B_HEADS = 6
B_KV_HEADS = 2
C_HEADS = 6
NA_ROWS = 8
NA_COLS = 16
N_EXPERTS = 8
ROPE_BASE = 10000.0
EPS = 1e-6
NEG_INF = -1e30

V7X_VMEM_BYTES = 64 * 1024 * 1024
VMEM_LIMIT = V7X_VMEM_BYTES - 6 * 1024 * 1024
LANES = 128

A_WIDTH = 512
QB_COL = 2 * A_WIDTH // HEAD_DIM
QC_COL = QB_COL + B_HEADS
KB_COL = QC_COL + C_HEADS
VB_COL = KB_COL + B_KV_HEADS
KC_COL = VB_COL + B_KV_HEADS
VC_COL = KC_COL + C_HEADS
IN_COLS = VC_COL + C_HEADS


def _params(sem, vmem=VMEM_LIMIT):
    return pltpu.CompilerParams(dimension_semantics=sem, vmem_limit_bytes=vmem)


def _rms(x):
    return x * lax.rsqrt(jnp.mean(x * x, axis=-1, keepdims=True) + EPS)


def _pick_rows(v2, row0, tm, n_lat):
    rows = row0 + lax.broadcasted_iota(jnp.int32, (tm, 1), 0)
    return jnp.where(rows >= n_lat, v2[1:2, :], v2[0:1, :])


def _norm_mod(x, g, sh2, sc2, row0, tm, n_lat):
    y = _rms(x) * g
    return y * (1.0 + _pick_rows(sc2, row0, tm, n_lat)) + _pick_rows(sh2, row0, tm, n_lat)


def _dot(a, b):
    return jnp.dot(a, b, preferred_element_type=F32)


def _dot_nt(a, b):
    return lax.dot_general(a, b, (((1,), (1,)), ((), ())), preferred_element_type=F32)


def _mod_kernel(c_ref, w_ref, b_ref, o_ref):
    a = jax.nn.silu(c_ref[...]).astype(BF16)
    o_ref[...] = _dot(a, w_ref[...].astype(BF16)) + b_ref[...]


def _mod_call(c8, w_mod, b_mod):
    depth, d, n = w_mod.shape
    tn = 1536
    return pl.pallas_call(
        _mod_kernel,
        grid=(depth, n // tn),
        in_specs=[
            pl.BlockSpec((8, d), lambda l, j: (0, 0)),
            pl.BlockSpec((None, d, tn), lambda l, j: (l, 0, j)),
            pl.BlockSpec((None, 1, tn), lambda l, j: (l, 0, j)),
        ],
        out_specs=pl.BlockSpec((None, 8, tn), lambda l, j: (l, 0, j)),
        out_shape=jax.ShapeDtypeStruct((depth, 8, n), F32),
        compiler_params=_params(("parallel", "parallel")),
        name="modulation",
    )(c8, w_mod, b_mod.reshape(depth, 1, n))


def _rope_head(x, cos, sin):
    lane = lax.broadcasted_iota(jnp.int32, x.shape, 1)
    first = (lane & 63) < 32
    partner = jnp.where(first, pltpu.roll(x, 96, 1), pltpu.roll(x, 32, 1))
    return x * cos + partner * sin


def _inproj_kernel(x_ref, g_ref, sh_ref, sc_ref, w_ref, cos_ref, sin_ref, o_ref, *, tm, n_lat):
    i = pl.program_id(0)
    rc = INPROJ_ROW_CHUNK
    heads_per_dot = INPROJ_TN // HEAD_DIM
    for r in range(tm // rc):
        rows = slice(r * rc, (r + 1) * rc)
        h = _norm_mod(x_ref[rows, :], g_ref[...], sh_ref[...], sc_ref[...], i * tm + r * rc, rc, n_lat)
        h = h.astype(BF16)
        for jt in range(IN_COLS // heads_per_dot):
            acc = _dot(h, w_ref[:, jt * INPROJ_TN:(jt + 1) * INPROJ_TN])
            for hd in range(heads_per_dot):
                head = jt * heads_per_dot + hd
                a = acc[:, hd * HEAD_DIM:(hd + 1) * HEAD_DIM]
                if INPROJ_HEAD_KIND[head] == "gelu":
                    a = jax.nn.gelu(a, approximate=True)
                elif INPROJ_HEAD_KIND[head] == "rope":
                    a = _rope_head(a, cos_ref[rows, :], sin_ref[rows, :])
                o_ref[rows, head * HEAD_DIM:(head + 1) * HEAD_DIM] = a.astype(BF16)


INPROJ_HEAD_KIND = (("gelu",) * QB_COL + ("rope",) * B_HEADS + ("none",) * C_HEADS
                    + ("rope",) * B_KV_HEADS + ("none",) * (B_KV_HEADS + 2 * C_HEADS))
INPROJ_TN = 12 * HEAD_DIM
INPROJ_ROW_CHUNK = 384


def _inproj_call(xa, g, sh2, sc2, w_all, layer, cos, sin, *, tm, n_lat):
    t, d = xa.shape
    n = w_all.shape[2]
    assert t % tm == 0 and tm % INPROJ_ROW_CHUNK == 0 and w_all.dtype == BF16
    assert n == IN_COLS * HEAD_DIM and n % INPROJ_TN == 0
    kern = functools.partial(_inproj_kernel, tm=tm, n_lat=n_lat)
    return pl.pallas_call(
        kern,
        grid=(t // tm,),
        in_specs=[
            pl.BlockSpec((tm, d), lambda i: (i, 0)),
            pl.BlockSpec((1, d), lambda i: (0, 0)),
            pl.BlockSpec((2, d), lambda i: (0, 0)),
            pl.BlockSpec((2, d), lambda i: (0, 0)),
            pl.BlockSpec((None, d, n), lambda i: (layer, 0, 0), pipeline_mode=pl.Buffered(1)),
            pl.BlockSpec((tm, HEAD_DIM), lambda i: (i, 0)),
            pl.BlockSpec((tm, HEAD_DIM), lambda i: (i, 0)),
        ],
        out_specs=pl.BlockSpec((tm, n), lambda i: (i, 0)),
        out_shape=jax.ShapeDtypeStruct((t, n), BF16),
        compiler_params=_params(("parallel",)),
        name="in_projection",
    )(xa, g.reshape(1, d), sh2, sc2, w_all, cos, sin)


def _rope_tables(n_lat, n_ctx):
    t = np.arange(n_lat)
    n = HEAD_DIM // 4
    inv = (ROPE_BASE ** (-(2.0 / (HEAD_DIM // 2)) * np.arange(n, dtype=np.float32))).astype(np.float32)
    ang_r = (t // GRID_W).astype(np.float32)[:, None] * inv[None, :]
    ang_c = (t % GRID_W).astype(np.float32)[:, None] * inv[None, :]
    cos = np.concatenate([np.cos(ang_r)] * 2 + [np.cos(ang_c)] * 2, axis=1)
    sin = np.concatenate([-np.sin(ang_r), np.sin(ang_r), -np.sin(ang_c), np.sin(ang_c)], axis=1)
    cos = np.concatenate([cos, np.ones((n_ctx, HEAD_DIM), np.float32)], axis=0)
    sin = np.concatenate([sin, np.zeros((n_ctx, HEAD_DIM), np.float32)], axis=0)
    return jnp.asarray(cos, F32), jnp.asarray(sin, F32)


def _sgu_kernel(u_ref, v_ref, lng_ref, w_ref, b_ref, o_ref, *, chunks):
    for c in range(chunks):
        rows = slice(c * CHUNK, (c + 1) * CHUNK)
        for g in range(A_WIDTH // HEAD_DIM):
            cols = slice(g * HEAD_DIM, (g + 1) * HEAD_DIM)
            v = v_ref[rows, cols].astype(F32)
            mu = jnp.mean(v, axis=-1, keepdims=True)
            var = jnp.mean(jnp.square(v - mu), axis=-1, keepdims=True)
            vn = (v - mu) * lax.rsqrt(var + EPS) * lng_ref[:, cols]
            s = _dot(w_ref[g].astype(BF16), vn.astype(BF16)) + b_ref[g]
            o_ref[rows, cols] = (u_ref[rows, cols].astype(F32) * s).astype(BF16)


def _sgu_call(p, ln_g, w_s, b_s, *, chunks):
    t = p.shape[0]
    rows = chunks * CHUNK
    assert t % rows == 0
    groups = A_WIDTH // HEAD_DIM
    b_full = jnp.broadcast_to(b_s[:, :, None], (groups, CHUNK, HEAD_DIM))
    kern = functools.partial(_sgu_kernel, chunks=chunks)
    return pl.pallas_call(
        kern,
        grid=(t // rows,),
        in_specs=[
            pl.BlockSpec((rows, A_WIDTH), lambda i: (i, 0)),
            pl.BlockSpec((rows, A_WIDTH), lambda i: (i, 1)),
            pl.BlockSpec((1, A_WIDTH), lambda i: (0, 0)),
            pl.BlockSpec((groups, CHUNK, CHUNK), lambda i: (0, 0, 0)),
            pl.BlockSpec((groups, CHUNK, HEAD_DIM), lambda i: (0, 0, 0)),
        ],
        out_specs=pl.BlockSpec((rows, A_WIDTH), lambda i: (i, 0)),
        out_shape=jax.ShapeDtypeStruct((t, A_WIDTH), BF16),
        compiler_params=_params(("parallel",)),
        name="spatial_gating",
    )(p, p, ln_g.reshape(1, A_WIDTH), w_s, b_full)


WIN_Q_BLOCKS = 4


def _softmax_pv(s_loc, s_ctx, sink_col, v_loc, v_ctx):
    n_loc = s_loc.shape[1]
    s = jnp.concatenate([s_loc, s_ctx], axis=1)
    m = jnp.max(s, axis=1, keepdims=True)
    if sink_col is not None:
        m = jnp.maximum(m, sink_col)
    p = jnp.exp(s - m)
    den = jnp.sum(p, axis=1, keepdims=True)
    if sink_col is not None:
        den = den + jnp.exp(sink_col - m)
    p = p.astype(BF16)
    o = _dot(p[:, :n_loc], v_loc) + _dot(p[:, n_loc:], v_ctx)
    return o / den


def _win_kernel(sink_ref, q0, q1, q2, kp, km, kn, vp, vm, vn, kc, vc, o_ref, kcat, vcat, *, n_lat):
    kv = pl.program_id(0)
    s = pl.program_id(1)
    wb = WINDOW_BLOCK
    main = WIN_Q_BLOCKS * wb
    kcat[0:wb] = kp[...]
    kcat[wb:wb + main] = km[...]
    kcat[wb + main:2 * wb + main] = kn[...]
    vcat[0:wb] = vp[...]
    vcat[wb:wb + main] = vm[...]
    vcat[wb + main:2 * wb + main] = vn[...]
    scale = HEAD_DIM ** -0.5
    g = B_HEADS // B_KV_HEADS
    qs = (q0, q1, q2)
    row = lax.broadcasted_iota(jnp.int32, (g * wb, 3 * wb), 0) & (wb - 1)
    col = lax.broadcasted_iota(jnp.int32, (g * wb, 3 * wb), 1)
    rel = col - wb - row
    band_bias = jnp.where(rel < -wb, NEG_INF, jnp.where(rel > wb, NEG_INF, 0.0))
    col1 = lax.broadcasted_iota(jnp.int32, (1, 3 * wb), 1)
    sink_col = jnp.concatenate(
        [jnp.full((wb, 1), sink_ref[kv * g + gi], F32) for gi in range(g)], axis=0)
    n_blocks = n_lat // wb
    for b in range(WIN_Q_BLOCKS):
        n = s * WIN_Q_BLOCKS + b
        rows = slice(b * wb, (b + 1) * wb)
        q3 = jnp.concatenate([qs[gi][rows, :] for gi in range(g)], axis=0)
        keys = kcat[b * wb:(b + 3) * wb, :]
        vals = vcat[b * wb:(b + 3) * wb, :]
        off_start = jnp.where(n == 0, NEG_INF, 0.0)
        off_end = jnp.where(n == n_blocks - 1, NEG_INF, 0.0)
        edge = jnp.where(col1 < wb, off_start, jnp.where(col1 >= 2 * wb, off_end, 0.0))
        s_loc = _dot_nt(q3, keys) * scale + band_bias + edge
        s_ctx = _dot_nt(q3, kc[...]) * scale
        o = _softmax_pv(s_loc, s_ctx, sink_col, vals, vc[...])
        for gi in range(g):
            o_ref[rows, gi * HEAD_DIM:(gi + 1) * HEAD_DIM] = o[gi * wb:(gi + 1) * wb, :].astype(BF16)


def _win_call(p, sink, *, n_lat, n_ctx):
    wb = WINDOW_BLOCK
    main = WIN_Q_BLOCKS * wb
    assert n_lat % main == 0 and n_lat % n_ctx == 0
    nsb = n_lat // main
    nb = n_lat // wb
    g = B_HEADS // B_KV_HEADS
    ctx_blk = n_lat // n_ctx

    def qspec(gi):
        return pl.BlockSpec((main, HEAD_DIM), lambda kv, s: (s, QB_COL + kv * g + gi))

    def band_specs(col0):
        return [
            pl.BlockSpec((wb, HEAD_DIM), lambda kv, s: (jnp.maximum(s * WIN_Q_BLOCKS - 1, 0), col0 + kv)),
            pl.BlockSpec((main, HEAD_DIM), lambda kv, s: (s, col0 + kv)),
            pl.BlockSpec((wb, HEAD_DIM), lambda kv, s: (jnp.minimum((s + 1) * WIN_Q_BLOCKS, nb - 1), col0 + kv)),
        ]

    def ctx_spec(col0):
        return pl.BlockSpec((n_ctx, HEAD_DIM), lambda kv, s: (ctx_blk, col0 + kv))

    kern = functools.partial(_win_kernel, n_lat=n_lat)
    return pl.pallas_call(
        kern,
        grid=(B_KV_HEADS, nsb),
        in_specs=[pl.BlockSpec(memory_space=pltpu.SMEM), qspec(0), qspec(1), qspec(2)]
        + band_specs(KB_COL) + band_specs(VB_COL) + [ctx_spec(KB_COL), ctx_spec(VB_COL)],
        out_specs=pl.BlockSpec((main, g * HEAD_DIM), lambda kv, s: (s, kv)),
        out_shape=jax.ShapeDtypeStruct((n_lat, B_HEADS * HEAD_DIM), BF16),
        scratch_shapes=[pltpu.VMEM((main + 2 * wb, HEAD_DIM), BF16)] * 2,
        compiler_params=_params(("parallel", "parallel")),
        name="window_attention",
    )(sink, p, p, p, p, p, p, p, p, p, p, p)


NA_Q_ROWS = 8
NA_K_ROWS = 16


def _na_row_windows(group, n_rows):
    lead = (NA_K_ROWS - NA_Q_ROWS) // 2
    rq = group * NA_Q_ROWS + np.arange(NA_Q_ROWS)
    rk = group * NA_Q_ROWS - lead + np.arange(NA_K_ROWS)
    r0 = np.clip(rq - NA_ROWS // 2, 0, n_rows - NA_ROWS)
    valid = (rk[None, :] >= r0[:, None]) & (rk[None, :] < r0[:, None] + NA_ROWS)
    roff = rk[None, :] - rq[:, None] + (NA_ROWS - 1)
    return valid, roff


def _na_fill_bias(bt_ref, mb_ref, group, n_rows):
    valid, roff = _na_row_windows(group, n_rows)
    w = GRID_W
    left = lax.broadcasted_iota(jnp.int32, (w, 2 * w), 1) < w
    neg = jnp.full((w, 2 * w), NEG_INF, F32)
    for i in range(NA_Q_ROWS):
        for u in range(0, NA_K_ROWS, 2):
            lo = bt_ref[int(roff[i, u])] if valid[i, u] else neg
            hi = bt_ref[int(roff[i, u + 1])] if valid[i, u + 1] else neg
            blk = neg if not (valid[i, u] or valid[i, u + 1]) else jnp.where(left, lo, hi)
            mb_ref[i * w:(i + 1) * w, u * w:(u + 2) * w] = blk


def _na_kernel(q, kp, km, kn, vp, vm, vn, kc, vc, bt, o, kcat, vcat, mb, *, n_rows):
    a = pl.program_id(1)
    ng = n_rows // NA_Q_ROWS
    for group in sorted({0, min(1, ng - 1), ng - 1}):
        @pl.when(a == group)
        def _(group=group):
            _na_fill_bias(bt, mb, group, n_rows)

    half = (NA_K_ROWS - NA_Q_ROWS) // 2 * GRID_W
    main = NA_Q_ROWS * GRID_W
    kcat[0:half] = kp[...]
    kcat[half:half + main] = km[...]
    kcat[half + main:2 * half + main] = kn[...]
    vcat[0:half] = vp[...]
    vcat[half:half + main] = vm[...]
    vcat[half + main:2 * half + main] = vn[...]
    scale = HEAD_DIM ** -0.5
    qc = NA_Q_ROWS // 2
    for c in range(2):
        rows = slice(c * qc * GRID_W, (c + 1) * qc * GRID_W)
        keys = slice(c * qc * GRID_W, (c * qc + qc + NA_ROWS) * GRID_W)
        qv = q[rows, :]
        s_loc = _dot_nt(qv, kcat[keys, :]) * scale + mb[rows, keys]
        s_ctx = _dot_nt(qv, kc[...]) * scale
        o[rows, :] = _softmax_pv(s_loc, s_ctx, None, vcat[keys, :], vc[...]).astype(BF16)


def _na_col_table(rpb):
    w = GRID_W
    cq = np.arange(w)
    c0 = np.clip(cq - NA_COLS // 2, 0, w - NA_COLS)
    cmask = (cq[None, :] >= c0[:, None]) & (cq[None, :] < c0[:, None] + NA_COLS)
    coff = np.clip(cq[None, :] - cq[:, None] + (NA_COLS - 1), 0, 2 * NA_COLS - 2)
    onehot = (coff[None] == np.arange(2 * NA_COLS - 1)[:, None, None]).astype(np.float32)
    picked = jnp.einsum("hrj,jqk->hrqk", rpb.astype(F32), onehot, precision=lax.Precision.HIGHEST)
    by_col = jnp.where(cmask[None, None], picked, NEG_INF)
    return jnp.concatenate([by_col, by_col], axis=-1)


def _na_call(p, rpb, *, n_lat, n_ctx):
    main = NA_Q_ROWS * GRID_W
    half = (NA_K_ROWS - NA_Q_ROWS) // 2 * GRID_W
    assert n_lat % main == 0 and n_lat % n_ctx == 0 and main == 2 * half
    ng = n_lat // main
    assert ng >= 2
    nhalf = n_lat // half
    ctx_blk = n_lat // n_ctx
    bt = _na_col_table(rpb)
    kern = functools.partial(_na_kernel, n_rows=n_lat // GRID_W)

    def band_specs(col0):
        return [
            pl.BlockSpec((half, HEAD_DIM), lambda h, a: (jnp.maximum(2 * a - 1, 0), col0 + h)),
            pl.BlockSpec((main, HEAD_DIM), lambda h, a: (a, col0 + h)),
            pl.BlockSpec((half, HEAD_DIM), lambda h, a: (jnp.minimum(2 * a + 2, nhalf - 1), col0 + h)),
        ]

    return pl.pallas_call(
        kern,
        grid=(C_HEADS, ng),
        in_specs=[pl.BlockSpec((main, HEAD_DIM), lambda h, a: (a, QC_COL + h))]
        + band_specs(KC_COL) + band_specs(VC_COL)
        + [pl.BlockSpec((n_ctx, HEAD_DIM), lambda h, a: (ctx_blk, KC_COL + h)),
           pl.BlockSpec((n_ctx, HEAD_DIM), lambda h, a: (ctx_blk, VC_COL + h)),
           pl.BlockSpec((None,) + bt.shape[1:], lambda h, a: (h, 0, 0, 0))],
        out_specs=pl.BlockSpec((main, HEAD_DIM), lambda h, a: (a, h)),
        out_shape=jax.ShapeDtypeStruct((n_lat, C_HEADS * HEAD_DIM), BF16),
        scratch_shapes=[pltpu.VMEM((main + 2 * half, HEAD_DIM), BF16)] * 2
        + [pltpu.VMEM((main, NA_K_ROWS * GRID_W), F32)],
        compiler_params=_params(("arbitrary", "arbitrary")),
        name="neighbourhood_attention",
    )(p, p, p, p, p, p, p, p, p, bt)


def _ctx_attn_kernel(sink_ref, q, k, v, o):
    hh = pl.program_id(0)
    scale = HEAD_DIM ** -0.5
    s = _dot_nt(q[...], k[...]) * scale
    sink = jnp.full((s.shape[0], 1), sink_ref[hh], F32)
    m = jnp.maximum(jnp.max(s, axis=1, keepdims=True), sink)
    pr = jnp.exp(s - m)
    den = jnp.sum(pr, axis=1, keepdims=True) + jnp.exp(sink - m)
    o[...] = (_dot(pr.astype(BF16), v[...]) / den).astype(BF16)


def _ctx_attn_call(p, sink12, *, n_lat, n_ctx):
    blk = n_lat // n_ctx
    g = B_HEADS // B_KV_HEADS

    def kcol(hh):
        return jnp.where(hh < B_HEADS, KB_COL + hh // g, KC_COL + hh - B_HEADS)

    def vcol(hh):
        return jnp.where(hh < B_HEADS, VB_COL + hh // g, VC_COL + hh - B_HEADS)

    return pl.pallas_call(
        _ctx_attn_kernel,
        grid=(B_HEADS + C_HEADS,),
        in_specs=[
            pl.BlockSpec(memory_space=pltpu.SMEM),
            pl.BlockSpec((n_ctx, HEAD_DIM), lambda hh: (blk, QB_COL + hh)),
            pl.BlockSpec((n_ctx, HEAD_DIM), lambda hh: (blk, kcol(hh))),
            pl.BlockSpec((n_ctx, HEAD_DIM), lambda hh: (blk, vcol(hh))),
        ],
        out_specs=pl.BlockSpec((n_ctx, HEAD_DIM), lambda hh: (0, hh)),
        out_shape=jax.ShapeDtypeStruct((n_ctx, (B_HEADS + C_HEADS) * HEAD_DIM), BF16),
        compiler_params=_params(("parallel",)),
        name="context_attention",
    )(sink12, p, p, p)


def _outproj_kernel(ya_ref, yb_ref, yc_ref, w_ref, x_ref, g_ref, gate_ref, o_ref, *, tm, n_lat):
    i = pl.program_id(0)
    ka = ya_ref.shape[1]
    kb = yb_ref.shape[1]
    rc = tm // 2
    for c in range(2):
        rows = slice(c * rc, (c + 1) * rc)
        acc = _dot(ya_ref[rows, :], w_ref[0:ka, :])
        acc += _dot(yb_ref[rows, :], w_ref[ka:ka + kb, :])
        acc += _dot(yc_ref[rows, :], w_ref[ka + kb:, :])
        r = _rms(acc) * g_ref[...]
        o_ref[rows, :] = x_ref[rows, :] + _pick_rows(gate_ref[...], i * tm + c * rc, rc, n_lat) * r


def _outproj_call(ya, yb, yc, w_all_bf16, layer, xa, g, gate2, *, rows, tm, n_lat):
    d = xa.shape[1]
    assert rows % tm == 0 and ya.shape[1] + yb.shape[1] + yc.shape[1] == d
    kern = functools.partial(_outproj_kernel, tm=tm, n_lat=n_lat)
    return pl.pallas_call(
        kern,
        grid=(rows // tm,),
        in_specs=[
            pl.BlockSpec((tm, ya.shape[1]), lambda i: (i, 0)),
            pl.BlockSpec((tm, yb.shape[1]), lambda i: (i, 0)),
            pl.BlockSpec((tm, yc.shape[1]), lambda i: (i, 0)),
            pl.BlockSpec((None, d, d), lambda i: (layer, 0, 0)),
            pl.BlockSpec((tm, d), lambda i: (i, 0)),
            pl.BlockSpec((1, d), lambda i: (0, 0)),
            pl.BlockSpec((2, d), lambda i: (0, 0)),
        ],
        out_specs=pl.BlockSpec((tm, d), lambda i: (i, 0)),
        out_shape=jax.ShapeDtypeStruct((rows, d), F32),
        compiler_params=_params(("parallel",)),
        name="out_projection",
    )(ya, yb, yc, w_all_bf16, xa, g.reshape(1, d), gate2)


FFN_TF = 256


def _ffn_weight_specs(d, tf, nf, expert_of, chunk_of=lambda m, f, *pf: f, down_chunk_of=None):
    down_chunk_of = down_chunk_of or chunk_of
    return [
        pl.BlockSpec((None, d, tf), lambda m, f, *pf: (expert_of(m, *pf), 0, chunk_of(m, f, *pf))),
        pl.BlockSpec((None, d, tf), lambda m, f, *pf: (expert_of(m, *pf), 0, nf + chunk_of(m, f, *pf))),
        pl.BlockSpec((None, tf, d), lambda m, f, *pf: (expert_of(m, *pf), down_chunk_of(m, f, *pf), 0)),
    ]


def _ffn_dense_kernel(x_ref, gpre_ref, sh_ref, sc_ref, wg_ref, wu_ref, wd_ref, gpost_ref, gate_ref,
                      o_ref, h_ref, act_ref, *, tm, n_lat):
    m = pl.program_id(0)
    f = pl.program_id(1)
    nf = pl.num_programs(1) - 1

    def gate_up():
        h = h_ref[...]
        gt = _dot(h, wg_ref[...].astype(BF16))
        up = _dot(h, wu_ref[...].astype(BF16))
        return (jax.nn.silu(gt) * up).astype(BF16)

    def down(act):
        o_ref[...] += _dot(act, wd_ref[...].astype(BF16))

    @pl.when(f == 0)
    def _():
        h = _norm_mod(x_ref[...], gpre_ref[...], sh_ref[...], sc_ref[...], m * tm, tm, n_lat)
        h_ref[...] = h.astype(BF16)
        o_ref[...] = jnp.zeros_like(o_ref)
        act_ref[0] = gate_up()

    @pl.when((f > 0) & (f < nf))
    def _():
        prev = act_ref[(f - 1) % 2]
        act_ref[f % 2] = gate_up()
        down(prev)

    @pl.when(f == nf)
    def _():
        down(act_ref[(f - 1) % 2])
        r = _rms(o_ref[...]) * gpost_ref[...]
        o_ref[...] = x_ref[...] + _pick_rows(gate_ref[...], m * tm, tm, n_lat) * r


def _ffn_dense_call(xa, gpre, sh2, sc2, w_gu, w_down, layer_set, gpost, gate2, *, tm, n_lat):
    t, d = xa.shape
    ffn = w_gu.shape[2] // 2
    tf = FFN_TF
    nf = ffn // tf
    assert t % tm == 0 and ffn % tf == 0
    kern = functools.partial(_ffn_dense_kernel, tm=tm, n_lat=n_lat)
    vec = lambda rows: pl.BlockSpec((rows, d), lambda m, f: (0, 0))
    return pl.pallas_call(
        kern,
        grid=(t // tm, nf + 1),
        in_specs=[pl.BlockSpec((tm, d), lambda m, f: (m, 0), pipeline_mode=pl.Buffered(1)),
                  vec(1), vec(2), vec(2)]
        + _ffn_weight_specs(d, tf, nf, lambda m: layer_set,
                            chunk_of=lambda m, f: jnp.minimum(f, nf - 1),
                            down_chunk_of=lambda m, f: jnp.maximum(f - 1, 0)) + [vec(1), vec(2)],
        out_specs=pl.BlockSpec((tm, d), lambda m, f: (m, 0)),
        out_shape=jax.ShapeDtypeStruct((t, d), F32),
        scratch_shapes=[pltpu.VMEM((tm, d), BF16), pltpu.VMEM((2, tm, tf), BF16)],
        compiler_params=_params(("parallel", "arbitrary")),
        name="swiglu_ffn",
    )(xa, gpre.reshape(1, d), sh2, sc2, w_gu, w_gu, w_down, gpost.reshape(1, d), gate2)


MOE_SB = 512
MOE_NSB = 4


DISPATCH_TM = 512


def _dispatch_kernel(pos_ref, pad0_ref, padn_ref, tail_ref, x_ref, gpre_ref, sh_ref, sc_ref, xs_hbm, hbuf, sems,
                     *, tm, sb):
    i = pl.program_id(0)
    last = pl.num_programs(0) - 1
    h = _rms(x_ref[...]) * gpre_ref[...]
    hbuf[i % 2] = h * (1.0 + sc_ref[0:1, :]) + sh_ref[0:1, :]

    def copy(step, t, k):
        p = pos_ref[(step * tm + t) * 2 + k]
        return pltpu.make_async_copy(hbuf.at[step % 2, pl.ds(t, 1)], xs_hbm.at[pl.ds(p, 1)], sems.at[step % 2])

    def issue(t, carry):
        copy(i, t, 0).start(priority=0)
        copy(i, t, 1).start(priority=1)
        return carry

    lax.fori_loop(0, tm, issue, 0, unroll=8)

    def drain_step(step):
        def drain(t, carry):
            copy(step, t, 0).wait()
            copy(step, t, 1).wait()
            return carry

        lax.fori_loop(0, tm, drain, 0, unroll=8)

    @pl.when(i > 0)
    def _():
        drain_step(i - 1)

    @pl.when(i == last)
    def _():
        drain_step(i)
        hbuf[0] = jnp.zeros(hbuf.shape[1:], F32)
        def zero_row(e, r):
            return pltpu.make_async_copy(hbuf.at[0, pl.ds(0, 1)], xs_hbm.at[pl.ds(pad0_ref[e] + r, 1)], sems.at[0])

        for e in range(N_EXPERTS):
            lax.fori_loop(0, padn_ref[e], lambda r, c, e=e: (zero_row(e, r).start(), c)[1], 0)
        for e in range(N_EXPERTS):
            lax.fori_loop(0, padn_ref[e], lambda r, c, e=e: (zero_row(e, r).wait(), c)[1], 0)

        def zero_block(g, carry):
            row = pl.multiple_of(tail_ref[0] + g * sb, sb)
            cp = pltpu.make_async_copy(hbuf.at[0, pl.ds(0, sb)], xs_hbm.at[pl.ds(row, sb)], sems.at[0])
            cp.start()
            cp.wait()
            return carry

        lax.fori_loop(0, (xs_hbm.shape[0] - tail_ref[0]) // sb, zero_block, 0)


def _dispatch_call(x, pos, pad0, padn, tail, gpre, sh2, sc2, rows_out):
    t, d = x.shape
    tm = DISPATCH_TM
    assert t % tm == 0 and tm >= MOE_SB
    kern = functools.partial(_dispatch_kernel, tm=tm, sb=MOE_SB)
    vec = lambda r: pl.BlockSpec((r, d), lambda i, *pf: (0, 0))
    grid_spec = pltpu.PrefetchScalarGridSpec(
        num_scalar_prefetch=4,
        grid=(t // tm,),
        in_specs=[pl.BlockSpec((tm, d), lambda i, *pf: (i, 0)), vec(1), vec(2), vec(2)],
        out_specs=pl.BlockSpec(memory_space=pl.ANY),
        scratch_shapes=[pltpu.VMEM((2, tm, d), F32), pltpu.SemaphoreType.DMA((2,))],
    )
    return pl.pallas_call(
        kern,
        grid_spec=grid_spec,
        out_shape=jax.ShapeDtypeStruct((rows_out, d), F32),
        compiler_params=_params(("arbitrary",)),
        name="dispatch",
    )(pos, pad0, padn, tail, x, gpre.reshape(1, d), sh2, sc2)


def _ffn_moe_kernel(te_ref, ns_ref, r0_ref, xs_hbm, wg_ref, wu_ref, wd_ref, o_hbm,
                    stage, h_ref, acc, wgb, wub, wdb, gsem, osem, *, sb, nsb):
    m = pl.program_id(0)
    f = pl.program_id(1)
    n = ns_ref[m]
    n_tiles = pl.num_programs(0)
    cur = m % 2

    def out_copy(mm, s):
        row = pl.multiple_of(r0_ref[mm] + s * sb, sb)
        return pltpu.make_async_copy(acc.at[pl.ds(s * sb, sb)], o_hbm.at[pl.ds(row, sb)], osem)

    def in_copy(mm, s):
        row = pl.multiple_of(r0_ref[mm] + s * sb, sb)
        return pltpu.make_async_copy(xs_hbm.at[pl.ds(row, sb)], stage, gsem)

    def for_valid(count, body):
        for s in range(nsb):
            @pl.when(s < count)
            def _(s=s):
                body(s)

    @pl.when(f == 0)
    def _():
        @pl.when(m == 0)
        def _():
            def load(s):
                in_copy(m, s).start()
                in_copy(m, s).wait()
                h_ref[0, s * sb:(s + 1) * sb, :] = stage[...].astype(BF16)

            for_valid(n, load)

        @pl.when(m > 0)
        def _():
            for_valid(ns_ref[jnp.maximum(m - 1, 0)], lambda s: out_copy(m - 1, s).wait())

        def clear(s):
            acc[s * sb:(s + 1) * sb, :] = jnp.zeros((sb, acc.shape[1]), F32)

        for_valid(n, clear)

    nxt = jnp.minimum(m + 1, n_tiles - 1)
    n_next = jnp.where(m + 1 < n_tiles, ns_ref[nxt], 0)
    s_next = f // 2

    @pl.when((f % 2 == 0) & (s_next < n_next))
    def _():
        in_copy(nxt, s_next).start()

    @pl.when((f % 2 == 1) & (s_next < n_next))
    def _():
        in_copy(nxt, s_next).wait()
        h_ref[1 - cur, pl.ds(pl.multiple_of(s_next * sb, sb), sb), :] = stage[...].astype(BF16)

    def accumulate(subs):
        if 0 in subs:
            wg = wg_ref[...].astype(BF16)
            wu = wu_ref[...].astype(BF16)
            wd = wd_ref[...].astype(BF16)
            wgb[...] = wg
            wub[...] = wu
            wdb[...] = wd
        else:
            wg, wu, wd = wgb[...], wub[...], wdb[...]
        for s in subs:
            h = h_ref[cur, s * sb:(s + 1) * sb, :]
            act = (jax.nn.silu(_dot(h, wg)) * _dot(h, wu)).astype(BF16)
            acc[s * sb:(s + 1) * sb, :] += _dot(act, wd)

    for s0 in range(0, nsb, 2):
        if s0 + 1 < nsb:
            @pl.when(s0 + 1 < n)
            def _(s0=s0):
                accumulate((s0, s0 + 1))

        @pl.when(s0 + 1 == n)
        def _(s0=s0):
            accumulate((s0,))

    @pl.when(f == pl.num_programs(1) - 1)
    def _():
        for_valid(n, lambda s: out_copy(m, s).start())

        @pl.when(m == n_tiles - 1)
        def _():
            for_valid(n, lambda s: out_copy(m, s).wait())
            stage[...] = jnp.zeros(stage.shape, F32)

            def fill(g, carry):
                row = pl.multiple_of(r0_ref[n_tiles] + g * sb, sb)
                cp = pltpu.make_async_copy(stage, o_hbm.at[pl.ds(row, sb)], osem)
                cp.start()
                cp.wait()
                return carry

            lax.fori_loop(0, (o_hbm.shape[0] - r0_ref[n_tiles]) // sb, fill, 0)


def _ffn_moe_call(xs, tile_expert, tile_nsb, tile_row0, w_gu, w_down):
    rows, d = xs.shape
    sb, nsb = MOE_SB, MOE_NSB
    n_tiles = tile_expert.shape[0]
    ffn = w_gu.shape[2] // 2
    tf = FFN_TF
    nf = ffn // tf
    assert rows % sb == 0 and tile_row0.shape[0] == n_tiles + 1 and ffn % tf == 0 and nf >= 2 * nsb
    kern = functools.partial(_ffn_moe_kernel, sb=sb, nsb=nsb)
    grid_spec = pltpu.PrefetchScalarGridSpec(
        num_scalar_prefetch=3,
        grid=(n_tiles, nf),
        in_specs=[pl.BlockSpec(memory_space=pl.ANY)]
        + _ffn_weight_specs(d, tf, nf, lambda m, te, *pf: te[m],
                            lambda m, f, te, ns, *pf: jnp.where(ns[m] > 0, f, nf - 1)),
        out_specs=pl.BlockSpec(memory_space=pl.ANY),
        scratch_shapes=[pltpu.VMEM((sb, d), F32), pltpu.VMEM((2, sb * nsb, d), BF16),
                        pltpu.VMEM((sb * nsb, d), F32),
                        pltpu.VMEM((d, tf), BF16), pltpu.VMEM((d, tf), BF16), pltpu.VMEM((tf, d), BF16),
                        pltpu.SemaphoreType.DMA(()), pltpu.SemaphoreType.DMA(())],
    )
    return pl.pallas_call(
        kern,
        grid_spec=grid_spec,
        out_shape=jax.ShapeDtypeStruct((rows, d), F32),
        compiler_params=_params(("arbitrary", "arbitrary")),
        name="expert_ffn",
    )(tile_expert, tile_nsb, tile_row0, xs, w_gu, w_gu, w_down)


ROUTER_TM = 512
R_E1, R_E2, R_W1, R_W2, R_RANK1, R_RANK2 = range(6)


def _router_kernel(x_ref, gpre_ref, sh_ref, sc_ref, wr_ref, route_ref, count_ref, carry_ref, *, tm):
    i = pl.program_id(0)

    @pl.when(i == 0)
    def _():
        carry_ref[...] = jnp.zeros_like(carry_ref)

    h = _rms(x_ref[...]) * gpre_ref[...]
    h = h * (1.0 + sc_ref[0:1, :]) + sh_ref[0:1, :]
    w = wr_ref[...]
    h_hi = h.astype(BF16)
    h_lo = (h - h_hi.astype(F32)).astype(BF16)
    w_hi = w.astype(BF16)
    w_lo = (w - w_hi.astype(F32)).astype(BF16)
    logits = _dot(h_hi, w_hi) + (_dot(h_hi, w_lo) + _dot(h_lo, w_hi))
    lane_i = lax.broadcasted_iota(jnp.int32, logits.shape, 1)
    lane = lane_i.astype(F32)
    logits = jnp.where(lane_i < N_EXPERTS, logits, -jnp.inf)
    v1 = jnp.max(logits, axis=1, keepdims=True)
    e1 = jnp.min(jnp.where(logits == v1, lane, float(LANES)), axis=1, keepdims=True)
    rest = jnp.where(lane == e1, -jnp.inf, logits)
    v2 = jnp.max(rest, axis=1, keepdims=True)
    e2 = jnp.min(jnp.where(rest == v2, lane, float(LANES)), axis=1, keepdims=True)
    ex = jnp.exp(v2 - v1)
    w1 = 1.0 / (1.0 + ex)
    w2 = ex / (1.0 + ex)
    hit1 = lane == e1
    hit2 = lane == e2
    assign = jnp.where(hit1, 1.0, jnp.where(hit2, 1.0, 0.0))
    r = lax.broadcasted_iota(jnp.int32, (tm, tm), 0)
    c = lax.broadcasted_iota(jnp.int32, (tm, tm), 1)
    before = jnp.where(c < r, 1.0, 0.0).astype(BF16)
    prefix = _dot(before, assign.astype(BF16)) + carry_ref[0:1, :]
    rank1 = jnp.sum(jnp.where(hit1, prefix, 0.0), axis=1, keepdims=True)
    rank2 = jnp.sum(jnp.where(hit2, prefix, 0.0), axis=1, keepdims=True)
    total = carry_ref[0:1, :] + jnp.sum(assign, axis=0, keepdims=True)
    carry_ref[...] = jnp.broadcast_to(total, carry_ref.shape)
    count_ref[...] = jnp.broadcast_to(total, count_ref.shape)
    rec = jnp.zeros(logits.shape, F32)
    for k, val in ((R_E1, e1), (R_E2, e2), (R_W1, w1), (R_W2, w2), (R_RANK1, rank1), (R_RANK2, rank2)):
        rec = jnp.where(lane_i == k, val, rec)
    route_ref[...] = rec


def _router_call(x, gpre, sh2, sc2, w_router):
    t, d = x.shape
    tm = ROUTER_TM
    assert t % tm == 0
    wr = jnp.zeros((d, LANES), F32).at[:, :N_EXPERTS].set(w_router)
    kern = functools.partial(_router_kernel, tm=tm)
    return pl.pallas_call(
        kern,
        grid=(t // tm,),
        in_specs=[
            pl.BlockSpec((tm, d), lambda i: (i, 0)),
            pl.BlockSpec((1, d), lambda i: (0, 0)),
            pl.BlockSpec((2, d), lambda i: (0, 0)),
            pl.BlockSpec((2, d), lambda i: (0, 0)),
            pl.BlockSpec((d, LANES), lambda i: (0, 0)),
        ],
        out_specs=[pl.BlockSpec((tm, LANES), lambda i: (i, 0)),
                   pl.BlockSpec((8, LANES), lambda i: (0, 0))],
        out_shape=[jax.ShapeDtypeStruct((t, LANES), F32), jax.ShapeDtypeStruct((8, LANES), F32)],
        scratch_shapes=[pltpu.VMEM((8, LANES), F32)],
        compiler_params=_params(("arbitrary",)),
        name="router_top2",
    )(x, gpre.reshape(1, d), sh2, sc2, wr)


COMBINE_TM = 256


def _combine_kernel(pos_ref, x_ref, route_ref, gpost_ref, gate_ref, ys_hbm, o_ref, buf, sem, *, tm):
    i = pl.program_id(0)
    slot = i % 2

    def copy(step, t, k):
        p = pos_ref[(step * tm + t) * 2 + k]
        return pltpu.make_async_copy(ys_hbm.at[pl.ds(p, 1)], buf.at[step % 2, k, pl.ds(t, 1)], sem.at[step % 2])

    def fetch(step):
        def issue(t, carry):
            copy(step, t, 0).start(priority=0)
            copy(step, t, 1).start(priority=1)
            return carry

        lax.fori_loop(0, tm, issue, 0, unroll=8)

    @pl.when(i == 0)
    def _():
        fetch(i)

    @pl.when(i + 1 < pl.num_programs(0))
    def _():
        fetch(i + 1)

    def drain(t, carry):
        copy(i, t, 0).wait()
        copy(i, t, 1).wait()
        return carry

    lax.fori_loop(0, tm, drain, 0, unroll=8)
    rec = route_ref[...]
    y = rec[:, R_W1:R_W1 + 1] * buf[slot, 0] + rec[:, R_W2:R_W2 + 1] * buf[slot, 1]
    o_ref[...] = x_ref[...] + gate_ref[0:1, :] * (_rms(y) * gpost_ref[...])


def _combine_call(x, ys, pos_flat, route, gpost, gate2):
    t, d = x.shape
    tm = COMBINE_TM
    assert t % tm == 0
    kern = functools.partial(_combine_kernel, tm=tm)
    grid_spec = pltpu.PrefetchScalarGridSpec(
        num_scalar_prefetch=1,
        grid=(t // tm,),
        in_specs=[
            pl.BlockSpec((tm, d), lambda i, pos: (i, 0)),
            pl.BlockSpec((tm, LANES), lambda i, pos: (i, 0)),
            pl.BlockSpec((1, d), lambda i, pos: (0, 0)),
            pl.BlockSpec((2, d), lambda i, pos: (0, 0)),
            pl.BlockSpec(memory_space=pl.ANY),
        ],
        out_specs=pl.BlockSpec((tm, d), lambda i, pos: (i, 0)),
        scratch_shapes=[pltpu.VMEM((2, 2, tm, d), F32), pltpu.SemaphoreType.DMA((2,))],
    )
    return pl.pallas_call(
        kern,
        grid_spec=grid_spec,
        out_shape=jax.ShapeDtypeStruct((t, d), F32),
        compiler_params=_params(("arbitrary",)),
        name="combine",
    )(pos_flat, x, route, gpost.reshape(1, d), gate2, ys)


DENSE_TM = 768
LAT_TM = 1024
DENSE_FFN_TM = 1056


def _moe_plan(route, counts):
    t = route.shape[0]
    e = N_EXPERTS
    sb, nsb = MOE_SB, MOE_NSB
    i32 = jnp.int32
    cnt = counts[0, :e].astype(i32)
    nsub = (cnt + sb - 1) // sb
    row_end = jnp.cumsum(nsub) * sb
    row_start = row_end - nsub * sb
    ntile = (nsub + nsb - 1) // nsb
    tile_end = jnp.cumsum(ntile)
    tile_start = tile_end - ntile
    ex = route[:, R_E1:R_E2 + 1].astype(i32)
    rank = route[:, R_RANK1:R_RANK2 + 1].astype(i32)
    pos = (row_start[ex] + rank).reshape(-1)
    max_sub = (2 * t + e * (sb - 1)) // sb
    max_tiles = (max_sub + e * (nsb - 1)) // nsb
    m = jnp.arange(max_tiles + 1, dtype=i32)
    used = m < tile_end[-1]
    m_used = jnp.minimum(m, tile_end[-1] - 1)
    te = jnp.minimum(jnp.sum((tile_end[None, :] <= m_used[:, None]).astype(i32), axis=1), e - 1)
    local = m_used - tile_start[te]
    per = nsub[te] // jnp.maximum(ntile[te], 1)
    extra = nsub[te] - per * ntile[te]
    tile_nsb = jnp.where(used, per + (local < extra).astype(i32), 0)
    first_sub = local * per + jnp.minimum(local, extra)
    tile_row0 = jnp.where(used, row_start[te] + first_sub * sb, row_end[-1])
    return dict(pos=pos, tile_expert=te[:-1].astype(i32), tile_nsb=tile_nsb[:-1].astype(i32),
                tile_row0=tile_row0.astype(i32), pad_start=(row_start + cnt).astype(i32),
                pad_len=(nsub * sb - cnt).astype(i32), tail=row_end[-1:].astype(i32), rows=max_sub * sb)


def kernel(x, c, ctx, c_ctx, w_mod, b_mod, g_mix_pre, g_mix_post, g_ffn_pre, g_ffn_post, w_in, sgu_ln_g,
           sgu_w, sgu_b, attn_sink, na_rpb, w_out, ffn_w_gu, ffn_w_down, moe_router, moe_w_gu, moe_w_down):
    assert x.shape[0] == 1 and ctx.shape[0] == 1
    n_lat, d = x.shape[1], x.shape[2]
    n_ctx = ctx.shape[1]
    depth = w_mod.shape[0]
    t_all = n_lat + n_ctx

    c8 = jnp.zeros((8, d), F32).at[0].set(c[0]).at[1].set(c_ctx)
    mod = _mod_call(c8, w_mod, b_mod)
    cos, sin = _rope_tables(n_lat, n_ctx)
    w_in_bf16 = w_in.astype(BF16)
    w_out_bf16 = w_out.astype(BF16)
    nb = B_HEADS * HEAD_DIM

    xa = jnp.concatenate([x[0], ctx[0]], axis=0)
    for l in range(depth):
        last = l == depth - 1
        m2 = mod[l, 0:2]
        sh1, sc1, g1, sh2, sc2, g2 = (m2[:, k * d:(k + 1) * d] for k in range(6))

        p = _inproj_call(xa, g_mix_pre[l], sh1, sc1, w_in_bf16, l, cos, sin, tm=DENSE_TM, n_lat=n_lat)
        y_sgu = _sgu_call(p, sgu_ln_g[l], sgu_w[l], sgu_b[l], chunks=DENSE_TM // CHUNK)
        y_win = _win_call(p, attn_sink[l], n_lat=n_lat, n_ctx=n_ctx)
        y_na = _na_call(p, na_rpb[l], n_lat=n_lat, n_ctx=n_ctx)
        if not last:
            sink12 = jnp.concatenate([attn_sink[l], jnp.full((C_HEADS,), NEG_INF, F32)])
            y_ctx = _ctx_attn_call(p, sink12, n_lat=n_lat, n_ctx=n_ctx)
            y_win = jnp.concatenate([y_win, y_ctx[:, :nb]], axis=0)
            y_na = jnp.concatenate([y_na, y_ctx[:, nb:]], axis=0)
            xa = _outproj_call(y_sgu, y_win, y_na, w_out_bf16, l, xa, g_mix_post[l], g1,
                               rows=t_all, tm=DENSE_TM, n_lat=n_lat)
        else:
            xa = _outproj_call(y_sgu, y_win, y_na, w_out_bf16, l, xa, g_mix_post[l], g1,
                               rows=n_lat, tm=LAT_TM, n_lat=n_lat)

        if l % 2 == 0:
            xa = _ffn_dense_call(xa, g_ffn_pre[l], sh2, sc2, ffn_w_gu, ffn_w_down, l // 2,
                                 g_ffn_post[l], g2, tm=DENSE_FFN_TM, n_lat=n_lat)
        else:
            assert last, "expert layers are only supported as the last layer (latent rows only)"
            route, counts = _router_call(xa, g_ffn_pre[l], sh2, sc2, moe_router[l // 2])
            plan = _moe_plan(route, counts)
            xs = _dispatch_call(xa, plan["pos"], plan["pad_start"], plan["pad_len"], plan["tail"],
                                g_ffn_pre[l], sh2, sc2, plan["rows"])
            ys = _ffn_moe_call(xs, plan["tile_expert"], plan["tile_nsb"], plan["tile_row0"],
                               moe_w_gu[l // 2], moe_w_down[l // 2])
            xa = _combine_call(xa, ys, plan["pos"], route, g_ffn_post[l], g2)
    return xa[:n_lat][None]
```
